```python
import math
import jax, jax.numpy as jnp
from jax import lax
import numpy as np

D_MODEL = 4096
BATCH = 2
SEQ = 8192
DEPTH = 4

CHUNK = 64
Q_BLOCK = 128
MEM_LEN = 256
EPS = 1e-6
ROPE_THETA = 10000.0

A_HEADS = 16
A_NOPE = 128
A_ROPE = 64
A_QK = A_NOPE + A_ROPE
A_V = 128
A_WIDTH = A_HEADS * A_V
Q_LORA = 1024
KV_LORA = 512
B_HEADS = 8
B_DIM = 128
B_WIDTH = B_HEADS * B_DIM
M_HEADS = 4
M_DIM = 256
M_WIDTH = M_HEADS * M_DIM
N_BRANCH = 3

IN_SPLITS = (Q_LORA, KV_LORA, A_ROPE, A_WIDTH,
             3 * B_WIDTH, B_WIDTH,
             M_WIDTH, M_WIDTH,
             N_BRANCH * D_MODEL)
IN_COLS = sum(IN_SPLITS)

kernel_name = "hybrid_mla_stickbreak_mem_gated"


def rms_norm(x, g):
    xf = x.astype(jnp.float32)
    y = xf * lax.rsqrt(jnp.mean(xf * xf, axis=-1, keepdims=True) + EPS)
    return (y * g.astype(jnp.float32)).astype(x.dtype)


def rope(x, positions):
    half = x.shape[-1] // 2
    freqs = ROPE_THETA ** (-jnp.arange(half, dtype=jnp.float32) / half)
    ang = positions.astype(jnp.float32)[..., None] * freqs
    cos = jnp.cos(ang)[:, :, None, :]
    sin = jnp.sin(ang)[:, :, None, :]
    x1 = x[..., :half].astype(jnp.float32)
    x2 = x[..., half:].astype(jnp.float32)
    out = jnp.concatenate([x1 * cos - x2 * sin, x2 * cos + x1 * sin], axis=-1)
    return out.astype(x.dtype)


def to_blocks(t):
    b, s, h, d = t.shape
    return t.reshape(b, s // Q_BLOCK, Q_BLOCK, h, d).transpose(1, 0, 3, 2, 4)


def from_blocks(t):
    n, b, h, qb, d = t.shape
    return t.transpose(1, 0, 3, 2, 4).reshape(b, n * qb, h, d)


def chunk_causal_softmax_attention(q, k, v, scale):
    s_len = k.shape[1]
    kh = k.transpose(0, 2, 1, 3)
    vh = v.transpose(0, 2, 1, 3)
    k_chunk = jnp.arange(s_len) // CHUNK
    starts = jnp.arange(s_len // Q_BLOCK) * Q_BLOCK

    def one_block(args):
        qblk, start = args
        q_chunk = (start + jnp.arange(Q_BLOCK)) // CHUNK
        sc = jnp.einsum('bhqd,bhkd->bhqk', qblk, kh,
                        preferred_element_type=jnp.float32) * scale
        sc = jnp.where(k_chunk[None, :] <= q_chunk[:, None], sc, -jnp.inf)
        p = jax.nn.softmax(sc, axis=-1)
        return jnp.einsum('bhqk,bhkd->bhqd', p.astype(vh.dtype), vh)

    return from_blocks(lax.map(one_block, (to_blocks(q), starts)))


def stick_breaking_attention(q, k, v):
    s_len = k.shape[1]
    scale = 1.0 / math.sqrt(q.shape[-1])
    kh = k.transpose(0, 2, 1, 3)
    vh = v.transpose(0, 2, 1, 3)
    k_idx = jnp.arange(s_len)
    starts = jnp.arange(s_len // Q_BLOCK) * Q_BLOCK

    def one_block(args):
        qblk, start = args
        q_idx = start + jnp.arange(Q_BLOCK)
        z = jnp.einsum('bhqd,bhkd->bhqk', qblk, kh,
                       preferred_element_type=jnp.float32) * scale
        before = k_idx[None, :] < q_idx[:, None]
        log_beta = jax.nn.log_sigmoid(z)
        log_fail = jnp.where(before, jax.nn.log_sigmoid(-z), 0.0)
        later_fail = lax.cumsum(log_fail, axis=3, reverse=True) - log_fail
        w = jnp.where(before, jnp.exp(log_beta + later_fail), 0.0)
        return jnp.einsum('bhqk,bhkd->bhqd', w.astype(vh.dtype), vh)

    return from_blocks(lax.map(one_block, (to_blocks(q), starts)))


def memory_cross_attention(q, k, v):
    scale = 1.0 / math.sqrt(q.shape[-1])
    sc = jnp.einsum('bshd,bmhd->bhsm', q, k, preferred_element_type=jnp.float32) * scale
    p = jax.nn.softmax(sc, axis=-1)
    return jnp.einsum('bhsm,bmhd->bshd', p.astype(v.dtype), v)


def hybrid_layer(x, mem, positions, g_pre, w_in, g_q_lat, w_uq, g_kv_lat, w_ukv,
                 g_qn_a, g_kn_a, w_pa, w_pb, g_mem, w_mkv, g_qn_m, g_kn_m, w_pm, w_out):
    b, s, _ = x.shape
    h = rms_norm(x, g_pre)
    proj = h @ w_in
    cuts = [int(c) for c in np.cumsum(IN_SPLITS)[:-1]]
    c_q, c_kv, k_r, gate_a, qkv_b, gate_b, q_m, gate_m, merge = jnp.split(proj, cuts, axis=-1)

    q_a = (rms_norm(c_q, g_q_lat) @ w_uq).reshape(b, s, A_HEADS, A_QK)
    kv_a = (rms_norm(c_kv, g_kv_lat) @ w_ukv).reshape(b, s, A_HEADS, A_NOPE + A_V)
    k_nope, v_a = kv_a[..., :A_NOPE], kv_a[..., A_NOPE:]
    k_rope_shared = jnp.broadcast_to(k_r[:, :, None, :], (b, s, A_HEADS, A_ROPE))
    k_a = jnp.concatenate([k_nope, k_rope_shared], axis=-1)
    q_a = rms_norm(q_a, g_qn_a)
    k_a = rms_norm(k_a, g_kn_a)
    q_a = jnp.concatenate([q_a[..., :A_NOPE], rope(q_a[..., A_NOPE:], positions)], axis=-1)
    k_a = jnp.concatenate([k_a[..., :A_NOPE], rope(k_a[..., A_NOPE:], positions)], axis=-1)
    o_a = chunk_causal_softmax_attention(q_a, k_a, v_a, 1.0 / math.sqrt(A_QK))
    y_a = (o_a.reshape(b, s, A_WIDTH) * jax.nn.silu(gate_a)) @ w_pa

    qkv = qkv_b.reshape(b, s, 3, B_HEADS, B_DIM)
    o_b = stick_breaking_attention(qkv[:, :, 0], qkv[:, :, 1], qkv[:, :, 2])
    y_b = (o_b.reshape(b, s, B_WIDTH) * jax.nn.silu(gate_b)) @ w_pb

    kv_m = (rms_norm(mem, g_mem) @ w_mkv).reshape(b, mem.shape[1], 2, M_HEADS, M_DIM)
    q_mh = rms_norm(q_m.reshape(b, s, M_HEADS, M_DIM), g_qn_m)
    k_mh = rms_norm(kv_m[:, :, 0], g_kn_m)
    o_m = memory_cross_attention(q_mh, k_mh, kv_m[:, :, 1])
    y_m = (o_m.reshape(b, s, M_WIDTH) * jax.nn.silu(gate_m)) @ w_pm

    r = jax.nn.sigmoid(merge.reshape(b, s, N_BRANCH, D_MODEL))
    mixed = r[:, :, 0] * y_a + r[:, :, 1] * y_b + r[:, :, 2] * y_m
    return x + mixed @ w_out


def setup_inputs(seed: int = 0) -> dict:
    key = jax.random.key(seed)
    ks = jax.random.split(key, 20)
    f32 = jnp.float32

    def w(k, shape, fan_in):
        return jax.random.normal(k, shape, f32) * (fan_in ** -0.5)

    def gain(k, shape):
        return 1.0 + 0.01 * jax.random.normal(k, shape, f32)

    L = DEPTH
    return {
        "x": jax.random.normal(ks[0], (BATCH, SEQ, D_MODEL), f32),
        "mem": jax.random.normal(ks[1], (BATCH, MEM_LEN, D_MODEL), f32),
        "positions": jnp.broadcast_to(jnp.arange(SEQ, dtype=jnp.int32), (BATCH, SEQ)),
        "g_pre": gain(ks[2], (L, D_MODEL)),
        "w_in": w(ks[3], (L, D_MODEL, IN_COLS), D_MODEL),
        "g_q_lat": gain(ks[4], (L, Q_LORA)),
        "w_uq": w(ks[5], (L, Q_LORA, A_HEADS * A_QK), Q_LORA),
        "g_kv_lat": gain(ks[6], (L, KV_LORA)),
        "w_ukv": w(ks[7], (L, KV_LORA, A_HEADS * (A_NOPE + A_V)), KV_LORA),
        "g_qn_a": gain(ks[8], (L, A_QK)),
        "g_kn_a": gain(ks[9], (L, A_QK)),
        "w_pa": w(ks[10], (L, A_WIDTH, D_MODEL), A_WIDTH),
        "w_pb": w(ks[11], (L, B_WIDTH, D_MODEL), B_WIDTH),
        "g_mem": gain(ks[12], (L, D_MODEL)),
        "w_mkv": w(ks[13], (L, D_MODEL, 2 * M_WIDTH), D_MODEL),
        "g_qn_m": gain(ks[14], (L, M_DIM)),
        "g_kn_m": gain(ks[15], (L, M_DIM)),
        "w_pm": w(ks[16], (L, M_WIDTH, D_MODEL), M_WIDTH),
        "w_out": w(ks[17], (L, D_MODEL, D_MODEL), D_MODEL),
    }


def reference(x, mem, positions, g_pre, w_in, g_q_lat, w_uq, g_kv_lat, w_ukv,
              g_qn_a, g_kn_a, w_pa, w_pb, g_mem, w_mkv, g_qn_m, g_kn_m, w_pm, w_out):
    for l in range(DEPTH):
        x = hybrid_layer(x, mem, positions, g_pre[l], w_in[l], g_q_lat[l], w_uq[l],
                         g_kv_lat[l], w_ukv[l], g_qn_a[l], g_kn_a[l], w_pa[l], w_pb[l],
                         g_mem[l], w_mkv[l], g_qn_m[l], g_kn_m[l], w_pm[l], w_out[l])
    return x
```

```python
import functools
import math

import jax
import jax.numpy as jnp
from jax import lax
from jax.experimental import pallas as pl
from jax.experimental.pallas import tpu as pltpu

F32 = jnp.float32
BF16 = jnp.bfloat16

EPS = 1e-6
ROPE_THETA = 10000.0
CHUNK = 64

A_HEADS, A_NOPE, A_ROPE, A_V = 16, 128, 64, 128
A_QK = A_NOPE + A_ROPE
A_PAD = 256
Q_LORA, KV_LORA = 1024, 512
B_HEADS, B_DIM = 8, 128
M_HEADS, M_DIM = 4, 256
D_MODEL = 4096
N_BRANCH = 3

LANE = 128
VMEM_LIMIT = 56 * 1024 * 1024


class _Cols:
    CQ = 0
    CKV = CQ + Q_LORA
    GA = CKV + KV_LORA
    QKVB = GA + A_HEADS * A_V
    GB = QKVB + 3 * B_HEADS * B_DIM
    QM = GB + B_HEADS * B_DIM
    GM = QM + M_HEADS * M_DIM
    MERGE = GM + M_HEADS * M_DIM
    KR = MERGE + N_BRANCH * D_MODEL
    USED = KR + LANE
    TOTAL = 22528


def _cparams(sem):
    return pltpu.CompilerParams(dimension_semantics=sem, vmem_limit_bytes=VMEM_LIMIT)


def _sigmoid(v):
    return 1.0 / (1.0 + jnp.exp(-v))


def _silu(v):
    return v * _sigmoid(v)


_NT = (((1,), (1,)), ((), ()))


def _rope_table_kernel(pos_ref, freq_ref, o_ref):
    ang = pos_ref[...] * freq_ref[...]
    lane = lax.broadcasted_iota(jnp.int32, ang.shape, 1)
    o_ref[...] = jnp.where(lane < A_ROPE, jnp.cos(ang), jnp.sin(ang))


def _rope_table(pos_b, freq_row, tm):
    m = pos_b.shape[0]
    return pl.pallas_call(
        _rope_table_kernel,
        grid=(m // tm,),
        in_specs=[pl.BlockSpec((tm, LANE), lambda i: (i, 0)),
                  pl.BlockSpec((1, LANE), lambda i: (0, 0))],
        out_specs=pl.BlockSpec((tm, LANE), lambda i: (i, 0)),
        out_shape=jax.ShapeDtypeStruct((m, LANE), F32),
        compiler_params=_cparams(("parallel",)),
        name="rope_table",
    )(pos_b, freq_row)


def _norm_kernel(x_ref, g_ref, o_ref):
    x = x_ref[...]
    y = x * lax.rsqrt(jnp.mean(x * x, axis=-1, keepdims=True) + EPS)
    o_ref[...] = (y * g_ref[...]).astype(o_ref.dtype)


def _norm(x, g_row, tm):
    m, d = x.shape
    return pl.pallas_call(
        _norm_kernel,
        grid=(m // tm,),
        in_specs=[pl.BlockSpec((tm, d), lambda i: (i, 0)),
                  pl.BlockSpec((1, d), lambda i: (0, 0))],
        out_specs=pl.BlockSpec((tm, d), lambda i: (i, 0)),
        out_shape=jax.ShapeDtypeStruct((m, d), BF16),
        compiler_params=_cparams(("parallel",)),
        name="norm",
    )(x, g_row)


def _matmul_kernel(a_ref, w_ref, o_ref):
    o_ref[...] = jnp.dot(a_ref[...], w_ref[...], preferred_element_type=F32).astype(o_ref.dtype)


def _in_proj(h, w, tm, tn):
    m, k = h.shape
    n = w.shape[1]
    return pl.pallas_call(
        _matmul_kernel,
        grid=(m // tm, n // tn),
        in_specs=[pl.BlockSpec((tm, k), lambda i, j: (i, 0)),
                  pl.BlockSpec((k, tn), lambda i, j: (0, j))],
        out_specs=pl.BlockSpec((tm, tn), lambda i, j: (i, j)),
        out_shape=jax.ShapeDtypeStruct((m, n), BF16),
        compiler_params=_cparams(("parallel", "arbitrary")),
        name="in_proj",
    )(h, w)


def _q_prep_kernel(c_ref, gl_ref, w_ref, gh_ref, tbl_ref, o_ref, *, scale):
    c = c_ref[...].astype(F32)
    cn = c * lax.rsqrt(jnp.mean(c * c, axis=-1, keepdims=True) + EPS) * gl_ref[...]
    cn = cn.astype(BF16)
    tbl = tbl_ref[...]
    g1 = gh_ref[:, :A_NOPE]
    g2t = gh_ref[:, A_NOPE:] * tbl
    lane = lax.broadcasted_iota(jnp.int32, tbl.shape, 1)
    for h in range(A_HEADS):
        acc = jnp.dot(cn, w_ref[:, h * A_PAD:(h + 1) * A_PAD], preferred_element_type=F32)
        v1 = acc[:, :A_NOPE]
        v2 = acc[:, A_NOPE:]
        ss = (jnp.sum(v1 * v1, axis=-1, keepdims=True)
              + jnp.sum(jnp.where(lane < A_ROPE, v2 * v2, 0.0), axis=-1, keepdims=True))
        rs = lax.rsqrt(ss * (1.0 / A_QK) + EPS) * scale
        o_ref[:, h * A_PAD:h * A_PAD + A_NOPE] = (v1 * g1 * rs).astype(o_ref.dtype)
        o_ref[:, h * A_PAD + A_NOPE:(h + 1) * A_PAD] = (v2 * g2t * rs).astype(o_ref.dtype)


def _q_prep(proj, gl_row, w_uq, gh_row, tbl, tm):
    m = proj.shape[0]
    n = A_HEADS * A_PAD
    return pl.pallas_call(
        functools.partial(_q_prep_kernel, scale=1.0 / math.sqrt(A_QK)),
        grid=(m // tm,),
        in_specs=[pl.BlockSpec((tm, Q_LORA), lambda i: (i, _Cols.CQ // Q_LORA)),
                  pl.BlockSpec((1, Q_LORA), lambda i: (0, 0)),
                  pl.BlockSpec((Q_LORA, n), lambda i: (0, 0)),
                  pl.BlockSpec((1, A_PAD), lambda i: (0, 0)),
                  pl.BlockSpec((tm, LANE), lambda i: (i, 0))],
        out_specs=pl.BlockSpec((tm, n), lambda i: (i, 0)),
        out_shape=jax.ShapeDtypeStruct((m, n), BF16),
        compiler_params=_cparams(("parallel",)),
        name="q_prep",
    )(proj, gl_row, w_uq, gh_row, tbl)


def _kv_prep_kernel(c_ref, kr_ref, gl_ref, w_ref, gh_ref, tbl_ref, k_ref, v_ref):
    c = c_ref[...].astype(F32)
    cn = c * lax.rsqrt(jnp.mean(c * c, axis=-1, keepdims=True) + EPS) * gl_ref[...]
    cn = cn.astype(BF16)
    kr = kr_ref[...].astype(F32)
    lane = lax.broadcasted_iota(jnp.int32, kr.shape, 1)
    ss_r = jnp.sum(jnp.where(lane < A_ROPE, kr * kr, 0.0), axis=-1, keepdims=True)
    ab = kr * gh_ref[:, A_NOPE:] * tbl_ref[...]
    kk = ab + pltpu.roll(ab, A_ROPE, axis=1)
    g1 = gh_ref[:, :A_NOPE]
    for h in range(A_HEADS):
        acc = jnp.dot(cn, w_ref[:, h * A_PAD:(h + 1) * A_PAD], preferred_element_type=F32)
        kn = acc[:, :A_NOPE]
        ss = jnp.sum(kn * kn, axis=-1, keepdims=True) + ss_r
        rs = lax.rsqrt(ss * (1.0 / A_QK) + EPS)
        k_ref[:, h * A_PAD:h * A_PAD + A_NOPE] = (kn * g1 * rs).astype(k_ref.dtype)
        k_ref[:, h * A_PAD + A_NOPE:(h + 1) * A_PAD] = (kk * rs).astype(k_ref.dtype)
        v_ref[:, h * A_V:(h + 1) * A_V] = acc[:, A_NOPE:].astype(v_ref.dtype)


def _kv_prep(proj, gl_row, w_ukv, gh_row, tbl, tm):
    m = proj.shape[0]
    nk = A_HEADS * A_PAD
    nv = A_HEADS * A_V
    return pl.pallas_call(
        _kv_prep_kernel,
        grid=(m // tm,),
        in_specs=[pl.BlockSpec((tm, KV_LORA), lambda i: (i, _Cols.CKV // KV_LORA)),
                  pl.BlockSpec((tm, LANE), lambda i: (i, _Cols.KR // LANE)),
                  pl.BlockSpec((1, KV_LORA), lambda i: (0, 0)),
                  pl.BlockSpec((KV_LORA, nk), lambda i: (0, 0)),
                  pl.BlockSpec((1, A_PAD), lambda i: (0, 0)),
                  pl.BlockSpec((tm, LANE), lambda i: (i, 0))],
        out_specs=[pl.BlockSpec((tm, nk), lambda i: (i, 0)),
                   pl.BlockSpec((tm, nv), lambda i: (i, 0))],
        out_shape=[jax.ShapeDtypeStruct((m, nk), BF16),
                   jax.ShapeDtypeStruct((m, nv), BF16)],
        compiler_params=_cparams(("parallel",)),
        name="kv_prep",
    )(proj, proj, gl_row, w_ukv, gh_row, tbl)


def _attn_a_kernel(q_ref, k_ref, v_ref, g_ref, o_ref, *, tq):
    qi = pl.program_id(2)
    q = q_ref[...]

    def step(j, carry, diagonal):
        m, l, acc = carry
        start = pl.multiple_of(j * tq, tq)
        k = k_ref[pl.ds(start, tq), :]
        v = v_ref[pl.ds(start, tq), :]
        s = lax.dot_general(q, k, _NT, preferred_element_type=F32)
        if diagonal:
            qc = lax.broadcasted_iota(jnp.int32, s.shape, 0) // CHUNK
            kc = lax.broadcasted_iota(jnp.int32, s.shape, 1) // CHUNK
            s = jnp.where(kc <= qc, s, -jnp.inf)
        m_new = jnp.maximum(m, jnp.max(s, axis=-1, keepdims=True))
        alpha = jnp.exp(m - m_new)
        p = jnp.exp(s - m_new)
        l = alpha * l + jnp.sum(p, axis=-1, keepdims=True)
        acc = alpha * acc + jnp.dot(p.astype(BF16), v, preferred_element_type=F32)
        return m_new, l, acc

    init = (jnp.full((tq, 1), -jnp.inf, F32), jnp.zeros((tq, 1), F32), jnp.zeros((tq, A_V), F32))
    carry = lax.fori_loop(0, qi, lambda j, c: step(j, c, False), init)
    _, l, acc = step(qi, carry, True)
    o = acc / l
    o_ref[...] = (o * _silu(g_ref[...].astype(F32))).astype(o_ref.dtype)


def _attn_a(q_a, k_a, v_a, proj, batch, seq, tq):
    m = q_a.shape[0]
    nq = seq // tq
    gate0 = _Cols.GA // A_V
    return pl.pallas_call(
        functools.partial(_attn_a_kernel, tq=tq),
        grid=(batch, A_HEADS, nq),
        in_specs=[pl.BlockSpec((tq, A_PAD), lambda b, h, i: (b * nq + i, h)),
                  pl.BlockSpec((seq, A_PAD), lambda b, h, i: (b, h)),
                  pl.BlockSpec((seq, A_V), lambda b, h, i: (b, h)),
                  pl.BlockSpec((tq, A_V), lambda b, h, i: (b * nq + i, gate0 + h))],
        out_specs=pl.BlockSpec((tq, A_V), lambda b, h, i: (b * nq + i, h)),
        out_shape=jax.ShapeDtypeStruct((m, A_HEADS * A_V), BF16),
        compiler_params=_cparams(("parallel", "parallel", "arbitrary")),
        name="attn_a",
    )(q_a, k_a, v_a, proj)


def _attn_b_kernel(q_ref, k_ref, v_ref, g_ref, o_ref, *, tq, scale):
    qi = pl.program_id(2)
    q = q_ref[...]
    row = lax.broadcasted_iota(jnp.int32, (tq, tq), 0)
    col = lax.broadcasted_iota(jnp.int32, (tq, tq), 1)
    tri = jnp.where(row > col, 1.0, 0.0).astype(BF16)

    def step(j, carry, diagonal):
        run, acc = carry
        start = pl.multiple_of(j * tq, tq)
        k = k_ref[pl.ds(start, tq), :]
        v = v_ref[pl.ds(start, tq), :]
        z = lax.dot_general(q, k, _NT, preferred_element_type=F32) * scale
        soft = jnp.log1p(jnp.exp(-jnp.abs(z)))
        log_fail = -(jnp.maximum(z, 0.0) + soft)
        log_beta = jnp.minimum(z, 0.0) - soft
        if diagonal:
            before = col < row
            log_fail = jnp.where(before, log_fail, 0.0)
        hi = log_fail.astype(BF16)
        lo = (log_fail - hi.astype(F32)).astype(BF16)
        later = (jnp.dot(hi, tri, preferred_element_type=F32)
                 + jnp.dot(lo, tri, preferred_element_type=F32))
        w = jnp.exp(log_beta + later + run)
        if diagonal:
            w = jnp.where(before, w, 0.0)
        acc = acc + jnp.dot(w.astype(BF16), v, preferred_element_type=F32)
        run = run + later[:, :1] + log_fail[:, :1]
        return run, acc

    carry = step(qi, (jnp.zeros((tq, 1), F32), jnp.zeros((tq, B_DIM), F32)), True)
    _, acc = lax.fori_loop(0, qi, lambda i, c: step(qi - 1 - i, c, False), carry)
    o_ref[...] = (acc * _silu(g_ref[...].astype(F32))).astype(o_ref.dtype)


def _attn_b(proj, batch, seq, tq):
    m = proj.shape[0]
    nq = seq // tq
    q0 = _Cols.QKVB // B_DIM
    k0 = q0 + B_HEADS
    v0 = k0 + B_HEADS
    g0 = _Cols.GB // B_DIM
    return pl.pallas_call(
        functools.partial(_attn_b_kernel, tq=tq, scale=1.0 / math.sqrt(B_DIM)),
        grid=(batch, B_HEADS, nq),
        in_specs=[pl.BlockSpec((tq, B_DIM), lambda b, h, i: (b * nq + i, q0 + h)),
                  pl.BlockSpec((seq, B_DIM), lambda b, h, i: (b, k0 + h)),
                  pl.BlockSpec((seq, B_DIM), lambda b, h, i: (b, v0 + h)),
                  pl.BlockSpec((tq, B_DIM), lambda b, h, i: (b * nq + i, g0 + h))],
        out_specs=pl.BlockSpec((tq, B_DIM), lambda b, h, i: (b * nq + i, h)),
        out_shape=jax.ShapeDtypeStruct((m, B_HEADS * B_DIM), BF16),
        compiler_params=_cparams(("parallel", "parallel", "arbitrary")),
        name="attn_b",
    )(proj, proj, proj, proj)


def _mem_prep_kernel(mem_ref, g_ref, w_ref, gk_ref, o_ref):
    j = pl.program_id(0)
    x = mem_ref[...]
    xn = (x * lax.rsqrt(jnp.mean(x * x, axis=-1, keepdims=True) + EPS) * g_ref[...]).astype(BF16)
    y = jnp.dot(xn, w_ref[...], preferred_element_type=F32)
    yk = y * lax.rsqrt(jnp.mean(y * y, axis=-1, keepdims=True) + EPS) * gk_ref[...]
    o_ref[...] = jnp.where(j < M_HEADS, yk, y).astype(o_ref.dtype)


def _mem_prep(mem2d, g_row, w_mkv, gk_row):
    mm, d = mem2d.shape
    n = 2 * M_HEADS * M_DIM
    return pl.pallas_call(
        _mem_prep_kernel,
        grid=(2 * M_HEADS,),
        in_specs=[pl.BlockSpec((mm, d), lambda j: (0, 0)),
                  pl.BlockSpec((1, d), lambda j: (0, 0)),
                  pl.BlockSpec((d, M_DIM), lambda j: (0, j)),
                  pl.BlockSpec((1, M_DIM), lambda j: (0, 0))],
        out_specs=pl.BlockSpec((mm, M_DIM), lambda j: (0, j)),
        out_shape=jax.ShapeDtypeStruct((mm, n), BF16),
        compiler_params=_cparams(("arbitrary",)),
        name="mem_prep",
    )(mem2d, g_row, w_mkv, gk_row)


def _attn_m_kernel(q_ref, k_ref, v_ref, g_ref, gq_ref, o_ref, *, scale):
    q = q_ref[...].astype(F32)
    qn = q * (lax.rsqrt(jnp.mean(q * q, axis=-1, keepdims=True) + EPS) * scale) * gq_ref[...]
    s = lax.dot_general(qn.astype(BF16), k_ref[...], _NT, preferred_element_type=F32)
    p = jnp.exp(s - jnp.max(s, axis=-1, keepdims=True))
    l = jnp.sum(p, axis=-1, keepdims=True)
    o = jnp.dot(p.astype(BF16), v_ref[...], preferred_element_type=F32) / l
    o_ref[...] = (o * _silu(g_ref[...].astype(F32))).astype(o_ref.dtype)


def _attn_m(proj, kv_m, gq_row, seq, mem_len, tm):
    m = proj.shape[0]
    nb = seq // tm
    q0 = _Cols.QM // M_DIM
    g0 = _Cols.GM // M_DIM
    assert mem_len == M_DIM, "memory block spec assumes MEM_LEN == M_DIM rows per batch"
    return pl.pallas_call(
        functools.partial(_attn_m_kernel, scale=1.0 / math.sqrt(M_DIM)),
        grid=(m // tm, M_HEADS),
        in_specs=[pl.BlockSpec((tm, M_DIM), lambda i, h: (i, q0 + h)),
                  pl.BlockSpec((mem_len, M_DIM), lambda i, h: (i // nb, h)),
                  pl.BlockSpec((mem_len, M_DIM), lambda i, h: (i // nb, M_HEADS + h)),
                  pl.BlockSpec((tm, M_DIM), lambda i, h: (i, g0 + h)),
                  pl.BlockSpec((1, M_DIM), lambda i, h: (0, 0))],
        out_specs=pl.BlockSpec((tm, M_DIM), lambda i, h: (i, h)),
        out_shape=jax.ShapeDtypeStruct((m, M_HEADS * M_DIM), BF16),
        compiler_params=_cparams(("parallel", "arbitrary")),
        name="attn_m",
    )(proj, kv_m, kv_m, proj, gq_row)


def _mix_kernel(ua_ref, ub_ref, um_ref, wa_ref, wb_ref, wm_ref, ra_ref, rb_ref, rm_ref, o_ref):
    ya = jnp.dot(ua_ref[...], wa_ref[...], preferred_element_type=F32)
    yb = jnp.dot(ub_ref[...], wb_ref[...], preferred_element_type=F32)
    ym = jnp.dot(um_ref[...], wm_ref[...], preferred_element_type=F32)
    mixed = (_sigmoid(ra_ref[...].astype(F32)) * ya + _sigmoid(rb_ref[...].astype(F32)) * yb
             + _sigmoid(rm_ref[...].astype(F32)) * ym)
    o_ref[...] = mixed.astype(o_ref.dtype)


def _mix(u_a, u_b, u_m, w_pa, w_pb, w_pm, proj, tm, tn):
    m = u_a.shape[0]
    r0 = _Cols.MERGE // tn
    rstep = D_MODEL // tn

    def rspec(branch):
        return pl.BlockSpec((tm, tn), lambda i, j: (i, r0 + branch * rstep + j))

    return pl.pallas_call(
        _mix_kernel,
        grid=(m // tm, D_MODEL // tn),
        in_specs=[pl.BlockSpec((tm, u_a.shape[1]), lambda i, j: (i, 0)),
                  pl.BlockSpec((tm, u_b.shape[1]), lambda i, j: (i, 0)),
                  pl.BlockSpec((tm, u_m.shape[1]), lambda i, j: (i, 0)),
                  pl.BlockSpec((w_pa.shape[0], tn), lambda i, j: (0, j)),
                  pl.BlockSpec((w_pb.shape[0], tn), lambda i, j: (0, j)),
                  pl.BlockSpec((w_pm.shape[0], tn), lambda i, j: (0, j)),
                  rspec(0), rspec(1), rspec(2)],
        out_specs=pl.BlockSpec((tm, tn), lambda i, j: (i, j)),
        out_shape=jax.ShapeDtypeStruct((m, D_MODEL), BF16),
        compiler_params=_cparams(("parallel", "arbitrary")),
        name="mix",
    )(u_a, u_b, u_m, w_pa, w_pb, w_pm, proj, proj, proj)


def _out_kernel(a_ref, w_ref, x_ref, o_ref):
    o_ref[...] = x_ref[...] + jnp.dot(a_ref[...], w_ref[...], preferred_element_type=F32)


def _out_proj(mixed, w_out, x, tm, tn):
    m, k = mixed.shape
    n = w_out.shape[1]
    return pl.pallas_call(
        _out_kernel,
        grid=(m // tm, n // tn),
        in_specs=[pl.BlockSpec((tm, k), lambda i, j: (i, 0)),
                  pl.BlockSpec((k, tn), lambda i, j: (0, j)),
                  pl.BlockSpec((tm, tn), lambda i, j: (i, j))],
        out_specs=pl.BlockSpec((tm, tn), lambda i, j: (i, j)),
        out_shape=jax.ShapeDtypeStruct((m, n), F32),
        compiler_params=_cparams(("parallel", "arbitrary")),
        name="out_proj",
    )(mixed, w_out, x)


def _rotate_half_cols(w):
    half = A_ROPE // 2
    return jnp.concatenate([-w[..., half:], w[..., :half]], axis=-1)


def _rope_gain(g_rope):
    half = A_ROPE // 2
    return jnp.concatenate([g_rope, g_rope[half:], g_rope[:half]])


def _prep_w_in(w_in):
    kr = w_in[:, 1536:1600]
    pad = jnp.zeros((w_in.shape[0], _Cols.TOTAL - _Cols.USED), w_in.dtype)
    return jnp.concatenate(
        [w_in[:, :1536], w_in[:, 1600:], kr, _rotate_half_cols(kr), pad], axis=1).astype(BF16)


def _prep_w_uq(w_uq):
    w = w_uq.reshape(Q_LORA, A_HEADS, A_QK)
    rope = w[..., A_NOPE:]
    w = jnp.concatenate([w[..., :A_NOPE], rope, _rotate_half_cols(rope)], axis=-1)
    return w.reshape(Q_LORA, A_HEADS * A_PAD).astype(BF16)


def _head_gain(g):
    return jnp.concatenate([g[:A_NOPE], _rope_gain(g[A_NOPE:])])[None, :]


def _tiles(m, seq):
    return dict(
        row=min(512, m),
        mm_m=min(1024, m),
        mm_n=1024,
        mix_n=512,
        attn=min(512, seq // 2),
    )


def kernel(x, mem, positions, g_pre, w_in, g_q_lat, w_uq, g_kv_lat, w_ukv, g_qn_a, g_kn_a,
           w_pa, w_pb, g_mem, w_mkv, g_qn_m, g_kn_m, w_pm, w_out):
    batch, seq, d = x.shape
    mem_len = mem.shape[1]
    depth = w_in.shape[0]
    m = batch * seq
    t = _tiles(m, seq)

    half = A_ROPE // 2
    freqs = ROPE_THETA ** (-jnp.arange(half, dtype=F32) / half)
    freq_row = jnp.tile(freqs, LANE // half)[None, :]
    pos_b = jnp.broadcast_to(positions.reshape(m, 1).astype(F32), (m, LANE))
    tbl = _rope_table(pos_b, freq_row, t["row"])

    xs = x.reshape(m, d)
    mem2d = mem.reshape(batch * mem_len, d)
    for l in range(depth):
        w_in_l = _prep_w_in(w_in[l])
        h = _norm(xs, g_pre[l][None, :], t["row"])
        proj = _in_proj(h, w_in_l, t["mm_m"], t["mm_n"])

        q_a = _q_prep(proj, g_q_lat[l][None, :], _prep_w_uq(w_uq[l]), _head_gain(g_qn_a[l]),
                      tbl, t["row"])
        k_a, v_a = _kv_prep(proj, g_kv_lat[l][None, :], w_ukv[l].astype(BF16),
                            _head_gain(g_kn_a[l]), tbl, t["row"])
        u_a = _attn_a(q_a, k_a, v_a, proj, batch, seq, t["attn"])
        u_b = _attn_b(proj, batch, seq, t["attn"])

        kv_m = _mem_prep(mem2d, g_mem[l][None, :], w_mkv[l].astype(BF16), g_kn_m[l][None, :])
        u_m = _attn_m(proj, kv_m, g_qn_m[l][None, :], seq, mem_len, t["row"])

        mixed = _mix(u_a, u_b, u_m, w_pa[l].astype(BF16), w_pb[l].astype(BF16),
                     w_pm[l].astype(BF16), proj, t["mm_m"], t["mix_n"])
        xs = _out_proj(mixed, w_out[l].astype(BF16), xs, t["mm_m"], t["mix_n"])
    return xs.reshape(batch, seq, d)
```

```python
import functools
import math

import jax
import jax.numpy as jnp
from jax import lax
from jax.experimental import pallas as pl
from jax.experimental.pallas import tpu as pltpu

F32 = jnp.float32
BF16 = jnp.bfloat16

EPS = 1e-6
LOG2E = 1.4426950408889634
NEG = -1e30
ROPE_THETA = 10000.0
CHUNK = 64

A_HEADS, A_NOPE, A_ROPE, A_V = 16, 128, 64, 128
A_QK = A_NOPE + A_ROPE
A_PAD = 256
Q_LORA, KV_LORA = 1024, 512
B_HEADS, B_DIM = 8, 128
M_HEADS, M_DIM = 4, 256
D_MODEL = 4096
N_BRANCH = 3

LANE = 128
VMEM_LIMIT = 56 * 1024 * 1024


class _Cols:
    CQ = 0
    CKV = CQ + Q_LORA
    GA = CKV + KV_LORA
    QKVB = GA + A_HEADS * A_V
    GB = QKVB + 3 * B_HEADS * B_DIM
    QM = GB + B_HEADS * B_DIM
    GM = QM + M_HEADS * M_DIM
    MERGE = GM + M_HEADS * M_DIM
    KR = MERGE + N_BRANCH * D_MODEL
    USED = KR + LANE
    TOTAL = 22528


def _cparams(sem):
    return pltpu.CompilerParams(dimension_semantics=sem, vmem_limit_bytes=VMEM_LIMIT)


def _sigmoid(v):
    return 1.0 / (1.0 + jnp.exp(-v))


def _silu(v):
    return v * _sigmoid(v)


_NT = (((1,), (1,)), ((), ()))


def _rope_table_kernel(pos_ref, freq_ref, o_ref):
    ang = pos_ref[...] * freq_ref[...]
    lane = lax.broadcasted_iota(jnp.int32, ang.shape, 1)
    o_ref[...] = jnp.where(lane < A_ROPE, jnp.cos(ang), jnp.sin(ang))


def _rope_table(pos_b, freq_row, tm):
    m = pos_b.shape[0]
    return pl.pallas_call(
        _rope_table_kernel,
        grid=(m // tm,),
        in_specs=[pl.BlockSpec((tm, LANE), lambda i: (i, 0)),
                  pl.BlockSpec((1, LANE), lambda i: (0, 0))],
        out_specs=pl.BlockSpec((tm, LANE), lambda i: (i, 0)),
        out_shape=jax.ShapeDtypeStruct((m, LANE), F32),
        compiler_params=_cparams(("parallel",)),
        name="rope_table",
    )(pos_b, freq_row)


def _norm_kernel(x_ref, g_ref, o_ref):
    x = x_ref[...]
    y = x * lax.rsqrt(jnp.mean(x * x, axis=-1, keepdims=True) + EPS)
    o_ref[...] = (y * g_ref[...]).astype(o_ref.dtype)


def _norm(x, g_row, tm):
    m, d = x.shape
    return pl.pallas_call(
        _norm_kernel,
        grid=(m // tm,),
        in_specs=[pl.BlockSpec((tm, d), lambda i: (i, 0)),
                  pl.BlockSpec((1, d), lambda i: (0, 0))],
        out_specs=pl.BlockSpec((tm, d), lambda i: (i, 0)),
        out_shape=jax.ShapeDtypeStruct((m, d), BF16),
        compiler_params=_cparams(("parallel",)),
        name="norm",
    )(x, g_row)


def _matmul_kernel(a_ref, w_ref, o_ref):
    o_ref[...] = jnp.dot(a_ref[...], w_ref[...], preferred_element_type=F32).astype(o_ref.dtype)


def _in_proj(h, w, tm, tn):
    m, k = h.shape
    n = w.shape[1]
    return pl.pallas_call(
        _matmul_kernel,
        grid=(m // tm, n // tn),
        in_specs=[pl.BlockSpec((tm, k), lambda i, j: (i, 0)),
                  pl.BlockSpec((k, tn), lambda i, j: (0, j))],
        out_specs=pl.BlockSpec((tm, tn), lambda i, j: (i, j)),
        out_shape=jax.ShapeDtypeStruct((m, n), BF16),
        compiler_params=_cparams(("parallel", "arbitrary")),
        name="in_proj",
    )(h, w)


def _q_prep_kernel(c_ref, gl_ref, w_ref, gh_ref, tbl_ref, o_ref, *, scale):
    c = c_ref[...].astype(F32)
    cn = c * lax.rsqrt(jnp.mean(c * c, axis=-1, keepdims=True) + EPS) * gl_ref[...]
    cn = cn.astype(BF16)
    tbl = tbl_ref[...]
    g1 = gh_ref[:, :A_NOPE]
    g2t = gh_ref[:, A_NOPE:] * tbl
    lane = lax.broadcasted_iota(jnp.int32, tbl.shape, 1)
    for h in range(A_HEADS):
        acc = jnp.dot(cn, w_ref[:, h * A_PAD:(h + 1) * A_PAD], preferred_element_type=F32)
        v1 = acc[:, :A_NOPE]
        v2 = acc[:, A_NOPE:]
        ss = (jnp.sum(v1 * v1, axis=-1, keepdims=True)
              + jnp.sum(jnp.where(lane < A_ROPE, v2 * v2, 0.0), axis=-1, keepdims=True))
        rs = lax.rsqrt(ss * (1.0 / A_QK) + EPS) * scale
        o_ref[:, h * A_PAD:h * A_PAD + A_NOPE] = (v1 * g1 * rs).astype(o_ref.dtype)
        o_ref[:, h * A_PAD + A_NOPE:(h + 1) * A_PAD] = (v2 * g2t * rs).astype(o_ref.dtype)


def _q_prep(proj, gl_row, w_uq, gh_row, tbl, tm):
    m = proj.shape[0]
    n = A_HEADS * A_PAD
    return pl.pallas_call(
        functools.partial(_q_prep_kernel, scale=LOG2E / math.sqrt(A_QK)),
        grid=(m // tm,),
        in_specs=[pl.BlockSpec((tm, Q_LORA), lambda i: (i, _Cols.CQ // Q_LORA)),
                  pl.BlockSpec((1, Q_LORA), lambda i: (0, 0)),
                  pl.BlockSpec((Q_LORA, n), lambda i: (0, 0)),
                  pl.BlockSpec((1, A_PAD), lambda i: (0, 0)),
                  pl.BlockSpec((tm, LANE), lambda i: (i, 0))],
        out_specs=pl.BlockSpec((tm, n), lambda i: (i, 0)),
        out_shape=jax.ShapeDtypeStruct((m, n), BF16),
        compiler_params=_cparams(("parallel",)),
        name="q_prep",
    )(proj, gl_row, w_uq, gh_row, tbl)


def _kv_prep_kernel(c_ref, kr_ref, gl_ref, w_ref, gh_ref, tbl_ref, kt_ref, v_ref):
    c = c_ref[...].astype(F32)
    cn = c * lax.rsqrt(jnp.mean(c * c, axis=-1, keepdims=True) + EPS) * gl_ref[...]
    cn = cn.astype(BF16)
    kr = kr_ref[...].astype(F32)
    lane = lax.broadcasted_iota(jnp.int32, kr.shape, 1)
    ss_r = jnp.sum(jnp.where(lane < A_ROPE, kr * kr, 0.0), axis=-1, keepdims=True)
    ab = kr * gh_ref[:, A_NOPE:] * tbl_ref[...]
    kk = ab + pltpu.roll(ab, A_ROPE, axis=1)
    g1 = gh_ref[:, :A_NOPE]
    ones = jnp.ones((kr.shape[0], A_PAD - A_V), v_ref.dtype)
    for h in range(A_HEADS):
        acc = jnp.dot(cn, w_ref[:, h * A_PAD:(h + 1) * A_PAD], preferred_element_type=F32)
        kn = acc[:, :A_NOPE]
        ss = jnp.sum(kn * kn, axis=-1, keepdims=True) + ss_r
        rs = lax.rsqrt(ss * (1.0 / A_QK) + EPS)
        kt_ref[h * A_PAD:h * A_PAD + A_NOPE, :] = (kn * g1 * rs).T.astype(kt_ref.dtype)
        kt_ref[h * A_PAD + A_NOPE:(h + 1) * A_PAD, :] = (kk * rs).T.astype(kt_ref.dtype)
        v_ref[:, h * A_PAD:h * A_PAD + A_V] = acc[:, A_NOPE:].astype(v_ref.dtype)
        v_ref[:, h * A_PAD + A_V:(h + 1) * A_PAD] = ones


def _kv_prep(proj, gl_row, w_ukv, gh_row, tbl, tm):
    m = proj.shape[0]
    nk = A_HEADS * A_PAD
    nv = A_HEADS * A_PAD
    return pl.pallas_call(
        _kv_prep_kernel,
        grid=(m // tm,),
        in_specs=[pl.BlockSpec((tm, KV_LORA), lambda i: (i, _Cols.CKV // KV_LORA)),
                  pl.BlockSpec((tm, LANE), lambda i: (i, _Cols.KR // LANE)),
                  pl.BlockSpec((1, KV_LORA), lambda i: (0, 0)),
                  pl.BlockSpec((KV_LORA, nk), lambda i: (0, 0)),
                  pl.BlockSpec((1, A_PAD), lambda i: (0, 0)),
                  pl.BlockSpec((tm, LANE), lambda i: (i, 0))],
        out_specs=[pl.BlockSpec((nk, tm), lambda i: (0, i)),
                   pl.BlockSpec((tm, nv), lambda i: (i, 0))],
        out_shape=[jax.ShapeDtypeStruct((nk, m), BF16),
                   jax.ShapeDtypeStruct((m, nv), BF16)],
        compiler_params=_cparams(("parallel",)),
        name="kv_prep",
    )(proj, proj, gl_row, w_ukv, gh_row, tbl)


def _attn_a_kernel(q_ref, kt_ref, v_ref, g_ref, o_ref, s_ref, p_ref, *, tq):
    qi = pl.program_id(2)
    q = q_ref[...]

    n_units = qi + 1

    def key_block(unit):
        unit = jnp.clip(unit, 0, qi)
        return jnp.where(unit == 0, qi, unit - 1)

    def scores(unit):
        start = pl.multiple_of(key_block(unit) * tq, tq)
        return jnp.dot(q, kt_ref[:, pl.ds(start, tq)], preferred_element_type=F32)

    def p_times_v(unit, slot):
        start = pl.multiple_of(key_block(unit) * tq, tq)
        return jnp.dot(p_ref[slot], v_ref[pl.ds(start, tq), :], preferred_element_type=F32)

    def softmax(slot, m, acc):
        s = s_ref[slot]
        m_new = jnp.maximum(m, jnp.max(s, axis=-1, keepdims=True))
        p_ref[slot] = jnp.exp2(s - m_new).astype(BF16)
        return m_new, jnp.exp2(m - m_new) * acc

    def step(i, slot, state):
        m, pend = state
        s_ref[slot] = jnp.where(i < n_units, scores(i), NEG)
        acc = pend + p_times_v(i - 2, slot)
        return softmax(1 - slot, m, acc)

    def two_steps(t, state):
        state = step(2 * t + 1, 1, state)
        return step(2 * t + 2, 0, state)

    qc = lax.broadcasted_iota(jnp.int32, (tq, tq), 0) // CHUNK
    kc = lax.broadcasted_iota(jnp.int32, (tq, tq), 1) // CHUNK
    s_ref[0] = jnp.where(kc <= qc, scores(0), -jnp.inf)
    p_ref[1] = jnp.zeros((tq, tq), BF16)
    state = (jnp.full((tq, 1), NEG, F32), jnp.zeros((tq, A_PAD), F32))
    _, acc = lax.fori_loop(0, (n_units + 2) // 2, two_steps, state)
    o = acc[:, :A_V] / acc[:, A_V:]
    o_ref[...] = (o * _silu(g_ref[...].astype(F32))).astype(o_ref.dtype)


def _attn_a(q_a, kt_a, v_a, proj, batch, seq, tq):
    m = q_a.shape[0]
    nq = seq // tq
    gate0 = _Cols.GA // A_V
    return pl.pallas_call(
        functools.partial(_attn_a_kernel, tq=tq),
        grid=(batch, A_HEADS, nq),
        in_specs=[pl.BlockSpec((tq, A_PAD), lambda b, h, i: (b * nq + i, h)),
                  pl.BlockSpec((A_PAD, seq), lambda b, h, i: (h, b)),
                  pl.BlockSpec((seq, A_PAD), lambda b, h, i: (b, h)),
                  pl.BlockSpec((tq, A_V), lambda b, h, i: (b * nq + i, gate0 + h))],
        out_specs=pl.BlockSpec((tq, A_V), lambda b, h, i: (b * nq + i, h)),
        out_shape=jax.ShapeDtypeStruct((m, A_HEADS * A_V), BF16),
        scratch_shapes=[pltpu.VMEM((2, tq, tq), F32), pltpu.VMEM((2, tq, tq), BF16)],
        compiler_params=_cparams(("parallel", "parallel", "arbitrary")),
        name="attn_a",
    )(q_a, kt_a, v_a, proj)


def _attn_b_kernel(q_ref, k_ref, v_ref, g_ref, o_ref,
                   z_ref, zc_ref, hl_ref, w_ref, *, tq, ks, scale2):
    qi = pl.program_id(2)
    q = q_ref[...]
    n_units = (tq // ks) * (qi + 1)
    r = lax.broadcasted_iota(jnp.int32, (2 * ks, ks), 0) % ks
    c = lax.broadcasted_iota(jnp.int32, (2 * ks, ks), 1)
    tri2 = jnp.where(r >= c, 1.0, 0.0).astype(BF16)

    def key_start(unit):
        unit = jnp.clip(unit, 0, n_units - 1)
        return pl.multiple_of((n_units - 1 - unit) * ks, ks)

    def stage_scores(unit, slot, own_block):
        start = key_start(unit)
        z = lax.dot_general(q, k_ref[pl.ds(start, ks), :], _NT, preferred_element_type=F32)
        if own_block:
            d = (lax.broadcasted_iota(jnp.int32, (tq, ks), 1)
                 - lax.broadcasted_iota(jnp.int32, (tq, ks), 0))
            keep = d < qi * tq - start
        else:
            keep = unit < n_units
        z_ref[slot] = jnp.where(keep, z * scale2, NEG)

    def stage_logs(slot):
        z = z_ref[slot]
        neg_abs = lax.bitcast_convert_type(
            lax.bitcast_convert_type(z, jnp.uint32) | jnp.uint32(0x80000000), F32)
        nlf = jnp.maximum(z, 0.0) + jnp.log(1.0 + jnp.exp2(neg_abs)) * LOG2E
        hi = nlf.astype(BF16)
        hl_ref[slot, :, :ks] = hi
        hl_ref[slot, :, ks:] = (nlf - hi.astype(F32)).astype(BF16)
        zc_ref[slot] = z

    def stage_weights(slot, mass):
        tail = jnp.dot(hl_ref[slot], tri2, preferred_element_type=F32)
        w_ref[slot] = jnp.exp2(zc_ref[slot] - tail - mass).astype(BF16)
        return mass + tail[:, :1]

    def stage_pv(unit, slot, acc):
        v = v_ref[pl.ds(key_start(unit), ks), :]
        return acc + jnp.dot(w_ref[slot], v, preferred_element_type=F32)

    def step(i, slot, state, own_block=False):
        mass, acc = state
        stage_scores(i, slot, own_block)
        acc = stage_pv(i - 3, 1 - slot, acc)
        mass = stage_weights(slot, mass)
        stage_logs(1 - slot)
        return mass, acc

    def two_steps(t, state):
        return step(2 * t + 1, 1, step(2 * t, 0, state))

    z_ref[1] = jnp.full((tq, ks), NEG, F32)
    zc_ref[0] = jnp.full((tq, ks), NEG, F32)
    hl_ref[0] = jnp.zeros((tq, 2 * ks), BF16)
    w_ref[1] = jnp.zeros((tq, ks), BF16)
    state = (jnp.zeros((tq, 1), F32), jnp.zeros((tq, B_DIM), F32))
    assert tq == 2 * ks, "units 0 and 1 are exactly the q block's own key block"
    state = step(1, 1, step(0, 0, state, own_block=True), own_block=True)
    _, acc = lax.fori_loop(1, (n_units + 4) // 2, two_steps, state)
    o_ref[...] = (acc * _silu(g_ref[...].astype(F32))).astype(o_ref.dtype)


def _attn_b(proj, batch, seq, tq, ks):
    m = proj.shape[0]
    nq = seq // tq
    q0 = _Cols.QKVB // B_DIM
    k0 = q0 + B_HEADS
    v0 = k0 + B_HEADS
    g0 = _Cols.GB // B_DIM
    return pl.pallas_call(
        functools.partial(_attn_b_kernel, tq=tq, ks=ks, scale2=LOG2E / math.sqrt(B_DIM)),
        grid=(batch, B_HEADS, nq),
        in_specs=[pl.BlockSpec((tq, B_DIM), lambda b, h, i: (b * nq + i, q0 + h)),
                  pl.BlockSpec((seq, B_DIM), lambda b, h, i: (b, k0 + h)),
                  pl.BlockSpec((seq, B_DIM), lambda b, h, i: (b, v0 + h)),
                  pl.BlockSpec((tq, B_DIM), lambda b, h, i: (b * nq + i, g0 + h))],
        out_specs=pl.BlockSpec((tq, B_DIM), lambda b, h, i: (b * nq + i, h)),
        out_shape=jax.ShapeDtypeStruct((m, B_HEADS * B_DIM), BF16),
        scratch_shapes=[pltpu.VMEM((2, tq, ks), F32), pltpu.VMEM((2, tq, ks), F32),
                        pltpu.VMEM((2, tq, 2 * ks), BF16), pltpu.VMEM((2, tq, ks), BF16)],
        compiler_params=_cparams(("parallel", "parallel", "arbitrary")),
        name="attn_b",
    )(proj, proj, proj, proj)


def _mem_prep_kernel(mem_ref, g_ref, w_ref, gk_ref, o_ref):
    j = pl.program_id(0)
    x = mem_ref[...]
    xn = (x * lax.rsqrt(jnp.mean(x * x, axis=-1, keepdims=True) + EPS) * g_ref[...]).astype(BF16)
    y = jnp.dot(xn, w_ref[...], preferred_element_type=F32)
    yk = y * lax.rsqrt(jnp.mean(y * y, axis=-1, keepdims=True) + EPS) * gk_ref[...]
    o_ref[...] = jnp.where(j < M_HEADS, yk, y).astype(o_ref.dtype)


def _mem_prep(mem2d, g_row, w_mkv, gk_row):
    mm, d = mem2d.shape
    n = 2 * M_HEADS * M_DIM
    return pl.pallas_call(
        _mem_prep_kernel,
        grid=(2 * M_HEADS,),
        in_specs=[pl.BlockSpec((mm, d), lambda j: (0, 0)),
                  pl.BlockSpec((1, d), lambda j: (0, 0)),
                  pl.BlockSpec((d, M_DIM), lambda j: (0, j)),
                  pl.BlockSpec((1, M_DIM), lambda j: (0, 0))],
        out_specs=pl.BlockSpec((mm, M_DIM), lambda j: (0, j)),
        out_shape=jax.ShapeDtypeStruct((mm, n), BF16),
        compiler_params=_cparams(("arbitrary",)),
        name="mem_prep",
    )(mem2d, g_row, w_mkv, gk_row)


def _attn_m_kernel(q_ref, k_ref, v_ref, g_ref, gq_ref, o_ref, *, scale):
    q = q_ref[...].astype(F32)
    qn = q * (lax.rsqrt(jnp.mean(q * q, axis=-1, keepdims=True) + EPS) * scale) * gq_ref[...]
    s = lax.dot_general(qn.astype(BF16), k_ref[...], _NT, preferred_element_type=F32)
    p = jnp.exp(s - jnp.max(s, axis=-1, keepdims=True))
    l = jnp.sum(p, axis=-1, keepdims=True)
    o = jnp.dot(p.astype(BF16), v_ref[...], preferred_element_type=F32) / l
    o_ref[...] = (o * _silu(g_ref[...].astype(F32))).astype(o_ref.dtype)


def _attn_m(proj, kv_m, gq_row, seq, mem_len, tm):
    m = proj.shape[0]
    nb = seq // tm
    q0 = _Cols.QM // M_DIM
    g0 = _Cols.GM // M_DIM
    assert mem_len == M_DIM, "memory block spec assumes MEM_LEN == M_DIM rows per batch"
    return pl.pallas_call(
        functools.partial(_attn_m_kernel, scale=1.0 / math.sqrt(M_DIM)),
        grid=(m // tm, M_HEADS),
        in_specs=[pl.BlockSpec((tm, M_DIM), lambda i, h: (i, q0 + h)),
                  pl.BlockSpec((mem_len, M_DIM), lambda i, h: (i // nb, h)),
                  pl.BlockSpec((mem_len, M_DIM), lambda i, h: (i // nb, M_HEADS + h)),
                  pl.BlockSpec((tm, M_DIM), lambda i, h: (i, g0 + h)),
                  pl.BlockSpec((1, M_DIM), lambda i, h: (0, 0))],
        out_specs=pl.BlockSpec((tm, M_DIM), lambda i, h: (i, h)),
        out_shape=jax.ShapeDtypeStruct((m, M_HEADS * M_DIM), BF16),
        compiler_params=_cparams(("parallel", "arbitrary")),
        name="attn_m",
    )(proj, kv_m, kv_m, proj, gq_row)


def _mix_kernel(ua_ref, ub_ref, um_ref, wa_ref, wb_ref, wm_ref, ra_ref, rb_ref, rm_ref, o_ref):
    ya = jnp.dot(ua_ref[...], wa_ref[...], preferred_element_type=F32)
    yb = jnp.dot(ub_ref[...], wb_ref[...], preferred_element_type=F32)
    ym = jnp.dot(um_ref[...], wm_ref[...], preferred_element_type=F32)
    mixed = (_sigmoid(ra_ref[...].astype(F32)) * ya + _sigmoid(rb_ref[...].astype(F32)) * yb
             + _sigmoid(rm_ref[...].astype(F32)) * ym)
    o_ref[...] = mixed.astype(o_ref.dtype)


def _mix(u_a, u_b, u_m, w_pa, w_pb, w_pm, proj, tm, tn):
    m = u_a.shape[0]
    r0 = _Cols.MERGE // tn
    rstep = D_MODEL // tn

    def rspec(branch):
        return pl.BlockSpec((tm, tn), lambda i, j: (i, r0 + branch * rstep + j))

    return pl.pallas_call(
        _mix_kernel,
        grid=(m // tm, D_MODEL // tn),
        in_specs=[pl.BlockSpec((tm, u_a.shape[1]), lambda i, j: (i, 0)),
                  pl.BlockSpec((tm, u_b.shape[1]), lambda i, j: (i, 0)),
                  pl.BlockSpec((tm, u_m.shape[1]), lambda i, j: (i, 0)),
                  pl.BlockSpec((w_pa.shape[0], tn), lambda i, j: (0, j)),
                  pl.BlockSpec((w_pb.shape[0], tn), lambda i, j: (0, j)),
                  pl.BlockSpec((w_pm.shape[0], tn), lambda i, j: (0, j)),
                  rspec(0), rspec(1), rspec(2)],
        out_specs=pl.BlockSpec((tm, tn), lambda i, j: (i, j)),
        out_shape=jax.ShapeDtypeStruct((m, D_MODEL), BF16),
        compiler_params=_cparams(("parallel", "arbitrary")),
        name="mix",
    )(u_a, u_b, u_m, w_pa, w_pb, w_pm, proj, proj, proj)


def _out_kernel(a_ref, w_ref, x_ref, o_ref):
    o_ref[...] = x_ref[...] + jnp.dot(a_ref[...], w_ref[...], preferred_element_type=F32)


def _out_proj(mixed, w_out, x, tm, tn):
    m, k = mixed.shape
    n = w_out.shape[1]
    return pl.pallas_call(
        _out_kernel,
        grid=(m // tm, n // tn),
        in_specs=[pl.BlockSpec((tm, k), lambda i, j: (i, 0)),
                  pl.BlockSpec((k, tn), lambda i, j: (0, j)),
                  pl.BlockSpec((tm, tn), lambda i, j: (i, j))],
        out_specs=pl.BlockSpec((tm, tn), lambda i, j: (i, j)),
        out_shape=jax.ShapeDtypeStruct((m, n), F32),
        compiler_params=_cparams(("parallel", "arbitrary")),
        name="out_proj",
    )(mixed, w_out, x)


def _rotate_half_cols(w):
    half = A_ROPE // 2
    return jnp.concatenate([-w[..., half:], w[..., :half]], axis=-1)


def _rope_gain(g_rope):
    half = A_ROPE // 2
    return jnp.concatenate([g_rope, g_rope[half:], g_rope[:half]])


def _prep_w_in(w_in):
    kr = w_in[:, 1536:1600]
    pad = jnp.zeros((w_in.shape[0], _Cols.TOTAL - _Cols.USED), w_in.dtype)
    return jnp.concatenate(
        [w_in[:, :1536], w_in[:, 1600:], kr, _rotate_half_cols(kr), pad], axis=1).astype(BF16)


def _prep_w_uq(w_uq):
    w = w_uq.reshape(Q_LORA, A_HEADS, A_QK)
    rope = w[..., A_NOPE:]
    w = jnp.concatenate([w[..., :A_NOPE], rope, _rotate_half_cols(rope)], axis=-1)
    return w.reshape(Q_LORA, A_HEADS * A_PAD).astype(BF16)


def _head_gain(g):
    return jnp.concatenate([g[:A_NOPE], _rope_gain(g[A_NOPE:])])[None, :]


def _tiles(m, seq):
    return dict(
        row=min(512, m),
        mm_m=min(1024, m),
        mm_n=1024,
        mix_n=512,
        attn=min(512, seq // 2),
        b_keys=min(256, seq // 4),
    )


def kernel(x, mem, positions, g_pre, w_in, g_q_lat, w_uq, g_kv_lat, w_ukv, g_qn_a, g_kn_a,
           w_pa, w_pb, g_mem, w_mkv, g_qn_m, g_kn_m, w_pm, w_out):
    batch, seq, d = x.shape
    mem_len = mem.shape[1]
    depth = w_in.shape[0]
    m = batch * seq
    t = _tiles(m, seq)

    half = A_ROPE // 2
    freqs = ROPE_THETA ** (-jnp.arange(half, dtype=F32) / half)
    freq_row = jnp.tile(freqs, LANE // half)[None, :]
    pos_b = jnp.broadcast_to(positions.reshape(m, 1).astype(F32), (m, LANE))
    tbl = _rope_table(pos_b, freq_row, t["row"])

    xs = x.reshape(m, d)
    mem2d = mem.reshape(batch * mem_len, d)
    for l in range(depth):
        w_in_l = _prep_w_in(w_in[l])
        h = _norm(xs, g_pre[l][None, :], t["row"])
        proj = _in_proj(h, w_in_l, t["mm_m"], t["mm_n"])

        q_a = _q_prep(proj, g_q_lat[l][None, :], _prep_w_uq(w_uq[l]), _head_gain(g_qn_a[l]),
                      tbl, t["row"])
        k_a, v_a = _kv_prep(proj, g_kv_lat[l][None, :], w_ukv[l].astype(BF16),
                            _head_gain(g_kn_a[l]), tbl, t["row"])
        u_a = _attn_a(q_a, k_a, v_a, proj, batch, seq, t["attn"])
        u_b = _attn_b(proj, batch, seq, t["attn"], t["b_keys"])

        kv_m = _mem_prep(mem2d, g_mem[l][None, :], w_mkv[l].astype(BF16), g_kn_m[l][None, :])
        u_m = _attn_m(proj, kv_m, g_qn_m[l][None, :], seq, mem_len, t["row"])

        mixed = _mix(u_a, u_b, u_m, w_pa[l].astype(BF16), w_pb[l].astype(BF16),
                     w_pm[l].astype(BF16), proj, t["mm_m"], t["mix_n"])
        xs = _out_proj(mixed, w_out[l].astype(BF16), xs, t["mm_m"], t["mix_n"])
    return xs.reshape(batch, seq, d)
```

```python
import functools
import math

import jax
import jax.numpy as jnp
from jax import lax
from jax.experimental import pallas as pl
from jax.experimental.pallas import tpu as pltpu

F32 = jnp.float32
BF16 = jnp.bfloat16

EPS = 1e-6
LOG2E = 1.4426950408889634
NEG = -1e30
ROPE_THETA = 10000.0
CHUNK = 64

A_HEADS, A_NOPE, A_ROPE, A_V = 16, 128, 64, 128
A_QK = A_NOPE + A_ROPE
A_PAD = 256
Q_LORA, KV_LORA = 1024, 512
B_HEADS, B_DIM = 8, 128
M_HEADS, M_DIM = 4, 256
D_MODEL = 4096
N_BRANCH = 3

LANE = 128
VMEM_LIMIT = 56 * 1024 * 1024


class _Cols:
    CQ = 0
    CKV = CQ + Q_LORA
    GA = CKV + KV_LORA
    QKVB = GA + A_HEADS * A_V
    GB = QKVB + 3 * B_HEADS * B_DIM
    QM = GB + B_HEADS * B_DIM
    GM = QM + M_HEADS * M_DIM
    MERGE = GM + M_HEADS * M_DIM
    KR = MERGE + N_BRANCH * D_MODEL
    USED = KR + LANE
    TOTAL = 22528


def _cparams(sem):
    return pltpu.CompilerParams(dimension_semantics=sem, vmem_limit_bytes=VMEM_LIMIT)


def _sigmoid(v):
    return 1.0 / (1.0 + jnp.exp(-v))


def _silu(v):
    return v * _sigmoid(v)


_NT = (((1,), (1,)), ((), ()))


def _rope_table_kernel(pos_ref, freq_ref, o_ref):
    ang = pos_ref[...] * freq_ref[...]
    lane = lax.broadcasted_iota(jnp.int32, ang.shape, 1)
    o_ref[...] = jnp.where(lane < A_ROPE, jnp.cos(ang), jnp.sin(ang))


def _rope_table(pos_b, freq_row, tm):
    m = pos_b.shape[0]
    return pl.pallas_call(
        _rope_table_kernel,
        grid=(m // tm,),
        in_specs=[pl.BlockSpec((tm, LANE), lambda i: (i, 0)),
                  pl.BlockSpec((1, LANE), lambda i: (0, 0))],
        out_specs=pl.BlockSpec((tm, LANE), lambda i: (i, 0)),
        out_shape=jax.ShapeDtypeStruct((m, LANE), F32),
        compiler_params=_cparams(("parallel",)),
        name="rope_table",
    )(pos_b, freq_row)


def _norm_kernel(x_ref, g_ref, o_ref):
    x = x_ref[...]
    y = x * lax.rsqrt(jnp.mean(x * x, axis=-1, keepdims=True) + EPS)
    o_ref[...] = (y * g_ref[...]).astype(o_ref.dtype)


def _norm(x, g_row, tm):
    m, d = x.shape
    return pl.pallas_call(
        _norm_kernel,
        grid=(m // tm,),
        in_specs=[pl.BlockSpec((tm, d), lambda i: (i, 0)),
                  pl.BlockSpec((1, d), lambda i: (0, 0))],
        out_specs=pl.BlockSpec((tm, d), lambda i: (i, 0)),
        out_shape=jax.ShapeDtypeStruct((m, d), BF16),
        compiler_params=_cparams(("parallel",)),
        name="norm",
    )(x, g_row)


def _matmul_nt_kernel(a_ref, wt_ref, o_ref):
    o_ref[...] = lax.dot_general(a_ref[...], wt_ref[...], _NT,
                                 preferred_element_type=F32).astype(o_ref.dtype)


def _in_proj(h, wt, tm, tn):
    m, k = h.shape
    n = wt.shape[0]
    return pl.pallas_call(
        _matmul_nt_kernel,
        grid=(m // tm, n // tn),
        in_specs=[pl.BlockSpec((tm, k), lambda i, j: (i, 0)),
                  pl.BlockSpec((tn, k), lambda i, j: (j, 0))],
        out_specs=pl.BlockSpec((tm, tn), lambda i, j: (i, j)),
        out_shape=jax.ShapeDtypeStruct((m, n), BF16),
        compiler_params=_cparams(("parallel", "arbitrary")),
        name="in_proj",
    )(h, wt)


def _q_prep_kernel(c_ref, gl_ref, w_ref, gh_ref, tbl_ref, o_ref, *, scale):
    c = c_ref[...].astype(F32)
    cn = c * lax.rsqrt(jnp.mean(c * c, axis=-1, keepdims=True) + EPS) * gl_ref[...]
    cn = cn.astype(BF16)
    tbl = tbl_ref[...]
    g1 = gh_ref[:, :A_NOPE]
    g2t = gh_ref[:, A_NOPE:] * tbl
    lane = lax.broadcasted_iota(jnp.int32, tbl.shape, 1)
    for h in range(A_HEADS):
        acc = jnp.dot(cn, w_ref[:, h * A_PAD:(h + 1) * A_PAD], preferred_element_type=F32)
        v1 = acc[:, :A_NOPE]
        v2 = acc[:, A_NOPE:]
        ss = (jnp.sum(v1 * v1, axis=-1, keepdims=True)
              + jnp.sum(jnp.where(lane < A_ROPE, v2 * v2, 0.0), axis=-1, keepdims=True))
        rs = lax.rsqrt(ss * (1.0 / A_QK) + EPS) * scale
        o_ref[:, h * A_PAD:h * A_PAD + A_NOPE] = (v1 * g1 * rs).astype(o_ref.dtype)
        o_ref[:, h * A_PAD + A_NOPE:(h + 1) * A_PAD] = (v2 * g2t * rs).astype(o_ref.dtype)


def _q_prep(proj, gl_row, w_uq, gh_row, tbl, tm):
    m = proj.shape[0]
    n = A_HEADS * A_PAD
    return pl.pallas_call(
        functools.partial(_q_prep_kernel, scale=LOG2E / math.sqrt(A_QK)),
        grid=(m // tm,),
        in_specs=[pl.BlockSpec((tm, Q_LORA), lambda i: (i, _Cols.CQ // Q_LORA)),
                  pl.BlockSpec((1, Q_LORA), lambda i: (0, 0)),
                  pl.BlockSpec((Q_LORA, n), lambda i: (0, 0)),
                  pl.BlockSpec((1, A_PAD), lambda i: (0, 0)),
                  pl.BlockSpec((tm, LANE), lambda i: (i, 0))],
        out_specs=pl.BlockSpec((tm, n), lambda i: (i, 0)),
        out_shape=jax.ShapeDtypeStruct((m, n), BF16),
        compiler_params=_cparams(("parallel",)),
        name="q_prep",
    )(proj, gl_row, w_uq, gh_row, tbl)


def _kv_prep_kernel(c_ref, kr_ref, gl_ref, w_ref, gh_ref, tbl_ref, kt_ref, v_ref):
    c = c_ref[...].astype(F32)
    cn = c * lax.rsqrt(jnp.mean(c * c, axis=-1, keepdims=True) + EPS) * gl_ref[...]
    cn = cn.astype(BF16)
    kr = kr_ref[...].astype(F32)
    lane = lax.broadcasted_iota(jnp.int32, kr.shape, 1)
    ss_r = jnp.sum(jnp.where(lane < A_ROPE, kr * kr, 0.0), axis=-1, keepdims=True)
    ab = kr * gh_ref[:, A_NOPE:] * tbl_ref[...]
    kk = ab + pltpu.roll(ab, A_ROPE, axis=1)
    g1 = gh_ref[:, :A_NOPE]
    ones = jnp.ones((kr.shape[0], A_PAD - A_V), v_ref.dtype)
    for h in range(A_HEADS):
        acc = jnp.dot(cn, w_ref[:, h * A_PAD:(h + 1) * A_PAD], preferred_element_type=F32)
        kn = acc[:, :A_NOPE]
        ss = jnp.sum(kn * kn, axis=-1, keepdims=True) + ss_r
        rs = lax.rsqrt(ss * (1.0 / A_QK) + EPS)
        kt_ref[h * A_PAD:h * A_PAD + A_NOPE, :] = (kn * g1 * rs).T.astype(kt_ref.dtype)
        kt_ref[h * A_PAD + A_NOPE:(h + 1) * A_PAD, :] = (kk * rs).T.astype(kt_ref.dtype)
        v_ref[:, h * A_PAD:h * A_PAD + A_V] = acc[:, A_NOPE:].astype(v_ref.dtype)
        v_ref[:, h * A_PAD + A_V:(h + 1) * A_PAD] = ones


def _kv_prep(proj, gl_row, w_ukv, gh_row, tbl, tm):
    m = proj.shape[0]
    nk = A_HEADS * A_PAD
    nv = A_HEADS * A_PAD
    return pl.pallas_call(
        _kv_prep_kernel,
        grid=(m // tm,),
        in_specs=[pl.BlockSpec((tm, KV_LORA), lambda i: (i, _Cols.CKV // KV_LORA)),
                  pl.BlockSpec((tm, LANE), lambda i: (i, _Cols.KR // LANE)),
                  pl.BlockSpec((1, KV_LORA), lambda i: (0, 0)),
                  pl.BlockSpec((KV_LORA, nk), lambda i: (0, 0)),
                  pl.BlockSpec((1, A_PAD), lambda i: (0, 0)),
                  pl.BlockSpec((tm, LANE), lambda i: (i, 0))],
        out_specs=[pl.BlockSpec((nk, tm), lambda i: (0, i)),
                   pl.BlockSpec((tm, nv), lambda i: (i, 0))],
        out_shape=[jax.ShapeDtypeStruct((nk, m), BF16),
                   jax.ShapeDtypeStruct((m, nv), BF16)],
        compiler_params=_cparams(("parallel",)),
        name="kv_prep",
    )(proj, proj, gl_row, w_ukv, gh_row, tbl)


def _attn_a_kernel(q_ref, kt_ref, v_ref, g_ref, o_ref, s_ref, p_ref, *, tq):
    qi = pl.program_id(2)
    q = q_ref[...]

    n_units = qi + 1

    def key_block(unit):
        unit = jnp.clip(unit, 0, qi)
        return jnp.where(unit == 0, qi, unit - 1)

    def scores(unit):
        start = pl.multiple_of(key_block(unit) * tq, tq)
        return jnp.dot(q, kt_ref[:, pl.ds(start, tq)], preferred_element_type=F32)

    def p_times_v(unit, slot):
        start = pl.multiple_of(key_block(unit) * tq, tq)
        return jnp.dot(p_ref[slot], v_ref[pl.ds(start, tq), :], preferred_element_type=F32)

    def softmax(slot, m, acc):
        s = s_ref[slot]
        m_new = jnp.maximum(m, jnp.max(s, axis=-1, keepdims=True))
        p_ref[slot] = jnp.exp2(s - m_new).astype(BF16)
        return m_new, jnp.exp2(m - m_new) * acc

    def step(i, slot, state):
        m, pend = state
        s_ref[slot] = jnp.where(i < n_units, scores(i), NEG)
        acc = pend + p_times_v(i - 2, slot)
        return softmax(1 - slot, m, acc)

    def two_steps(t, state):
        state = step(2 * t + 1, 1, state)
        return step(2 * t + 2, 0, state)

    qc = lax.broadcasted_iota(jnp.int32, (tq, tq), 0) // CHUNK
    kc = lax.broadcasted_iota(jnp.int32, (tq, tq), 1) // CHUNK
    s_ref[0] = jnp.where(kc <= qc, scores(0), -jnp.inf)
    p_ref[1] = jnp.zeros((tq, tq), BF16)
    state = (jnp.full((tq, 1), NEG, F32), jnp.zeros((tq, A_PAD), F32))
    _, acc = lax.fori_loop(0, (n_units + 2) // 2, two_steps, state)
    o = acc[:, :A_V] / acc[:, A_V:]
    o_ref[...] = (o * _silu(g_ref[...].astype(F32))).astype(o_ref.dtype)


def _attn_a(q_a, kt_a, v_a, proj, batch, seq, tq):
    m = q_a.shape[0]
    nq = seq // tq
    gate0 = _Cols.GA // A_V
    return pl.pallas_call(
        functools.partial(_attn_a_kernel, tq=tq),
        grid=(batch, A_HEADS, nq),
        in_specs=[pl.BlockSpec((tq, A_PAD), lambda b, h, i: (b * nq + i, h)),
                  pl.BlockSpec((A_PAD, seq), lambda b, h, i: (h, b)),
                  pl.BlockSpec((seq, A_PAD), lambda b, h, i: (b, h)),
                  pl.BlockSpec((tq, A_V), lambda b, h, i: (b * nq + i, gate0 + h))],
        out_specs=pl.BlockSpec((tq, A_V), lambda b, h, i: (b * nq + i, h)),
        out_shape=jax.ShapeDtypeStruct((m, A_HEADS * A_V), BF16),
        scratch_shapes=[pltpu.VMEM((2, tq, tq), F32), pltpu.VMEM((2, tq, tq), BF16)],
        compiler_params=_cparams(("parallel", "parallel", "arbitrary")),
        name="attn_a",
    )(q_a, kt_a, v_a, proj)


def _attn_b_kernel(q_ref, k_ref, v_ref, g_ref, o_ref,
                   z_ref, zc_ref, hl_ref, w_ref, *, tq, ks, scale2):
    qi = pl.program_id(2)
    q = q_ref[...]
    n_units = (tq // ks) * (qi + 1)
    r = lax.broadcasted_iota(jnp.int32, (ks, ks), 0)
    c = lax.broadcasted_iota(jnp.int32, (ks, ks), 1)
    tri = jnp.where(r >= c, 1.0, 0.0).astype(BF16)

    def key_start(unit):
        unit = jnp.clip(unit, 0, n_units - 1)
        return pl.multiple_of((n_units - 1 - unit) * ks, ks)

    def stage_scores(unit, slot, own_block):
        start = key_start(unit)
        z = lax.dot_general(q, k_ref[pl.ds(start, ks), :], _NT, preferred_element_type=F32)
        if own_block:
            d = (lax.broadcasted_iota(jnp.int32, (tq, ks), 1)
                 - lax.broadcasted_iota(jnp.int32, (tq, ks), 0))
            keep = d < qi * tq - start
        else:
            keep = unit < n_units
        z_ref[slot] = jnp.where(keep, z * scale2, NEG)

    def stage_logs(slot):
        z = z_ref[slot]
        neg_abs = lax.bitcast_convert_type(
            lax.bitcast_convert_type(z, jnp.uint32) | jnp.uint32(0x80000000), F32)
        nlf = jnp.maximum(z, 0.0) + jnp.log(1.0 + jnp.exp2(neg_abs)) * LOG2E
        hl_ref[slot] = nlf.astype(BF16)
        zc_ref[slot] = z

    def stage_weights(slot, mass):
        tail = jnp.dot(hl_ref[slot], tri, preferred_element_type=F32)
        w_ref[slot] = jnp.exp2(zc_ref[slot] - tail - mass).astype(BF16)
        return mass + tail[:, :1]

    def stage_pv(unit, slot, acc):
        v = v_ref[pl.ds(key_start(unit), ks), :]
        return acc + jnp.dot(w_ref[slot], v, preferred_element_type=F32)

    def step(i, slot, state, own_block=False):
        mass, acc = state
        stage_scores(i, slot, own_block)
        acc = stage_pv(i - 3, 1 - slot, acc)
        mass = stage_weights(slot, mass)
        stage_logs(1 - slot)
        return mass, acc

    def two_steps(t, state):
        return step(2 * t + 1, 1, step(2 * t, 0, state))

    z_ref[1] = jnp.full((tq, ks), NEG, F32)
    zc_ref[0] = jnp.full((tq, ks), NEG, F32)
    hl_ref[0] = jnp.zeros((tq, ks), BF16)
    w_ref[1] = jnp.zeros((tq, ks), BF16)
    state = (jnp.zeros((tq, 1), F32), jnp.zeros((tq, B_DIM), F32))
    assert tq == 2 * ks, "units 0 and 1 are exactly the q block's own key block"
    state = step(1, 1, step(0, 0, state, own_block=True), own_block=True)
    _, acc = lax.fori_loop(1, (n_units + 4) // 2, two_steps, state)
    o_ref[...] = (acc * _silu(g_ref[...].astype(F32))).astype(o_ref.dtype)


def _attn_b(proj, batch, seq, tq, ks):
    m = proj.shape[0]
    nq = seq // tq
    q0 = _Cols.QKVB // B_DIM
    k0 = q0 + B_HEADS
    v0 = k0 + B_HEADS
    g0 = _Cols.GB // B_DIM
    return pl.pallas_call(
        functools.partial(_attn_b_kernel, tq=tq, ks=ks, scale2=LOG2E / math.sqrt(B_DIM)),
        grid=(batch, B_HEADS, nq),
        in_specs=[pl.BlockSpec((tq, B_DIM), lambda b, h, i: (b * nq + i, q0 + h)),
                  pl.BlockSpec((seq, B_DIM), lambda b, h, i: (b, k0 + h)),
                  pl.BlockSpec((seq, B_DIM), lambda b, h, i: (b, v0 + h)),
                  pl.BlockSpec((tq, B_DIM), lambda b, h, i: (b * nq + i, g0 + h))],
        out_specs=pl.BlockSpec((tq, B_DIM), lambda b, h, i: (b * nq + i, h)),
        out_shape=jax.ShapeDtypeStruct((m, B_HEADS * B_DIM), BF16),
        scratch_shapes=[pltpu.VMEM((2, tq, ks), F32), pltpu.VMEM((2, tq, ks), F32),
                        pltpu.VMEM((2, tq, ks), BF16), pltpu.VMEM((2, tq, ks), BF16)],
        compiler_params=_cparams(("parallel", "parallel", "arbitrary")),
        name="attn_b",
    )(proj, proj, proj, proj)


def _mem_prep_kernel(mem_ref, g_ref, w_ref, gk_ref, o_ref):
    j = pl.program_id(0)
    x = mem_ref[...]
    xn = (x * lax.rsqrt(jnp.mean(x * x, axis=-1, keepdims=True) + EPS) * g_ref[...]).astype(BF16)
    y = jnp.dot(xn, w_ref[...], preferred_element_type=F32)
    yk = y * lax.rsqrt(jnp.mean(y * y, axis=-1, keepdims=True) + EPS) * gk_ref[...]
    o_ref[...] = jnp.where(j < M_HEADS, yk, y).astype(o_ref.dtype)


def _mem_prep(mem2d, g_row, w_mkv, gk_row):
    mm, d = mem2d.shape
    n = 2 * M_HEADS * M_DIM
    return pl.pallas_call(
        _mem_prep_kernel,
        grid=(2 * M_HEADS,),
        in_specs=[pl.BlockSpec((mm, d), lambda j: (0, 0)),
                  pl.BlockSpec((1, d), lambda j: (0, 0)),
                  pl.BlockSpec((d, M_DIM), lambda j: (0, j)),
                  pl.BlockSpec((1, M_DIM), lambda j: (0, 0))],
        out_specs=pl.BlockSpec((mm, M_DIM), lambda j: (0, j)),
        out_shape=jax.ShapeDtypeStruct((mm, n), BF16),
        compiler_params=_cparams(("arbitrary",)),
        name="mem_prep",
    )(mem2d, g_row, w_mkv, gk_row)


def _attn_m_kernel(q_ref, k_ref, v_ref, g_ref, gq_ref, o_ref, *, scale):
    q = q_ref[...].astype(F32)
    qn = q * (lax.rsqrt(jnp.mean(q * q, axis=-1, keepdims=True) + EPS) * scale) * gq_ref[...]
    s = lax.dot_general(qn.astype(BF16), k_ref[...], _NT, preferred_element_type=F32)
    p = jnp.exp(s - jnp.max(s, axis=-1, keepdims=True))
    l = jnp.sum(p, axis=-1, keepdims=True)
    o = jnp.dot(p.astype(BF16), v_ref[...], preferred_element_type=F32) / l
    o_ref[...] = (o * _silu(g_ref[...].astype(F32))).astype(o_ref.dtype)


def _attn_m(proj, kv_m, gq_row, seq, mem_len, tm):
    m = proj.shape[0]
    nb = seq // tm
    q0 = _Cols.QM // M_DIM
    g0 = _Cols.GM // M_DIM
    assert mem_len == M_DIM, "memory block spec assumes MEM_LEN == M_DIM rows per batch"
    return pl.pallas_call(
        functools.partial(_attn_m_kernel, scale=1.0 / math.sqrt(M_DIM)),
        grid=(m // tm, M_HEADS),
        in_specs=[pl.BlockSpec((tm, M_DIM), lambda i, h: (i, q0 + h)),
                  pl.BlockSpec((mem_len, M_DIM), lambda i, h: (i // nb, h)),
                  pl.BlockSpec((mem_len, M_DIM), lambda i, h: (i // nb, M_HEADS + h)),
                  pl.BlockSpec((tm, M_DIM), lambda i, h: (i, g0 + h)),
                  pl.BlockSpec((1, M_DIM), lambda i, h: (0, 0))],
        out_specs=pl.BlockSpec((tm, M_DIM), lambda i, h: (i, h)),
        out_shape=jax.ShapeDtypeStruct((m, M_HEADS * M_DIM), BF16),
        compiler_params=_cparams(("parallel", "arbitrary")),
        name="attn_m",
    )(proj, kv_m, kv_m, proj, gq_row)


def _mix_kernel(ua_ref, ub_ref, um_ref, wa_ref, wb_ref, wm_ref, ra_ref, rb_ref, rm_ref, o_ref):
    ya = jnp.dot(ua_ref[...], wa_ref[...], preferred_element_type=F32)
    yb = jnp.dot(ub_ref[...], wb_ref[...], preferred_element_type=F32)
    ym = jnp.dot(um_ref[...], wm_ref[...], preferred_element_type=F32)
    mixed = (_sigmoid(ra_ref[...].astype(F32)) * ya + _sigmoid(rb_ref[...].astype(F32)) * yb
             + _sigmoid(rm_ref[...].astype(F32)) * ym)
    o_ref[...] = mixed.astype(o_ref.dtype)


def _mix(u_a, u_b, u_m, w_pa, w_pb, w_pm, proj, tm, tn):
    m = u_a.shape[0]
    r0 = _Cols.MERGE // tn
    rstep = D_MODEL // tn

    def rspec(branch):
        return pl.BlockSpec((tm, tn), lambda i, j: (i, r0 + branch * rstep + j))

    return pl.pallas_call(
        _mix_kernel,
        grid=(m // tm, D_MODEL // tn),
        in_specs=[pl.BlockSpec((tm, u_a.shape[1]), lambda i, j: (i, 0)),
                  pl.BlockSpec((tm, u_b.shape[1]), lambda i, j: (i, 0)),
                  pl.BlockSpec((tm, u_m.shape[1]), lambda i, j: (i, 0)),
                  pl.BlockSpec((w_pa.shape[0], tn), lambda i, j: (0, j)),
                  pl.BlockSpec((w_pb.shape[0], tn), lambda i, j: (0, j)),
                  pl.BlockSpec((w_pm.shape[0], tn), lambda i, j: (0, j)),
                  rspec(0), rspec(1), rspec(2)],
        out_specs=pl.BlockSpec((tm, tn), lambda i, j: (i, j)),
        out_shape=jax.ShapeDtypeStruct((m, D_MODEL), BF16),
        compiler_params=_cparams(("parallel", "arbitrary")),
        name="mix",
    )(u_a, u_b, u_m, w_pa, w_pb, w_pm, proj, proj, proj)


def _out_kernel(a_ref, w_ref, x_ref, o_ref):
    o_ref[...] = x_ref[...] + jnp.dot(a_ref[...], w_ref[...], preferred_element_type=F32)


def _out_proj(mixed, w_out, x, tm, tn):
    m, k = mixed.shape
    n = w_out.shape[1]
    return pl.pallas_call(
        _out_kernel,
        grid=(m // tm, n // tn),
        in_specs=[pl.BlockSpec((tm, k), lambda i, j: (i, 0)),
                  pl.BlockSpec((k, tn), lambda i, j: (0, j)),
                  pl.BlockSpec((tm, tn), lambda i, j: (i, j))],
        out_specs=pl.BlockSpec((tm, tn), lambda i, j: (i, j)),
        out_shape=jax.ShapeDtypeStruct((m, n), F32),
        compiler_params=_cparams(("parallel", "arbitrary")),
        name="out_proj",
    )(mixed, w_out, x)


def _rotate_half_cols(w):
    half = A_ROPE // 2
    return jnp.concatenate([-w[..., half:], w[..., :half]], axis=-1)


def _rope_gain(g_rope):
    half = A_ROPE // 2
    return jnp.concatenate([g_rope, g_rope[half:], g_rope[:half]])


def _prep_w_in(w_in):
    wt = w_in.T
    kr = wt[1536:1600]
    rot = _rotate_half_cols(kr.T).T
    pad = jnp.zeros((_Cols.TOTAL - _Cols.USED, wt.shape[1]), wt.dtype)
    return jnp.concatenate([wt[:1536], wt[1600:], kr, rot, pad], axis=0).astype(BF16)


def _prep_w_uq(w_uq):
    w = w_uq.reshape(Q_LORA, A_HEADS, A_QK)
    rope = w[..., A_NOPE:]
    w = jnp.concatenate([w[..., :A_NOPE], rope, _rotate_half_cols(rope)], axis=-1)
    return w.reshape(Q_LORA, A_HEADS * A_PAD).astype(BF16)


def _head_gain(g):
    return jnp.concatenate([g[:A_NOPE], _rope_gain(g[A_NOPE:])])[None, :]


def _tiles(m, seq):
    return dict(
        row=min(512, m),
        mm_m=min(1024, m),
        mm_n=1024,
        mix_n=512,
        attn=min(512, seq // 2),
        b_keys=min(256, seq // 4),
    )


def kernel(x, mem, positions, g_pre, w_in, g_q_lat, w_uq, g_kv_lat, w_ukv, g_qn_a, g_kn_a,
           w_pa, w_pb, g_mem, w_mkv, g_qn_m, g_kn_m, w_pm, w_out):
    batch, seq, d = x.shape
    mem_len = mem.shape[1]
    depth = w_in.shape[0]
    m = batch * seq
    t = _tiles(m, seq)

    half = A_ROPE // 2
    freqs = ROPE_THETA ** (-jnp.arange(half, dtype=F32) / half)
    freq_row = jnp.tile(freqs, LANE // half)[None, :]
    pos_b = jnp.broadcast_to(positions.reshape(m, 1).astype(F32), (m, LANE))
    tbl = _rope_table(pos_b, freq_row, t["row"])

    xs = x.reshape(m, d)
    mem2d = mem.reshape(batch * mem_len, d)
    for l in range(depth):
        w_in_l = _prep_w_in(w_in[l])
        h = _norm(xs, g_pre[l][None, :], t["row"])
        proj = _in_proj(h, w_in_l, t["mm_m"], t["mm_n"])

        q_a = _q_prep(proj, g_q_lat[l][None, :], _prep_w_uq(w_uq[l]), _head_gain(g_qn_a[l]),
                      tbl, t["row"])
        k_a, v_a = _kv_prep(proj, g_kv_lat[l][None, :], w_ukv[l].astype(BF16),
                            _head_gain(g_kn_a[l]), tbl, t["row"])
        u_a = _attn_a(q_a, k_a, v_a, proj, batch, seq, t["attn"])
        u_b = _attn_b(proj, batch, seq, t["attn"], t["b_keys"])

        kv_m = _mem_prep(mem2d, g_mem[l][None, :], w_mkv[l].astype(BF16), g_kn_m[l][None, :])
        u_m = _attn_m(proj, kv_m, g_qn_m[l][None, :], seq, mem_len, t["row"])

        mixed = _mix(u_a, u_b, u_m, w_pa[l].astype(BF16), w_pb[l].astype(BF16),
                     w_pm[l].astype(BF16), proj, t["mm_m"], t["mix_n"])
        xs = _out_proj(mixed, w_out[l].astype(BF16), xs, t["mm_m"], t["mix_n"])
    return xs.reshape(batch, seq, d)
```

```python
import functools
import math

import jax
import jax.numpy as jnp
from jax import lax
from jax.experimental import pallas as pl
from jax.experimental.pallas import tpu as pltpu

F32 = jnp.float32
BF16 = jnp.bfloat16

EPS = 1e-6
LOG2E = 1.4426950408889634
NEG = -1e30
ROPE_THETA = 10000.0
CHUNK = 64

A_HEADS, A_NOPE, A_ROPE, A_V = 16, 128, 64, 128
A_QK = A_NOPE + A_ROPE
A_PAD = 256
Q_LORA, KV_LORA = 1024, 512
B_HEADS, B_DIM = 8, 128
M_HEADS, M_DIM = 4, 256
D_MODEL = 4096
N_BRANCH = 3

LANE = 128
VMEM_LIMIT = 56 * 1024 * 1024


class _Cols:
    CQ = 0
    CKV = CQ + Q_LORA
    GA = CKV + KV_LORA
    QKVB = GA + A_HEADS * A_V
    GB = QKVB + 3 * B_HEADS * B_DIM
    QM = GB + B_HEADS * B_DIM
    GM = QM + M_HEADS * M_DIM
    MERGE = GM + M_HEADS * M_DIM
    KR = MERGE + N_BRANCH * D_MODEL
    USED = KR + LANE
    TOTAL = 22528


def _cparams(sem):
    return pltpu.CompilerParams(dimension_semantics=sem, vmem_limit_bytes=VMEM_LIMIT)


def _sigmoid(v):
    return 1.0 / (1.0 + jnp.exp(-v))


def _silu(v):
    return v * _sigmoid(v)


_NT = (((1,), (1,)), ((), ()))


def _rope_table_kernel(pos_ref, freq_ref, o_ref):
    ang = pos_ref[...] * freq_ref[...]
    lane = lax.broadcasted_iota(jnp.int32, ang.shape, 1)
    o_ref[...] = jnp.where(lane < A_ROPE, jnp.cos(ang), jnp.sin(ang))


def _rope_table(pos_b, freq_row, tm):
    m = pos_b.shape[0]
    return pl.pallas_call(
        _rope_table_kernel,
        grid=(m // tm,),
        in_specs=[pl.BlockSpec((tm, LANE), lambda i: (i, 0)),
                  pl.BlockSpec((1, LANE), lambda i: (0, 0))],
        out_specs=pl.BlockSpec((tm, LANE), lambda i: (i, 0)),
        out_shape=jax.ShapeDtypeStruct((m, LANE), F32),
        compiler_params=_cparams(("parallel",)),
        name="rope_table",
    )(pos_b, freq_row)


def _norm_kernel(x_ref, g_ref, o_ref):
    x = x_ref[...]
    y = x * lax.rsqrt(jnp.mean(x * x, axis=-1, keepdims=True) + EPS)
    o_ref[...] = (y * g_ref[...]).astype(o_ref.dtype)


def _norm(x, g_row, tm):
    m, d = x.shape
    return pl.pallas_call(
        _norm_kernel,
        grid=(m // tm,),
        in_specs=[pl.BlockSpec((tm, d), lambda i: (i, 0)),
                  pl.BlockSpec((1, d), lambda i: (0, 0))],
        out_specs=pl.BlockSpec((tm, d), lambda i: (i, 0)),
        out_shape=jax.ShapeDtypeStruct((m, d), BF16),
        compiler_params=_cparams(("parallel",)),
        name="norm",
    )(x, g_row)


def _matmul_nt_kernel(a_ref, wt_ref, o_ref):
    o_ref[...] = lax.dot_general(a_ref[...], wt_ref[...], _NT,
                                 preferred_element_type=F32).astype(o_ref.dtype)


def _in_proj(h, wt, tm, tn):
    m, k = h.shape
    n = wt.shape[0]
    return pl.pallas_call(
        _matmul_nt_kernel,
        grid=(m // tm, n // tn),
        in_specs=[pl.BlockSpec((tm, k), lambda i, j: (i, 0)),
                  pl.BlockSpec((tn, k), lambda i, j: (j, 0))],
        out_specs=pl.BlockSpec((tm, tn), lambda i, j: (i, j)),
        out_shape=jax.ShapeDtypeStruct((m, n), BF16),
        compiler_params=_cparams(("parallel", "arbitrary")),
        name="in_proj",
    )(h, wt)


def _q_prep_kernel(c_ref, gl_ref, w_ref, gh_ref, tbl_ref, o_ref, *, scale):
    c = c_ref[...].astype(F32)
    cn = c * lax.rsqrt(jnp.mean(c * c, axis=-1, keepdims=True) + EPS) * gl_ref[...]
    cn = cn.astype(BF16)
    tbl = tbl_ref[...]
    g1 = gh_ref[:, :A_NOPE]
    g2t = gh_ref[:, A_NOPE:] * tbl
    lane = lax.broadcasted_iota(jnp.int32, tbl.shape, 1)
    for h in range(A_HEADS):
        acc = jnp.dot(cn, w_ref[:, h * A_PAD:(h + 1) * A_PAD], preferred_element_type=F32)
        v1 = acc[:, :A_NOPE]
        v2 = acc[:, A_NOPE:]
        ss = (jnp.sum(v1 * v1, axis=-1, keepdims=True)
              + jnp.sum(jnp.where(lane < A_ROPE, v2 * v2, 0.0), axis=-1, keepdims=True))
        rs = lax.rsqrt(ss * (1.0 / A_QK) + EPS) * scale
        o_ref[:, h * A_PAD:h * A_PAD + A_NOPE] = (v1 * g1 * rs).astype(o_ref.dtype)
        o_ref[:, h * A_PAD + A_NOPE:(h + 1) * A_PAD] = (v2 * g2t * rs).astype(o_ref.dtype)


def _q_prep(proj, gl_row, w_uq, gh_row, tbl, tm):
    m = proj.shape[0]
    n = A_HEADS * A_PAD
    return pl.pallas_call(
        functools.partial(_q_prep_kernel, scale=LOG2E / math.sqrt(A_QK)),
        grid=(m // tm,),
        in_specs=[pl.BlockSpec((tm, Q_LORA), lambda i: (i, _Cols.CQ // Q_LORA)),
                  pl.BlockSpec((1, Q_LORA), lambda i: (0, 0)),
                  pl.BlockSpec((Q_LORA, n), lambda i: (0, 0)),
                  pl.BlockSpec((1, A_PAD), lambda i: (0, 0)),
                  pl.BlockSpec((tm, LANE), lambda i: (i, 0))],
        out_specs=pl.BlockSpec((tm, n), lambda i: (i, 0)),
        out_shape=jax.ShapeDtypeStruct((m, n), BF16),
        compiler_params=_cparams(("parallel",)),
        name="q_prep",
    )(proj, gl_row, w_uq, gh_row, tbl)


def _kv_prep_kernel(c_ref, kr_ref, gl_ref, w_ref, gh_ref, tbl_ref, kt_ref, v_ref):
    c = c_ref[...].astype(F32)
    cn = c * lax.rsqrt(jnp.mean(c * c, axis=-1, keepdims=True) + EPS) * gl_ref[...]
    cn = cn.astype(BF16)
    kr = kr_ref[...].astype(F32)
    lane = lax.broadcasted_iota(jnp.int32, kr.shape, 1)
    ss_r = jnp.sum(jnp.where(lane < A_ROPE, kr * kr, 0.0), axis=-1, keepdims=True)
    ab = kr * gh_ref[:, A_NOPE:] * tbl_ref[...]
    kk = ab + pltpu.roll(ab, A_ROPE, axis=1)
    g1 = gh_ref[:, :A_NOPE]
    ones = jnp.ones((kr.shape[0], A_PAD - A_V), v_ref.dtype)
    for h in range(A_HEADS):
        acc = jnp.dot(cn, w_ref[:, h * A_PAD:(h + 1) * A_PAD], preferred_element_type=F32)
        kn = acc[:, :A_NOPE]
        ss = jnp.sum(kn * kn, axis=-1, keepdims=True) + ss_r
        rs = lax.rsqrt(ss * (1.0 / A_QK) + EPS)
        kt_ref[h * A_PAD:h * A_PAD + A_NOPE, :] = (kn * g1 * rs).T.astype(kt_ref.dtype)
        kt_ref[h * A_PAD + A_NOPE:(h + 1) * A_PAD, :] = (kk * rs).T.astype(kt_ref.dtype)
        v_ref[:, h * A_PAD:h * A_PAD + A_V] = acc[:, A_NOPE:].astype(v_ref.dtype)
        v_ref[:, h * A_PAD + A_V:(h + 1) * A_PAD] = ones


def _kv_prep(proj, gl_row, w_ukv, gh_row, tbl, tm):
    m = proj.shape[0]
    nk = A_HEADS * A_PAD
    nv = A_HEADS * A_PAD
    return pl.pallas_call(
        _kv_prep_kernel,
        grid=(m // tm,),
        in_specs=[pl.BlockSpec((tm, KV_LORA), lambda i: (i, _Cols.CKV // KV_LORA)),
                  pl.BlockSpec((tm, LANE), lambda i: (i, _Cols.KR // LANE)),
                  pl.BlockSpec((1, KV_LORA), lambda i: (0, 0)),
                  pl.BlockSpec((KV_LORA, nk), lambda i: (0, 0)),
                  pl.BlockSpec((1, A_PAD), lambda i: (0, 0)),
                  pl.BlockSpec((tm, LANE), lambda i: (i, 0))],
        out_specs=[pl.BlockSpec((nk, tm), lambda i: (0, i)),
                   pl.BlockSpec((tm, nv), lambda i: (i, 0))],
        out_shape=[jax.ShapeDtypeStruct((nk, m), BF16),
                   jax.ShapeDtypeStruct((m, nv), BF16)],
        compiler_params=_cparams(("parallel",)),
        name="kv_prep",
    )(proj, proj, gl_row, w_ukv, gh_row, tbl)


def _attn_a_kernel(q_ref, kt_ref, v_ref, g_ref, o_ref, s_ref, p_ref, *, tq):
    qi = pl.program_id(2)
    q = q_ref[...]

    n_units = qi + 1

    def key_block(unit):
        unit = jnp.clip(unit, 0, qi)
        return jnp.where(unit == 0, qi, unit - 1)

    def scores(unit):
        start = pl.multiple_of(key_block(unit) * tq, tq)
        return jnp.dot(q, kt_ref[:, pl.ds(start, tq)], preferred_element_type=F32)

    def p_times_v(unit, slot):
        start = pl.multiple_of(key_block(unit) * tq, tq)
        return jnp.dot(p_ref[slot], v_ref[pl.ds(start, tq), :], preferred_element_type=F32)

    def softmax(slot, m, acc):
        s = s_ref[slot]
        m_new = jnp.maximum(m, jnp.max(s, axis=-1, keepdims=True))
        p_ref[slot] = jnp.exp2(s - m_new).astype(BF16)
        return m_new, jnp.exp2(m - m_new) * acc

    def step(i, slot, state):
        m, pend = state
        s_ref[slot] = jnp.where(i < n_units, scores(i), NEG)
        acc = pend + p_times_v(i - 2, slot)
        return softmax(1 - slot, m, acc)

    def steps_from(base, count, state):
        for k in range(count):
            state = step(base + 1 + k, (1 + k) % 2, state)
        return state

    qc = lax.broadcasted_iota(jnp.int32, (tq, tq), 0) // CHUNK
    kc = lax.broadcasted_iota(jnp.int32, (tq, tq), 1) // CHUNK
    s_ref[0] = jnp.where(kc <= qc, scores(0), -jnp.inf)
    p_ref[1] = jnp.zeros((tq, tq), BF16)
    state = (jnp.full((tq, 1), NEG, F32), jnp.zeros((tq, A_PAD), F32))
    quads = (n_units + 1) // 4
    pairs = (n_units + 2 - 4 * quads) // 2
    state = lax.fori_loop(0, quads, lambda t, st: steps_from(4 * t, 4, st), state)
    _, acc = lax.fori_loop(0, pairs, lambda t, st: steps_from(4 * quads + 2 * t, 2, st), state)
    o = acc[:, :A_V] / acc[:, A_V:]
    o_ref[...] = (o * _silu(g_ref[...].astype(F32))).astype(o_ref.dtype)


def _attn_a(q_a, kt_a, v_a, proj, batch, seq, tq):
    m = q_a.shape[0]
    nq = seq // tq
    gate0 = _Cols.GA // A_V
    return pl.pallas_call(
        functools.partial(_attn_a_kernel, tq=tq),
        grid=(batch, A_HEADS, nq),
        in_specs=[pl.BlockSpec((tq, A_PAD), lambda b, h, i: (b * nq + i, h)),
                  pl.BlockSpec((A_PAD, seq), lambda b, h, i: (h, b)),
                  pl.BlockSpec((seq, A_PAD), lambda b, h, i: (b, h)),
                  pl.BlockSpec((tq, A_V), lambda b, h, i: (b * nq + i, gate0 + h))],
        out_specs=pl.BlockSpec((tq, A_V), lambda b, h, i: (b * nq + i, h)),
        out_shape=jax.ShapeDtypeStruct((m, A_HEADS * A_V), BF16),
        scratch_shapes=[pltpu.VMEM((2, tq, tq), F32), pltpu.VMEM((2, tq, tq), BF16)],
        compiler_params=_cparams(("parallel", "parallel", "arbitrary")),
        name="attn_a",
    )(q_a, kt_a, v_a, proj)


def _attn_b_kernel(q_ref, k_ref, v_ref, g_ref, o_ref,
                   z_ref, zc_ref, hl_ref, w_ref, *, tq, ks, scale2):
    qi = pl.program_id(2)
    q = q_ref[...]
    n_units = (tq // ks) * (qi + 1)
    r = lax.broadcasted_iota(jnp.int32, (ks, ks), 0)
    c = lax.broadcasted_iota(jnp.int32, (ks, ks), 1)
    tri = jnp.where(r >= c, 1.0, 0.0).astype(BF16)

    def key_start(unit):
        unit = jnp.clip(unit, 0, n_units - 1)
        return pl.multiple_of((n_units - 1 - unit) * ks, ks)

    def stage_scores(unit, slot, own_block):
        start = key_start(unit)
        z = lax.dot_general(q, k_ref[pl.ds(start, ks), :], _NT, preferred_element_type=F32)
        if own_block:
            d = (lax.broadcasted_iota(jnp.int32, (tq, ks), 1)
                 - lax.broadcasted_iota(jnp.int32, (tq, ks), 0))
            keep = d < qi * tq - start
        else:
            keep = unit < n_units
        z_ref[slot] = jnp.where(keep, z * scale2, NEG)

    def stage_logs(slot):
        z = z_ref[slot]
        neg_abs = lax.bitcast_convert_type(
            lax.bitcast_convert_type(z, jnp.uint32) | jnp.uint32(0x80000000), F32)
        nlf = jnp.maximum(z, 0.0) + jnp.log(1.0 + jnp.exp2(neg_abs)) * LOG2E
        hl_ref[slot] = nlf.astype(BF16)
        zc_ref[slot] = z

    def stage_weights(slot, mass):
        tail = jnp.dot(hl_ref[slot], tri, preferred_element_type=F32)
        w_ref[slot] = jnp.exp2(zc_ref[slot] - tail - mass).astype(BF16)
        return mass + tail[:, :1]

    def stage_pv(unit, slot, acc):
        v = v_ref[pl.ds(key_start(unit), ks), :]
        return acc + jnp.dot(w_ref[slot], v, preferred_element_type=F32)

    def step(i, slot, state, own_block=False):
        mass, acc = state
        stage_scores(i, slot, own_block)
        acc = stage_pv(i - 3, 1 - slot, acc)
        mass = stage_weights(slot, mass)
        stage_logs(1 - slot)
        return mass, acc

    def steps_from(base, count, state):
        for k in range(count):
            state = step(base + k, k % 2, state)
        return state

    z_ref[1] = jnp.full((tq, ks), NEG, F32)
    zc_ref[0] = jnp.full((tq, ks), NEG, F32)
    hl_ref[0] = jnp.zeros((tq, ks), BF16)
    w_ref[1] = jnp.zeros((tq, ks), BF16)
    state = (jnp.zeros((tq, 1), F32), jnp.zeros((tq, B_DIM), F32))
    assert tq == 2 * ks, "units 0 and 1 are exactly the q block's own key block"
    state = step(1, 1, step(0, 0, state, own_block=True), own_block=True)
    quads = (n_units + 1) // 4
    pairs = (n_units + 2 - 4 * quads) // 2
    state = lax.fori_loop(0, quads, lambda t, st: steps_from(2 + 4 * t, 4, st), state)
    _, acc = lax.fori_loop(0, pairs, lambda t, st: steps_from(2 + 4 * quads + 2 * t, 2, st), state)
    o_ref[...] = (acc * _silu(g_ref[...].astype(F32))).astype(o_ref.dtype)


def _attn_b(proj, batch, seq, tq, ks):
    m = proj.shape[0]
    nq = seq // tq
    q0 = _Cols.QKVB // B_DIM
    k0 = q0 + B_HEADS
    v0 = k0 + B_HEADS
    g0 = _Cols.GB // B_DIM
    return pl.pallas_call(
        functools.partial(_attn_b_kernel, tq=tq, ks=ks, scale2=LOG2E / math.sqrt(B_DIM)),
        grid=(batch, B_HEADS, nq),
        in_specs=[pl.BlockSpec((tq, B_DIM), lambda b, h, i: (b * nq + i, q0 + h)),
                  pl.BlockSpec((seq, B_DIM), lambda b, h, i: (b, k0 + h)),
                  pl.BlockSpec((seq, B_DIM), lambda b, h, i: (b, v0 + h)),
                  pl.BlockSpec((tq, B_DIM), lambda b, h, i: (b * nq + i, g0 + h))],
        out_specs=pl.BlockSpec((tq, B_DIM), lambda b, h, i: (b * nq + i, h)),
        out_shape=jax.ShapeDtypeStruct((m, B_HEADS * B_DIM), BF16),
        scratch_shapes=[pltpu.VMEM((2, tq, ks), F32), pltpu.VMEM((2, tq, ks), F32),
                        pltpu.VMEM((2, tq, ks), BF16), pltpu.VMEM((2, tq, ks), BF16)],
        compiler_params=_cparams(("parallel", "parallel", "arbitrary")),
        name="attn_b",
    )(proj, proj, proj, proj)


def _mem_prep_kernel(mem_ref, g_ref, w_ref, gk_ref, o_ref):
    j = pl.program_id(0)
    x = mem_ref[...]
    xn = (x * lax.rsqrt(jnp.mean(x * x, axis=-1, keepdims=True) + EPS) * g_ref[...]).astype(BF16)
    y = jnp.dot(xn, w_ref[...], preferred_element_type=F32)
    yk = y * lax.rsqrt(jnp.mean(y * y, axis=-1, keepdims=True) + EPS) * gk_ref[...]
    o_ref[...] = jnp.where(j < M_HEADS, yk, y).astype(o_ref.dtype)


def _mem_prep(mem2d, g_row, w_mkv, gk_row):
    mm, d = mem2d.shape
    n = 2 * M_HEADS * M_DIM
    return pl.pallas_call(
        _mem_prep_kernel,
        grid=(2 * M_HEADS,),
        in_specs=[pl.BlockSpec((mm, d), lambda j: (0, 0)),
                  pl.BlockSpec((1, d), lambda j: (0, 0)),
                  pl.BlockSpec((d, M_DIM), lambda j: (0, j)),
                  pl.BlockSpec((1, M_DIM), lambda j: (0, 0))],
        out_specs=pl.BlockSpec((mm, M_DIM), lambda j: (0, j)),
        out_shape=jax.ShapeDtypeStruct((mm, n), BF16),
        compiler_params=_cparams(("arbitrary",)),
        name="mem_prep",
    )(mem2d, g_row, w_mkv, gk_row)


def _attn_m_kernel(q_ref, k_ref, v_ref, g_ref, gq_ref, o_ref, *, scale):
    q = q_ref[...].astype(F32)
    qn = q * (lax.rsqrt(jnp.mean(q * q, axis=-1, keepdims=True) + EPS) * scale) * gq_ref[...]
    s = lax.dot_general(qn.astype(BF16), k_ref[...], _NT, preferred_element_type=F32)
    p = jnp.exp(s - jnp.max(s, axis=-1, keepdims=True))
    l = jnp.sum(p, axis=-1, keepdims=True)
    o = jnp.dot(p.astype(BF16), v_ref[...], preferred_element_type=F32) / l
    o_ref[...] = (o * _silu(g_ref[...].astype(F32))).astype(o_ref.dtype)


def _attn_m(proj, kv_m, gq_row, seq, mem_len, tm):
    m = proj.shape[0]
    nb = seq // tm
    q0 = _Cols.QM // M_DIM
    g0 = _Cols.GM // M_DIM
    assert mem_len == M_DIM, "memory block spec assumes MEM_LEN == M_DIM rows per batch"
    return pl.pallas_call(
        functools.partial(_attn_m_kernel, scale=1.0 / math.sqrt(M_DIM)),
        grid=(m // tm, M_HEADS),
        in_specs=[pl.BlockSpec((tm, M_DIM), lambda i, h: (i, q0 + h)),
                  pl.BlockSpec((mem_len, M_DIM), lambda i, h: (i // nb, h)),
                  pl.BlockSpec((mem_len, M_DIM), lambda i, h: (i // nb, M_HEADS + h)),
                  pl.BlockSpec((tm, M_DIM), lambda i, h: (i, g0 + h)),
                  pl.BlockSpec((1, M_DIM), lambda i, h: (0, 0))],
        out_specs=pl.BlockSpec((tm, M_DIM), lambda i, h: (i, h)),
        out_shape=jax.ShapeDtypeStruct((m, M_HEADS * M_DIM), BF16),
        compiler_params=_cparams(("parallel", "arbitrary")),
        name="attn_m",
    )(proj, kv_m, kv_m, proj, gq_row)


def _mix_kernel(ua_ref, ub_ref, um_ref, wa_ref, wb_ref, wm_ref, ra_ref, rb_ref, rm_ref, o_ref):
    ya = jnp.dot(ua_ref[...], wa_ref[...], preferred_element_type=F32)
    yb = jnp.dot(ub_ref[...], wb_ref[...], preferred_element_type=F32)
    ym = jnp.dot(um_ref[...], wm_ref[...], preferred_element_type=F32)
    mixed = (_sigmoid(ra_ref[...].astype(F32)) * ya + _sigmoid(rb_ref[...].astype(F32)) * yb
             + _sigmoid(rm_ref[...].astype(F32)) * ym)
    o_ref[...] = mixed.astype(o_ref.dtype)


def _mix(u_a, u_b, u_m, w_pa, w_pb, w_pm, proj, tm, tn):
    m = u_a.shape[0]
    r0 = _Cols.MERGE // tn
    rstep = D_MODEL // tn

    def rspec(branch):
        return pl.BlockSpec((tm, tn), lambda i, j: (i, r0 + branch * rstep + j))

    return pl.pallas_call(
        _mix_kernel,
        grid=(m // tm, D_MODEL // tn),
        in_specs=[pl.BlockSpec((tm, u_a.shape[1]), lambda i, j: (i, 0)),
                  pl.BlockSpec((tm, u_b.shape[1]), lambda i, j: (i, 0)),
                  pl.BlockSpec((tm, u_m.shape[1]), lambda i, j: (i, 0)),
                  pl.BlockSpec((w_pa.shape[0], tn), lambda i, j: (0, j)),
                  pl.BlockSpec((w_pb.shape[0], tn), lambda i, j: (0, j)),
                  pl.BlockSpec((w_pm.shape[0], tn), lambda i, j: (0, j)),
                  rspec(0), rspec(1), rspec(2)],
        out_specs=pl.BlockSpec((tm, tn), lambda i, j: (i, j)),
        out_shape=jax.ShapeDtypeStruct((m, D_MODEL), BF16),
        compiler_params=_cparams(("parallel", "arbitrary")),
        name="mix",
    )(u_a, u_b, u_m, w_pa, w_pb, w_pm, proj, proj, proj)


def _out_kernel(a_ref, w_ref, x_ref, o_ref):
    o_ref[...] = x_ref[...] + jnp.dot(a_ref[...], w_ref[...], preferred_element_type=F32)


def _out_proj(mixed, w_out, x, tm, tn):
    m, k = mixed.shape
    n = w_out.shape[1]
    return pl.pallas_call(
        _out_kernel,
        grid=(m // tm, n // tn),
        in_specs=[pl.BlockSpec((tm, k), lambda i, j: (i, 0)),
                  pl.BlockSpec((k, tn), lambda i, j: (0, j)),
                  pl.BlockSpec((tm, tn), lambda i, j: (i, j))],
        out_specs=pl.BlockSpec((tm, tn), lambda i, j: (i, j)),
        out_shape=jax.ShapeDtypeStruct((m, n), F32),
        compiler_params=_cparams(("parallel", "arbitrary")),
        name="out_proj",
    )(mixed, w_out, x)


def _rotate_half_cols(w):
    half = A_ROPE // 2
    return jnp.concatenate([-w[..., half:], w[..., :half]], axis=-1)


def _rope_gain(g_rope):
    half = A_ROPE // 2
    return jnp.concatenate([g_rope, g_rope[half:], g_rope[:half]])


def _prep_w_in(w_in):
    wt = w_in.T.astype(BF16)
    kr = wt[1536:1600]
    rot = _rotate_half_cols(kr.T).T
    pad = jnp.zeros((_Cols.TOTAL - _Cols.USED, wt.shape[1]), wt.dtype)
    return jnp.concatenate([wt[:1536], wt[1600:], kr, rot, pad], axis=0)


def _prep_w_uq(w_uq):
    w = w_uq.reshape(Q_LORA, A_HEADS, A_QK)
    rope = w[..., A_NOPE:]
    w = jnp.concatenate([w[..., :A_NOPE], rope, _rotate_half_cols(rope)], axis=-1)
    return w.reshape(Q_LORA, A_HEADS * A_PAD).astype(BF16)


def _head_gain(g):
    return jnp.concatenate([g[:A_NOPE], _rope_gain(g[A_NOPE:])])[None, :]


def _tiles(m, seq):
    return dict(
        row=min(512, m),
        mm_m=min(1024, m),
        mm_n=1024,
        mix_n=512,
        attn=min(512, seq // 2),
        b_keys=min(256, seq // 4),
    )


def kernel(x, mem, positions, g_pre, w_in, g_q_lat, w_uq, g_kv_lat, w_ukv, g_qn_a, g_kn_a,
           w_pa, w_pb, g_mem, w_mkv, g_qn_m, g_kn_m, w_pm, w_out):
    batch, seq, d = x.shape
    mem_len = mem.shape[1]
    depth = w_in.shape[0]
    m = batch * seq
    t = _tiles(m, seq)

    half = A_ROPE // 2
    freqs = ROPE_THETA ** (-jnp.arange(half, dtype=F32) / half)
    freq_row = jnp.tile(freqs, LANE // half)[None, :]
    pos_b = jnp.broadcast_to(positions.reshape(m, 1).astype(F32), (m, LANE))
    tbl = _rope_table(pos_b, freq_row, t["row"])

    xs = x.reshape(m, d)
    mem2d = mem.reshape(batch * mem_len, d)
    for l in range(depth):
        w_in_l = _prep_w_in(w_in[l])
        h = _norm(xs, g_pre[l][None, :], t["row"])
        proj = _in_proj(h, w_in_l, t["mm_m"], t["mm_n"])

        q_a = _q_prep(proj, g_q_lat[l][None, :], _prep_w_uq(w_uq[l]), _head_gain(g_qn_a[l]),
                      tbl, t["row"])
        k_a, v_a = _kv_prep(proj, g_kv_lat[l][None, :], w_ukv[l].astype(BF16),
                            _head_gain(g_kn_a[l]), tbl, t["row"])
        u_a = _attn_a(q_a, k_a, v_a, proj, batch, seq, t["attn"])
        u_b = _attn_b(proj, batch, seq, t["attn"], t["b_keys"])

        kv_m = _mem_prep(mem2d, g_mem[l][None, :], w_mkv[l].astype(BF16), g_kn_m[l][None, :])
        u_m = _attn_m(proj, kv_m, g_qn_m[l][None, :], seq, mem_len, t["row"])

        mixed = _mix(u_a, u_b, u_m, w_pa[l].astype(BF16), w_pb[l].astype(BF16),
                     w_pm[l].astype(BF16), proj, t["mm_m"], t["mix_n"])
        xs = _out_proj(mixed, w_out[l].astype(BF16), xs, t["mm_m"], t["mix_n"])
    return xs.reshape(batch, seq, d)
```

```python
import functools
import math

import jax
import jax.numpy as jnp
from jax import lax
from jax.experimental import pallas as pl
from jax.experimental.pallas import tpu as pltpu

F32 = jnp.float32
BF16 = jnp.bfloat16

EPS = 1e-6
LOG2E = 1.4426950408889634
NEG = -1e30
ROPE_THETA = 10000.0
CHUNK = 64

A_HEADS, A_NOPE, A_ROPE, A_V = 16, 128, 64, 128
A_QK = A_NOPE + A_ROPE
A_PAD = 256
Q_LORA, KV_LORA = 1024, 512
B_HEADS, B_DIM = 8, 128
M_HEADS, M_DIM = 4, 256
D_MODEL = 4096
N_BRANCH = 3

LANE = 128
VMEM_LIMIT = 56 * 1024 * 1024


class _Cols:
    CQ = 0
    CKV = CQ + Q_LORA
    GA = CKV + KV_LORA
    QKVB = GA + A_HEADS * A_V
    GB = QKVB + 3 * B_HEADS * B_DIM
    QM = GB + B_HEADS * B_DIM
    GM = QM + M_HEADS * M_DIM
    MERGE = GM + M_HEADS * M_DIM
    KR = MERGE + N_BRANCH * D_MODEL
    USED = KR + LANE
    TOTAL = 22528


def _cparams(sem):
    return pltpu.CompilerParams(dimension_semantics=sem, vmem_limit_bytes=VMEM_LIMIT)


def _sigmoid(v):
    return 1.0 / (1.0 + jnp.exp(-v))


def _silu(v):
    return v * _sigmoid(v)


_NT = (((1,), (1,)), ((), ()))


def _rope_table_kernel(pos_ref, freq_ref, o_ref):
    ang = pos_ref[...] * freq_ref[...]
    lane = lax.broadcasted_iota(jnp.int32, ang.shape, 1)
    o_ref[...] = jnp.where(lane < A_ROPE, jnp.cos(ang), jnp.sin(ang))


def _rope_table(pos_b, freq_row, tm):
    m = pos_b.shape[0]
    return pl.pallas_call(
        _rope_table_kernel,
        grid=(m // tm,),
        in_specs=[pl.BlockSpec((tm, LANE), lambda i: (i, 0)),
                  pl.BlockSpec((1, LANE), lambda i: (0, 0))],
        out_specs=pl.BlockSpec((tm, LANE), lambda i: (i, 0)),
        out_shape=jax.ShapeDtypeStruct((m, LANE), F32),
        compiler_params=_cparams(("parallel",)),
        name="rope_table",
    )(pos_b, freq_row)


def _norm_kernel(x_ref, g_ref, o_ref):
    x = x_ref[...]
    y = x * lax.rsqrt(jnp.mean(x * x, axis=-1, keepdims=True) + EPS)
    o_ref[...] = (y * g_ref[...]).astype(o_ref.dtype)


def _norm(x, g_row, tm):
    m, d = x.shape
    return pl.pallas_call(
        _norm_kernel,
        grid=(m // tm,),
        in_specs=[pl.BlockSpec((tm, d), lambda i: (i, 0)),
                  pl.BlockSpec((1, d), lambda i: (0, 0))],
        out_specs=pl.BlockSpec((tm, d), lambda i: (i, 0)),
        out_shape=jax.ShapeDtypeStruct((m, d), BF16),
        compiler_params=_cparams(("parallel",)),
        name="norm",
    )(x, g_row)


IN_TN = 1024
KR_SRC = Q_LORA + KV_LORA


def _in_proj_kernel(a_ref, w_ref, wnext_ref, wkr_ref, o_ref, wp_ref):
    j = pl.program_id(0)
    last = pl.num_programs(0) - 1
    cut = KR_SRC % IN_TN
    tail = _Cols.KR % IN_TN
    half = A_ROPE // 2
    assert KR_SRC // IN_TN == 1 and _Cols.KR // IN_TN == _Cols.TOTAL // IN_TN - 1

    @pl.when(pl.program_id(1) == 0)
    def _assemble():
        @pl.when(j == 0)
        def _():
            wp_ref[...] = w_ref[...].astype(BF16)

        @pl.when(j == 1)
        def _():
            wp_ref[:cut, :] = w_ref[:cut, :].astype(BF16)
            wp_ref[cut:IN_TN - A_ROPE, :] = w_ref[cut + A_ROPE:, :].astype(BF16)
            wp_ref[IN_TN - A_ROPE:, :] = wnext_ref[...].astype(BF16)

        @pl.when(jnp.logical_and(j > 1, j < last))
        def _():
            wp_ref[:IN_TN - A_ROPE, :] = w_ref[A_ROPE:, :].astype(BF16)
            wp_ref[IN_TN - A_ROPE:, :] = wnext_ref[...].astype(BF16)

        @pl.when(j == last)
        def _():
            kr = wkr_ref[...]
            wp_ref[:tail, :] = w_ref[A_ROPE:A_ROPE + tail, :].astype(BF16)
            wp_ref[tail:tail + A_ROPE, :] = kr.astype(BF16)
            wp_ref[tail + A_ROPE:tail + A_ROPE + half, :] = (-kr[half:, :]).astype(BF16)
            wp_ref[tail + A_ROPE + half:tail + 2 * A_ROPE, :] = kr[:half, :].astype(BF16)
            wp_ref[tail + 2 * A_ROPE:, :] = jnp.zeros((IN_TN - tail - 2 * A_ROPE, wp_ref.shape[1]),
                                                      BF16)

    o_ref[...] = lax.dot_general(a_ref[...], wp_ref[...], _NT,
                                 preferred_element_type=F32).astype(o_ref.dtype)


def _in_proj(h, w_in_t, layer, tm):
    m, k = h.shape
    n_src = w_in_t.shape[1]
    rope_blocks = IN_TN // A_ROPE
    last_rope_block = n_src // A_ROPE - 1
    return pl.pallas_call(
        _in_proj_kernel,
        grid=(_Cols.TOTAL // IN_TN, m // tm),
        in_specs=[pl.BlockSpec((tm, k), lambda j, i: (i, 0)),
                  pl.BlockSpec((None, IN_TN, k), lambda j, i: (layer, j, 0),
                               pipeline_mode=pl.Buffered(1)),
                  pl.BlockSpec((None, A_ROPE, k),
                               lambda j, i: (layer, jnp.minimum((j + 1) * rope_blocks,
                                                                last_rope_block), 0)),
                  pl.BlockSpec((None, A_ROPE, k), lambda j, i: (layer, KR_SRC // A_ROPE, 0))],
        out_specs=pl.BlockSpec((tm, IN_TN), lambda j, i: (i, j)),
        out_shape=jax.ShapeDtypeStruct((m, _Cols.TOTAL), BF16),
        scratch_shapes=[pltpu.VMEM((IN_TN, k), BF16)],
        compiler_params=_cparams(("arbitrary", "arbitrary")),
        name="in_proj",
    )(h, w_in_t, w_in_t, w_in_t)


def _q_prep_kernel(c_ref, gl_ref, w_ref, gh_ref, tbl_ref, o_ref, *, scale):
    c = c_ref[...].astype(F32)
    cn = c * lax.rsqrt(jnp.mean(c * c, axis=-1, keepdims=True) + EPS) * gl_ref[...]
    cn = cn.astype(BF16)
    tbl = tbl_ref[...]
    g1 = gh_ref[:, :A_NOPE]
    g2t = gh_ref[:, A_NOPE:] * tbl
    lane = lax.broadcasted_iota(jnp.int32, tbl.shape, 1)
    for h in range(A_HEADS):
        acc = jnp.dot(cn, w_ref[:, h * A_PAD:(h + 1) * A_PAD], preferred_element_type=F32)
        v1 = acc[:, :A_NOPE]
        v2 = acc[:, A_NOPE:]
        ss = (jnp.sum(v1 * v1, axis=-1, keepdims=True)
              + jnp.sum(jnp.where(lane < A_ROPE, v2 * v2, 0.0), axis=-1, keepdims=True))
        rs = lax.rsqrt(ss * (1.0 / A_QK) + EPS) * scale
        o_ref[:, h * A_PAD:h * A_PAD + A_NOPE] = (v1 * g1 * rs).astype(o_ref.dtype)
        o_ref[:, h * A_PAD + A_NOPE:(h + 1) * A_PAD] = (v2 * g2t * rs).astype(o_ref.dtype)


def _q_prep(proj, gl_row, w_uq, gh_row, tbl, tm):
    m = proj.shape[0]
    n = A_HEADS * A_PAD
    return pl.pallas_call(
        functools.partial(_q_prep_kernel, scale=LOG2E / math.sqrt(A_QK)),
        grid=(m // tm,),
        in_specs=[pl.BlockSpec((tm, Q_LORA), lambda i: (i, _Cols.CQ // Q_LORA)),
                  pl.BlockSpec((1, Q_LORA), lambda i: (0, 0)),
                  pl.BlockSpec((Q_LORA, n), lambda i: (0, 0)),
                  pl.BlockSpec((1, A_PAD), lambda i: (0, 0)),
                  pl.BlockSpec((tm, LANE), lambda i: (i, 0))],
        out_specs=pl.BlockSpec((tm, n), lambda i: (i, 0)),
        out_shape=jax.ShapeDtypeStruct((m, n), BF16),
        compiler_params=_cparams(("parallel",)),
        name="q_prep",
    )(proj, gl_row, w_uq, gh_row, tbl)


def _kv_prep_kernel(c_ref, kr_ref, gl_ref, w_ref, gh_ref, tbl_ref, kt_ref, v_ref):
    c = c_ref[...].astype(F32)
    cn = c * lax.rsqrt(jnp.mean(c * c, axis=-1, keepdims=True) + EPS) * gl_ref[...]
    cn = cn.astype(BF16)
    kr = kr_ref[...].astype(F32)
    lane = lax.broadcasted_iota(jnp.int32, kr.shape, 1)
    ss_r = jnp.sum(jnp.where(lane < A_ROPE, kr * kr, 0.0), axis=-1, keepdims=True)
    ab = kr * gh_ref[:, A_NOPE:] * tbl_ref[...]
    kk = ab + pltpu.roll(ab, A_ROPE, axis=1)
    g1 = gh_ref[:, :A_NOPE]
    ones = jnp.ones((kr.shape[0], A_PAD - A_V), v_ref.dtype)
    for h in range(A_HEADS):
        acc = jnp.dot(cn, w_ref[:, h * A_PAD:(h + 1) * A_PAD], preferred_element_type=F32)
        kn = acc[:, :A_NOPE]
        ss = jnp.sum(kn * kn, axis=-1, keepdims=True) + ss_r
        rs = lax.rsqrt(ss * (1.0 / A_QK) + EPS)
        kt_ref[h * A_PAD:h * A_PAD + A_NOPE, :] = (kn * g1 * rs).T.astype(kt_ref.dtype)
        kt_ref[h * A_PAD + A_NOPE:(h + 1) * A_PAD, :] = (kk * rs).T.astype(kt_ref.dtype)
        v_ref[:, h * A_PAD:h * A_PAD + A_V] = acc[:, A_NOPE:].astype(v_ref.dtype)
        v_ref[:, h * A_PAD + A_V:(h + 1) * A_PAD] = ones


def _kv_prep(proj, gl_row, w_ukv, gh_row, tbl, tm):
    m = proj.shape[0]
    nk = A_HEADS * A_PAD
    nv = A_HEADS * A_PAD
    return pl.pallas_call(
        _kv_prep_kernel,
        grid=(m // tm,),
        in_specs=[pl.BlockSpec((tm, KV_LORA), lambda i: (i, _Cols.CKV // KV_LORA)),
                  pl.BlockSpec((tm, LANE), lambda i: (i, _Cols.KR // LANE)),
                  pl.BlockSpec((1, KV_LORA), lambda i: (0, 0)),
                  pl.BlockSpec((KV_LORA, nk), lambda i: (0, 0)),
                  pl.BlockSpec((1, A_PAD), lambda i: (0, 0)),
                  pl.BlockSpec((tm, LANE), lambda i: (i, 0))],
        out_specs=[pl.BlockSpec((nk, tm), lambda i: (0, i)),
                   pl.BlockSpec((tm, nv), lambda i: (i, 0))],
        out_shape=[jax.ShapeDtypeStruct((nk, m), BF16),
                   jax.ShapeDtypeStruct((m, nv), BF16)],
        compiler_params=_cparams(("parallel",)),
        name="kv_prep",
    )(proj, proj, gl_row, w_ukv, gh_row, tbl)


def _attn_a_kernel(q_ref, kt_ref, v_ref, g_ref, o_ref, s_ref, p_ref, *, tq):
    qi = pl.program_id(2)
    q = q_ref[...]

    n_units = qi + 1

    def key_block(unit):
        unit = jnp.clip(unit, 0, qi)
        return jnp.where(unit == 0, qi, unit - 1)

    def scores(unit):
        start = pl.multiple_of(key_block(unit) * tq, tq)
        return jnp.dot(q, kt_ref[:, pl.ds(start, tq)], preferred_element_type=F32)

    def p_times_v(unit, slot):
        start = pl.multiple_of(key_block(unit) * tq, tq)
        return jnp.dot(p_ref[slot], v_ref[pl.ds(start, tq), :], preferred_element_type=F32)

    def softmax(slot, m, acc):
        s = s_ref[slot]
        m_new = jnp.maximum(m, jnp.max(s, axis=-1, keepdims=True))
        p_ref[slot] = jnp.exp2(s - m_new).astype(BF16)
        return m_new, jnp.exp2(m - m_new) * acc

    def step(i, slot, state):
        m, pend = state
        s_ref[slot] = jnp.where(i < n_units, scores(i), NEG)
        acc = pend + p_times_v(i - 2, slot)
        return softmax(1 - slot, m, acc)

    def steps_from(base, count, state):
        for k in range(count):
            state = step(base + 1 + k, (1 + k) % 2, state)
        return state

    qc = lax.broadcasted_iota(jnp.int32, (tq, tq), 0) // CHUNK
    kc = lax.broadcasted_iota(jnp.int32, (tq, tq), 1) // CHUNK
    s_ref[0] = jnp.where(kc <= qc, scores(0), -jnp.inf)
    p_ref[1] = jnp.zeros((tq, tq), BF16)
    state = (jnp.full((tq, 1), NEG, F32), jnp.zeros((tq, A_PAD), F32))
    quads = (n_units + 1) // 4
    pairs = (n_units + 2 - 4 * quads) // 2
    state = lax.fori_loop(0, quads, lambda t, st: steps_from(4 * t, 4, st), state)
    _, acc = lax.fori_loop(0, pairs, lambda t, st: steps_from(4 * quads + 2 * t, 2, st), state)
    o = acc[:, :A_V] / acc[:, A_V:]
    o_ref[...] = (o * _silu(g_ref[...].astype(F32))).astype(o_ref.dtype)


def _attn_a(q_a, kt_a, v_a, proj, batch, seq, tq):
    m = q_a.shape[0]
    nq = seq // tq
    gate0 = _Cols.GA // A_V
    return pl.pallas_call(
        functools.partial(_attn_a_kernel, tq=tq),
        grid=(batch, A_HEADS, nq),
        in_specs=[pl.BlockSpec((tq, A_PAD), lambda b, h, i: (b * nq + i, h)),
                  pl.BlockSpec((A_PAD, seq), lambda b, h, i: (h, b)),
                  pl.BlockSpec((seq, A_PAD), lambda b, h, i: (b, h)),
                  pl.BlockSpec((tq, A_V), lambda b, h, i: (b * nq + i, gate0 + h))],
        out_specs=pl.BlockSpec((tq, A_V), lambda b, h, i: (b * nq + i, h)),
        out_shape=jax.ShapeDtypeStruct((m, A_HEADS * A_V), BF16),
        scratch_shapes=[pltpu.VMEM((2, tq, tq), F32), pltpu.VMEM((2, tq, tq), BF16)],
        compiler_params=_cparams(("parallel", "parallel", "arbitrary")),
        name="attn_a",
    )(q_a, kt_a, v_a, proj)


def _attn_b_kernel(q_ref, k_ref, v_ref, g_ref, o_ref,
                   z_ref, zc_ref, hl_ref, w_ref, *, tq, ks, scale2):
    qi = pl.program_id(2)
    q = q_ref[...]
    n_units = (tq // ks) * (qi + 1)
    r = lax.broadcasted_iota(jnp.int32, (ks, ks), 0)
    c = lax.broadcasted_iota(jnp.int32, (ks, ks), 1)
    tri = jnp.where(r >= c, 1.0, 0.0).astype(BF16)

    def key_start(unit):
        unit = jnp.clip(unit, 0, n_units - 1)
        return pl.multiple_of((n_units - 1 - unit) * ks, ks)

    def stage_scores(unit, slot, own_block):
        start = key_start(unit)
        z = lax.dot_general(q, k_ref[pl.ds(start, ks), :], _NT, preferred_element_type=F32)
        if own_block:
            d = (lax.broadcasted_iota(jnp.int32, (tq, ks), 1)
                 - lax.broadcasted_iota(jnp.int32, (tq, ks), 0))
            keep = d < qi * tq - start
        else:
            keep = unit < n_units
        z_ref[slot] = jnp.where(keep, z * scale2, NEG)

    def stage_logs(slot):
        z = z_ref[slot]
        neg_abs = lax.bitcast_convert_type(
            lax.bitcast_convert_type(z, jnp.uint32) | jnp.uint32(0x80000000), F32)
        nlf = jnp.maximum(z, 0.0) + jnp.log(1.0 + jnp.exp2(neg_abs)) * LOG2E
        hl_ref[slot] = nlf.astype(BF16)
        zc_ref[slot] = z

    def stage_weights(slot, mass):
        tail = jnp.dot(hl_ref[slot], tri, preferred_element_type=F32)
        w_ref[slot] = jnp.exp2(zc_ref[slot] - tail - mass).astype(BF16)
        return mass + tail[:, :1]

    def stage_pv(unit, slot, acc):
        v = v_ref[pl.ds(key_start(unit), ks), :]
        return acc + jnp.dot(w_ref[slot], v, preferred_element_type=F32)

    def step(i, slot, state, own_block=False):
        mass, acc = state
        stage_scores(i, slot, own_block)
        acc = stage_pv(i - 3, 1 - slot, acc)
        mass = stage_weights(slot, mass)
        stage_logs(1 - slot)
        return mass, acc

    def steps_from(base, count, state):
        for k in range(count):
            state = step(base + k, k % 2, state)
        return state

    z_ref[1] = jnp.full((tq, ks), NEG, F32)
    zc_ref[0] = jnp.full((tq, ks), NEG, F32)
    hl_ref[0] = jnp.zeros((tq, ks), BF16)
    w_ref[1] = jnp.zeros((tq, ks), BF16)
    state = (jnp.zeros((tq, 1), F32), jnp.zeros((tq, B_DIM), F32))
    assert tq == 2 * ks, "units 0 and 1 are exactly the q block's own key block"
    state = step(1, 1, step(0, 0, state, own_block=True), own_block=True)
    quads = (n_units + 1) // 4
    pairs = (n_units + 2 - 4 * quads) // 2
    state = lax.fori_loop(0, quads, lambda t, st: steps_from(2 + 4 * t, 4, st), state)
    _, acc = lax.fori_loop(0, pairs, lambda t, st: steps_from(2 + 4 * quads + 2 * t, 2, st), state)
    o_ref[...] = (acc * _silu(g_ref[...].astype(F32))).astype(o_ref.dtype)


def _attn_b(proj, batch, seq, tq, ks):
    m = proj.shape[0]
    nq = seq // tq
    q0 = _Cols.QKVB // B_DIM
    k0 = q0 + B_HEADS
    v0 = k0 + B_HEADS
    g0 = _Cols.GB // B_DIM
    return pl.pallas_call(
        functools.partial(_attn_b_kernel, tq=tq, ks=ks, scale2=LOG2E / math.sqrt(B_DIM)),
        grid=(batch, B_HEADS, nq),
        in_specs=[pl.BlockSpec((tq, B_DIM), lambda b, h, i: (b * nq + i, q0 + h)),
                  pl.BlockSpec((seq, B_DIM), lambda b, h, i: (b, k0 + h)),
                  pl.BlockSpec((seq, B_DIM), lambda b, h, i: (b, v0 + h)),
                  pl.BlockSpec((tq, B_DIM), lambda b, h, i: (b * nq + i, g0 + h))],
        out_specs=pl.BlockSpec((tq, B_DIM), lambda b, h, i: (b * nq + i, h)),
        out_shape=jax.ShapeDtypeStruct((m, B_HEADS * B_DIM), BF16),
        scratch_shapes=[pltpu.VMEM((2, tq, ks), F32), pltpu.VMEM((2, tq, ks), F32),
                        pltpu.VMEM((2, tq, ks), BF16), pltpu.VMEM((2, tq, ks), BF16)],
        compiler_params=_cparams(("parallel", "parallel", "arbitrary")),
        name="attn_b",
    )(proj, proj, proj, proj)


def _mem_prep_kernel(mem_ref, g_ref, w_ref, gk_ref, o_ref):
    j = pl.program_id(0)
    x = mem_ref[...]
    xn = (x * lax.rsqrt(jnp.mean(x * x, axis=-1, keepdims=True) + EPS) * g_ref[...]).astype(BF16)
    y = jnp.dot(xn, w_ref[...].astype(BF16), preferred_element_type=F32)
    yk = y * lax.rsqrt(jnp.mean(y * y, axis=-1, keepdims=True) + EPS) * gk_ref[...]
    o_ref[...] = jnp.where(j < M_HEADS, yk, y).astype(o_ref.dtype)


def _mem_prep(mem2d, g_row, w_mkv, layer, gk_row):
    mm, d = mem2d.shape
    n = 2 * M_HEADS * M_DIM
    return pl.pallas_call(
        _mem_prep_kernel,
        grid=(2 * M_HEADS,),
        in_specs=[pl.BlockSpec((mm, d), lambda j: (0, 0)),
                  pl.BlockSpec((1, d), lambda j: (0, 0)),
                  pl.BlockSpec((None, d, M_DIM), lambda j: (layer, 0, j)),
                  pl.BlockSpec((1, M_DIM), lambda j: (0, 0))],
        out_specs=pl.BlockSpec((mm, M_DIM), lambda j: (0, j)),
        out_shape=jax.ShapeDtypeStruct((mm, n), BF16),
        compiler_params=_cparams(("arbitrary",)),
        name="mem_prep",
    )(mem2d, g_row, w_mkv, gk_row)


def _attn_m_kernel(q_ref, k_ref, v_ref, g_ref, gq_ref, o_ref, *, scale):
    q = q_ref[...].astype(F32)
    qn = q * (lax.rsqrt(jnp.mean(q * q, axis=-1, keepdims=True) + EPS) * scale) * gq_ref[...]
    s = lax.dot_general(qn.astype(BF16), k_ref[...], _NT, preferred_element_type=F32)
    p = jnp.exp(s - jnp.max(s, axis=-1, keepdims=True))
    l = jnp.sum(p, axis=-1, keepdims=True)
    o = jnp.dot(p.astype(BF16), v_ref[...], preferred_element_type=F32) / l
    o_ref[...] = (o * _silu(g_ref[...].astype(F32))).astype(o_ref.dtype)


def _attn_m(proj, kv_m, gq_row, seq, mem_len, tm):
    m = proj.shape[0]
    nb = seq // tm
    q0 = _Cols.QM // M_DIM
    g0 = _Cols.GM // M_DIM
    assert mem_len == M_DIM, "memory block spec assumes MEM_LEN == M_DIM rows per batch"
    return pl.pallas_call(
        functools.partial(_attn_m_kernel, scale=1.0 / math.sqrt(M_DIM)),
        grid=(m // tm, M_HEADS),
        in_specs=[pl.BlockSpec((tm, M_DIM), lambda i, h: (i, q0 + h)),
                  pl.BlockSpec((mem_len, M_DIM), lambda i, h: (i // nb, h)),
                  pl.BlockSpec((mem_len, M_DIM), lambda i, h: (i // nb, M_HEADS + h)),
                  pl.BlockSpec((tm, M_DIM), lambda i, h: (i, g0 + h)),
                  pl.BlockSpec((1, M_DIM), lambda i, h: (0, 0))],
        out_specs=pl.BlockSpec((tm, M_DIM), lambda i, h: (i, h)),
        out_shape=jax.ShapeDtypeStruct((m, M_HEADS * M_DIM), BF16),
        compiler_params=_cparams(("parallel", "arbitrary")),
        name="attn_m",
    )(proj, kv_m, kv_m, proj, gq_row)


def _mix_kernel(ua_ref, ub_ref, um_ref, wa_ref, wb_ref, wm_ref, ra_ref, rb_ref, rm_ref, o_ref,
                wa_bf, wb_bf, wm_bf):
    @pl.when(pl.program_id(1) == 0)
    def _cast():
        wa_bf[...] = wa_ref[...].astype(BF16)
        wb_bf[...] = wb_ref[...].astype(BF16)
        wm_bf[...] = wm_ref[...].astype(BF16)

    ya = jnp.dot(ua_ref[...], wa_bf[...], preferred_element_type=F32)
    yb = jnp.dot(ub_ref[...], wb_bf[...], preferred_element_type=F32)
    ym = jnp.dot(um_ref[...], wm_bf[...], preferred_element_type=F32)
    mixed = (_sigmoid(ra_ref[...].astype(F32)) * ya + _sigmoid(rb_ref[...].astype(F32)) * yb
             + _sigmoid(rm_ref[...].astype(F32)) * ym)
    o_ref[...] = mixed.astype(o_ref.dtype)


def _mix(u_a, u_b, u_m, w_pa, w_pb, w_pm, layer, proj, tm, tn):
    m = u_a.shape[0]
    r0 = _Cols.MERGE // tn
    rstep = D_MODEL // tn

    def rspec(branch):
        return pl.BlockSpec((tm, tn), lambda j, i: (i, r0 + branch * rstep + j))

    def wspec(w):
        return pl.BlockSpec((None, w.shape[1], tn), lambda j, i: (layer, 0, j))

    return pl.pallas_call(
        _mix_kernel,
        grid=(D_MODEL // tn, m // tm),
        in_specs=[pl.BlockSpec((tm, u_a.shape[1]), lambda j, i: (i, 0)),
                  pl.BlockSpec((tm, u_b.shape[1]), lambda j, i: (i, 0)),
                  pl.BlockSpec((tm, u_m.shape[1]), lambda j, i: (i, 0)),
                  wspec(w_pa), wspec(w_pb), wspec(w_pm),
                  rspec(0), rspec(1), rspec(2)],
        out_specs=pl.BlockSpec((tm, tn), lambda j, i: (i, j)),
        out_shape=jax.ShapeDtypeStruct((m, D_MODEL), BF16),
        scratch_shapes=[pltpu.VMEM((w_pa.shape[1], tn), BF16), pltpu.VMEM((w_pb.shape[1], tn), BF16),
                        pltpu.VMEM((w_pm.shape[1], tn), BF16)],
        compiler_params=_cparams(("arbitrary", "arbitrary")),
        name="mix",
    )(u_a, u_b, u_m, w_pa, w_pb, w_pm, proj, proj, proj)


def _out_kernel(a_ref, w_ref, x_ref, o_ref, w_bf):
    @pl.when(pl.program_id(1) == 0)
    def _cast():
        w_bf[...] = w_ref[...].astype(BF16)

    o_ref[...] = x_ref[...] + jnp.dot(a_ref[...], w_bf[...], preferred_element_type=F32)


def _out_proj(mixed, w_out, layer, x, tm, tn):
    m, k = mixed.shape
    n = w_out.shape[2]
    return pl.pallas_call(
        _out_kernel,
        grid=(n // tn, m // tm),
        in_specs=[pl.BlockSpec((tm, k), lambda j, i: (i, 0)),
                  pl.BlockSpec((None, k, tn), lambda j, i: (layer, 0, j)),
                  pl.BlockSpec((tm, tn), lambda j, i: (i, j))],
        out_specs=pl.BlockSpec((tm, tn), lambda j, i: (i, j)),
        out_shape=jax.ShapeDtypeStruct((m, n), F32),
        scratch_shapes=[pltpu.VMEM((k, tn), BF16)],
        compiler_params=_cparams(("arbitrary", "arbitrary")),
        name="out_proj",
    )(mixed, w_out, x)


def _rotate_half_cols(w):
    half = A_ROPE // 2
    return jnp.concatenate([-w[..., half:], w[..., :half]], axis=-1)


def _rope_gain(g_rope):
    half = A_ROPE // 2
    return jnp.concatenate([g_rope, g_rope[half:], g_rope[:half]])


def _prep_w_uq(w_uq):
    w = w_uq.reshape(Q_LORA, A_HEADS, A_QK)
    rope = w[..., A_NOPE:]
    w = jnp.concatenate([w[..., :A_NOPE], rope, _rotate_half_cols(rope)], axis=-1)
    return w.reshape(Q_LORA, A_HEADS * A_PAD).astype(BF16)


def _head_gain(g):
    return jnp.concatenate([g[:A_NOPE], _rope_gain(g[A_NOPE:])])[None, :]


def _tiles(m, seq):
    return dict(
        row=min(512, m),
        mm_m=min(1024, m),
        mix_n=512,
        attn=min(512, seq // 2),
        b_keys=min(256, seq // 4),
    )


def kernel(x, mem, positions, g_pre, w_in, g_q_lat, w_uq, g_kv_lat, w_ukv, g_qn_a, g_kn_a,
           w_pa, w_pb, g_mem, w_mkv, g_qn_m, g_kn_m, w_pm, w_out):
    batch, seq, d = x.shape
    mem_len = mem.shape[1]
    depth = w_in.shape[0]
    m = batch * seq
    t = _tiles(m, seq)

    half = A_ROPE // 2
    freqs = ROPE_THETA ** (-jnp.arange(half, dtype=F32) / half)
    freq_row = jnp.tile(freqs, LANE // half)[None, :]
    pos_b = jnp.broadcast_to(positions.reshape(m, 1).astype(F32), (m, LANE))
    tbl = _rope_table(pos_b, freq_row, t["row"])

    xs = x.reshape(m, d)
    mem2d = mem.reshape(batch * mem_len, d)
    w_in_t = jnp.swapaxes(w_in, 1, 2)
    for l in range(depth):
        h = _norm(xs, g_pre[l][None, :], t["row"])
        proj = _in_proj(h, w_in_t, l, t["mm_m"])

        q_a = _q_prep(proj, g_q_lat[l][None, :], _prep_w_uq(w_uq[l]), _head_gain(g_qn_a[l]),
                      tbl, t["row"])
        k_a, v_a = _kv_prep(proj, g_kv_lat[l][None, :], w_ukv[l].astype(BF16),
                            _head_gain(g_kn_a[l]), tbl, t["row"])
        u_a = _attn_a(q_a, k_a, v_a, proj, batch, seq, t["attn"])
        u_b = _attn_b(proj, batch, seq, t["attn"], t["b_keys"])

        kv_m = _mem_prep(mem2d, g_mem[l][None, :], w_mkv, l, g_kn_m[l][None, :])
        u_m = _attn_m(proj, kv_m, g_qn_m[l][None, :], seq, mem_len, t["row"])

        mixed = _mix(u_a, u_b, u_m, w_pa, w_pb, w_pm, l, proj, t["mm_m"], t["mix_n"])
        xs = _out_proj(mixed, w_out, l, xs, t["mm_m"], t["mix_n"])
    return xs.reshape(batch, seq, d)
```

```python
import functools
import math

import jax
import jax.numpy as jnp
from jax import lax
from jax.experimental import pallas as pl
from jax.experimental.pallas import tpu as pltpu

F32 = jnp.float32
BF16 = jnp.bfloat16

EPS = 1e-6
LOG2E = 1.4426950408889634
NEG = -1e30
ROPE_THETA = 10000.0
CHUNK = 64

A_HEADS, A_NOPE, A_ROPE, A_V = 16, 128, 64, 128
A_QK = A_NOPE + A_ROPE
A_PAD = 256
Q_LORA, KV_LORA = 1024, 512
B_HEADS, B_DIM = 8, 128
M_HEADS, M_DIM = 4, 256
D_MODEL = 4096
N_BRANCH = 3

LANE = 128
VMEM_LIMIT = 56 * 1024 * 1024


class _Cols:
    CQ = 0
    CKV = CQ + Q_LORA
    GA = CKV + KV_LORA
    QKVB = GA + A_HEADS * A_V
    GB = QKVB + 3 * B_HEADS * B_DIM
    QM = GB + B_HEADS * B_DIM
    GM = QM + M_HEADS * M_DIM
    MERGE = GM + M_HEADS * M_DIM
    KR = MERGE + N_BRANCH * D_MODEL
    USED = KR + LANE
    TOTAL = 22528


def _cparams(sem):
    return pltpu.CompilerParams(dimension_semantics=sem, vmem_limit_bytes=VMEM_LIMIT)


def _sigmoid(v):
    return 1.0 / (1.0 + jnp.exp(-v))


def _silu(v):
    return v * _sigmoid(v)


_NT = (((1,), (1,)), ((), ()))


def _rope_table_kernel(pos_ref, freq_ref, o_ref):
    ang = pos_ref[...] * freq_ref[...]
    lane = lax.broadcasted_iota(jnp.int32, ang.shape, 1)
    o_ref[...] = jnp.where(lane < A_ROPE, jnp.cos(ang), jnp.sin(ang))


def _rope_table(pos_b, freq_row, tm):
    m = pos_b.shape[0]
    return pl.pallas_call(
        _rope_table_kernel,
        grid=(m // tm,),
        in_specs=[pl.BlockSpec((tm, LANE), lambda i: (i, 0)),
                  pl.BlockSpec((1, LANE), lambda i: (0, 0))],
        out_specs=pl.BlockSpec((tm, LANE), lambda i: (i, 0)),
        out_shape=jax.ShapeDtypeStruct((m, LANE), F32),
        compiler_params=_cparams(("parallel",)),
        name="rope_table",
    )(pos_b, freq_row)


def _norm_kernel(x_ref, g_ref, o_ref):
    x = x_ref[...]
    y = x * lax.rsqrt(jnp.mean(x * x, axis=-1, keepdims=True) + EPS)
    o_ref[...] = (y * g_ref[...]).astype(o_ref.dtype)


def _norm(x, g_row, tm):
    m, d = x.shape
    return pl.pallas_call(
        _norm_kernel,
        grid=(m // tm,),
        in_specs=[pl.BlockSpec((tm, d), lambda i: (i, 0)),
                  pl.BlockSpec((1, d), lambda i: (0, 0))],
        out_specs=pl.BlockSpec((tm, d), lambda i: (i, 0)),
        out_shape=jax.ShapeDtypeStruct((m, d), BF16),
        compiler_params=_cparams(("parallel",)),
        name="norm",
    )(x, g_row)


IN_TN = 1024
KR_SRC = Q_LORA + KV_LORA


def _in_proj_kernel(a_ref, w_ref, wnext_ref, wkr_ref, o_ref, wp_ref):
    j = pl.program_id(0)
    last = pl.num_programs(0) - 1
    cut = KR_SRC % IN_TN
    tail = _Cols.KR % IN_TN
    half = A_ROPE // 2
    assert KR_SRC // IN_TN == 1 and _Cols.KR // IN_TN == _Cols.TOTAL // IN_TN - 1

    @pl.when(pl.program_id(1) == 0)
    def _assemble():
        @pl.when(j == 0)
        def _():
            wp_ref[...] = w_ref[...].astype(BF16)

        @pl.when(j == 1)
        def _():
            wp_ref[:cut, :] = w_ref[:cut, :].astype(BF16)
            wp_ref[cut:IN_TN - A_ROPE, :] = w_ref[cut + A_ROPE:, :].astype(BF16)
            wp_ref[IN_TN - A_ROPE:, :] = wnext_ref[...].astype(BF16)

        @pl.when(jnp.logical_and(j > 1, j < last))
        def _():
            wp_ref[:IN_TN - A_ROPE, :] = w_ref[A_ROPE:, :].astype(BF16)
            wp_ref[IN_TN - A_ROPE:, :] = wnext_ref[...].astype(BF16)

        @pl.when(j == last)
        def _():
            kr = wkr_ref[...]
            wp_ref[:tail, :] = w_ref[A_ROPE:A_ROPE + tail, :].astype(BF16)
            wp_ref[tail:tail + A_ROPE, :] = kr.astype(BF16)
            wp_ref[tail + A_ROPE:tail + A_ROPE + half, :] = (-kr[half:, :]).astype(BF16)
            wp_ref[tail + A_ROPE + half:tail + 2 * A_ROPE, :] = kr[:half, :].astype(BF16)
            wp_ref[tail + 2 * A_ROPE:, :] = jnp.zeros((IN_TN - tail - 2 * A_ROPE, wp_ref.shape[1]),
                                                      BF16)

    o_ref[...] = lax.dot_general(a_ref[...], wp_ref[...], _NT,
                                 preferred_element_type=F32).astype(o_ref.dtype)


def _in_proj(h, w_in_t, layer, tm):
    m, k = h.shape
    n_src = w_in_t.shape[1]
    rope_blocks = IN_TN // A_ROPE
    last_rope_block = n_src // A_ROPE - 1
    return pl.pallas_call(
        _in_proj_kernel,
        grid=(_Cols.TOTAL // IN_TN, m // tm),
        in_specs=[pl.BlockSpec((tm, k), lambda j, i: (i, 0)),
                  pl.BlockSpec((None, IN_TN, k), lambda j, i: (layer, j, 0),
                               pipeline_mode=pl.Buffered(1)),
                  pl.BlockSpec((None, A_ROPE, k),
                               lambda j, i: (layer, jnp.minimum((j + 1) * rope_blocks,
                                                                last_rope_block), 0)),
                  pl.BlockSpec((None, A_ROPE, k), lambda j, i: (layer, KR_SRC // A_ROPE, 0))],
        out_specs=pl.BlockSpec((tm, IN_TN), lambda j, i: (i, j)),
        out_shape=jax.ShapeDtypeStruct((m, _Cols.TOTAL), BF16),
        scratch_shapes=[pltpu.VMEM((IN_TN, k), BF16)],
        compiler_params=_cparams(("arbitrary", "arbitrary")),
        name="in_proj",
    )(h, w_in_t, w_in_t, w_in_t)


def _q_prep_kernel(c_ref, gl_ref, w_ref, gh_ref, tbl_ref, o_ref, *, scale):
    c = c_ref[...].astype(F32)
    cn = c * lax.rsqrt(jnp.mean(c * c, axis=-1, keepdims=True) + EPS) * gl_ref[...]
    cn = cn.astype(BF16)
    tbl = tbl_ref[...]
    g1 = gh_ref[:, :A_NOPE]
    g2t = gh_ref[:, A_NOPE:] * tbl
    lane = lax.broadcasted_iota(jnp.int32, tbl.shape, 1)
    for h in range(A_HEADS):
        acc = jnp.dot(cn, w_ref[:, h * A_PAD:(h + 1) * A_PAD], preferred_element_type=F32)
        v1 = acc[:, :A_NOPE]
        v2 = acc[:, A_NOPE:]
        ss = (jnp.sum(v1 * v1, axis=-1, keepdims=True)
              + jnp.sum(jnp.where(lane < A_ROPE, v2 * v2, 0.0), axis=-1, keepdims=True))
        rs = lax.rsqrt(ss * (1.0 / A_QK) + EPS) * scale
        o_ref[:, h * A_PAD:h * A_PAD + A_NOPE] = (v1 * g1 * rs).astype(o_ref.dtype)
        o_ref[:, h * A_PAD + A_NOPE:(h + 1) * A_PAD] = (v2 * g2t * rs).astype(o_ref.dtype)


def _q_prep(proj, gl_row, w_uq, gh_row, tbl, tm):
    m = proj.shape[0]
    n = A_HEADS * A_PAD
    return pl.pallas_call(
        functools.partial(_q_prep_kernel, scale=LOG2E / math.sqrt(A_QK)),
        grid=(m // tm,),
        in_specs=[pl.BlockSpec((tm, Q_LORA), lambda i: (i, _Cols.CQ // Q_LORA)),
                  pl.BlockSpec((1, Q_LORA), lambda i: (0, 0)),
                  pl.BlockSpec((Q_LORA, n), lambda i: (0, 0)),
                  pl.BlockSpec((1, A_PAD), lambda i: (0, 0)),
                  pl.BlockSpec((tm, LANE), lambda i: (i, 0))],
        out_specs=pl.BlockSpec((tm, n), lambda i: (i, 0)),
        out_shape=jax.ShapeDtypeStruct((m, n), BF16),
        compiler_params=_cparams(("parallel",)),
        name="q_prep",
    )(proj, gl_row, w_uq, gh_row, tbl)


def _kv_prep_kernel(c_ref, kr_ref, gl_ref, w_ref, gh_ref, tbl_ref, kt_ref, v_ref):
    c = c_ref[...].astype(F32)
    cn = c * lax.rsqrt(jnp.mean(c * c, axis=-1, keepdims=True) + EPS) * gl_ref[...]
    cn = cn.astype(BF16)
    kr = kr_ref[...].astype(F32)
    lane = lax.broadcasted_iota(jnp.int32, kr.shape, 1)
    ss_r = jnp.sum(jnp.where(lane < A_ROPE, kr * kr, 0.0), axis=-1, keepdims=True)
    ab = kr * gh_ref[:, A_NOPE:] * tbl_ref[...]
    kk = ab + pltpu.roll(ab, A_ROPE, axis=1)
    g1 = gh_ref[:, :A_NOPE]
    ones = jnp.ones((kr.shape[0], A_PAD - A_V), v_ref.dtype)
    for h in range(A_HEADS):
        acc = jnp.dot(cn, w_ref[:, h * A_PAD:(h + 1) * A_PAD], preferred_element_type=F32)
        kn = acc[:, :A_NOPE]
        ss = jnp.sum(kn * kn, axis=-1, keepdims=True) + ss_r
        rs = lax.rsqrt(ss * (1.0 / A_QK) + EPS)
        kt_ref[h * A_PAD:h * A_PAD + A_NOPE, :] = (kn * g1 * rs).T.astype(kt_ref.dtype)
        kt_ref[h * A_PAD + A_NOPE:(h + 1) * A_PAD, :] = (kk * rs).T.astype(kt_ref.dtype)
        v_ref[:, h * A_PAD:h * A_PAD + A_V] = acc[:, A_NOPE:].astype(v_ref.dtype)
        v_ref[:, h * A_PAD + A_V:(h + 1) * A_PAD] = ones


def _kv_prep(proj, gl_row, w_ukv, gh_row, tbl, tm):
    m = proj.shape[0]
    nk = A_HEADS * A_PAD
    nv = A_HEADS * A_PAD
    return pl.pallas_call(
        _kv_prep_kernel,
        grid=(m // tm,),
        in_specs=[pl.BlockSpec((tm, KV_LORA), lambda i: (i, _Cols.CKV // KV_LORA)),
                  pl.BlockSpec((tm, LANE), lambda i: (i, _Cols.KR // LANE)),
                  pl.BlockSpec((1, KV_LORA), lambda i: (0, 0)),
                  pl.BlockSpec((KV_LORA, nk), lambda i: (0, 0)),
                  pl.BlockSpec((1, A_PAD), lambda i: (0, 0)),
                  pl.BlockSpec((tm, LANE), lambda i: (i, 0))],
        out_specs=[pl.BlockSpec((nk, tm), lambda i: (0, i)),
                   pl.BlockSpec((tm, nv), lambda i: (i, 0))],
        out_shape=[jax.ShapeDtypeStruct((nk, m), BF16),
                   jax.ShapeDtypeStruct((m, nv), BF16)],
        compiler_params=_cparams(("parallel",)),
        name="kv_prep",
    )(proj, proj, gl_row, w_ukv, gh_row, tbl)


def _attn_a_flat_kernel(q_ref, kt_ref, v_ref, g_ref, o_ref, s_ref, p_ref, bias_ref, oacc_ref,
                        *, tq, nq):
    n_units = nq * (nq + 1) // 2

    def advance(unit):
        qb, u = unit
        done = u == qb
        return jnp.where(done, qb + 1, qb), jnp.where(done, 0, u + 1)

    def key_block(unit):
        qb, u = unit
        return jnp.where(u == 0, qb, u - 1)

    def scores(unit, slot):
        qb, u = unit
        q = q_ref[pl.ds(pl.multiple_of(qb * tq, tq), tq), :]
        kt = kt_ref[:, pl.ds(pl.multiple_of(key_block(unit) * tq, tq), tq)]
        own = jnp.where(u == 0, 1, 0)
        s_ref[slot] = jnp.dot(q, kt, preferred_element_type=F32) + bias_ref[own]

    def p_times_v(unit, slot):
        v = v_ref[pl.ds(pl.multiple_of(key_block(unit) * tq, tq), tq), :]
        return jnp.dot(p_ref[slot], v, preferred_element_type=F32)

    def softmax(unit, slot, m, acc):
        s = s_ref[slot]
        m = jnp.where(unit[1] == 0, NEG, m)
        m_new = jnp.maximum(m, jnp.max(s, axis=-1, keepdims=True))
        p_ref[slot] = jnp.exp2(s - m_new).astype(BF16)
        return m_new, jnp.exp2(m - m_new) * acc

    def step(slot, state):
        m, pend, u1, u2, u3 = state
        scores(u1, slot)
        acc = pend + p_times_v(u3, slot)
        oacc_ref[u3[0]] = acc
        m, pend = softmax(u2, 1 - slot, m, acc)
        return m, pend, advance(u1), u1, u2

    row_chunk = lax.broadcasted_iota(jnp.int32, (tq, tq), 0) // CHUNK
    key_chunk = lax.broadcasted_iota(jnp.int32, (tq, tq), 1) // CHUNK
    bias_ref[0] = jnp.zeros((tq, tq), F32)
    bias_ref[1] = jnp.where(key_chunk <= row_chunk, 0.0, NEG)
    p_ref[1] = jnp.zeros((tq, tq), BF16)
    zero = jnp.int32(0)
    first = (zero, zero)
    scores(first, 0)
    state = (jnp.full((tq, 1), NEG, F32), jnp.zeros((tq, A_PAD), F32), advance(first), first, first)

    def steps(count, state):
        for k in range(count):
            state = step((1 + k) % 2, state)
        return state

    n_loop = n_units - 1
    state = lax.fori_loop(0, n_loop // 4, lambda _, st: steps(4, st), state)
    assert (n_loop // 4 * 4) % 2 == 0
    state = steps(n_loop % 4, state)
    m, pend, _, u2, u3 = state
    last_slot = (n_units - 1) % 2
    acc = pend + p_times_v(u3, 1 - last_slot)
    _, pend = softmax(u2, last_slot, m, acc)
    oacc_ref[u2[0]] = pend + p_times_v(u2, last_slot)
    for qb in range(nq):
        acc = oacc_ref[qb]
        gate = g_ref[qb * tq:(qb + 1) * tq, :].astype(F32)
        o = acc[:, :A_V] / acc[:, A_V:]
        o_ref[qb * tq:(qb + 1) * tq, :] = (o * _silu(gate)).astype(o_ref.dtype)


def _attn_a_flat(q_a, kt_a, v_a, proj, batch, seq, tq):
    m = q_a.shape[0]
    nq = seq // tq
    gate0 = _Cols.GA // A_V
    return pl.pallas_call(
        functools.partial(_attn_a_flat_kernel, tq=tq, nq=nq),
        grid=(batch, A_HEADS),
        in_specs=[pl.BlockSpec((seq, A_PAD), lambda b, h: (b, h)),
                  pl.BlockSpec((A_PAD, seq), lambda b, h: (h, b)),
                  pl.BlockSpec((seq, A_PAD), lambda b, h: (b, h)),
                  pl.BlockSpec((seq, A_V), lambda b, h: (b, gate0 + h))],
        out_specs=pl.BlockSpec((seq, A_V), lambda b, h: (b, h)),
        out_shape=jax.ShapeDtypeStruct((m, A_HEADS * A_V), BF16),
        scratch_shapes=[pltpu.VMEM((2, tq, tq), F32), pltpu.VMEM((2, tq, tq), BF16),
                        pltpu.VMEM((2, tq, tq), F32), pltpu.VMEM((nq, tq, A_PAD), F32)],
        compiler_params=_cparams(("parallel", "arbitrary")),
        name="attn_a",
    )(q_a, kt_a, v_a, proj)


def _attn_b_flat_kernel(q_ref, k_ref, v_ref, g_ref, o_ref,
                        z_ref, zc_ref, hl_ref, w_ref, bias_ref, oacc_ref, *, tq, ks, nq, scale2):
    per_block = tq // ks
    assert per_block == 2, "bias tables below cover the two own-block units"
    n_units = per_block * nq * (nq + 1) // 2
    r = lax.broadcasted_iota(jnp.int32, (ks, ks), 0)
    c = lax.broadcasted_iota(jnp.int32, (ks, ks), 1)
    tri = jnp.where(r >= c, 1.0, 0.0).astype(BF16)

    def advance(unit):
        qb, u = unit
        done = u == per_block * qb + per_block - 1
        return jnp.where(done, qb + 1, qb), jnp.where(done, 0, u + 1)

    def key_start(unit):
        qb, u = unit
        sub = jnp.clip(per_block * qb + per_block - 1 - u, 0, per_block * nq - 1)
        return pl.multiple_of(sub * ks, ks)

    def stage_scores(t, unit, slot):
        qb, u = unit
        q = q_ref[pl.ds(pl.multiple_of(jnp.minimum(qb, nq - 1) * tq, tq), tq), :]
        z = lax.dot_general(q, k_ref[pl.ds(key_start(unit), ks), :], _NT,
                            preferred_element_type=F32)
        table = jnp.where(t < n_units, jnp.where(u < per_block, u + 1, 0), 3)
        z_ref[slot] = z * scale2 + bias_ref[table]

    def stage_logs(slot):
        z = z_ref[slot]
        neg_abs = lax.bitcast_convert_type(
            lax.bitcast_convert_type(z, jnp.uint32) | jnp.uint32(0x80000000), F32)
        nlf = jnp.maximum(z, 0.0) + jnp.log(1.0 + jnp.exp2(neg_abs)) * LOG2E
        hl_ref[slot] = nlf.astype(BF16)
        zc_ref[slot] = z

    def stage_weights(unit, slot, mass):
        mass = jnp.where(unit[1] == 0, 0.0, mass)
        tail = jnp.dot(hl_ref[slot], tri, preferred_element_type=F32)
        w_ref[slot] = jnp.exp2(zc_ref[slot] - tail - mass).astype(BF16)
        return mass + tail[:, :1]

    def stage_pv(unit, slot, acc):
        acc = jnp.where(unit[1] == 0, 0.0, acc)
        v = v_ref[pl.ds(key_start(unit), ks), :]
        acc = acc + jnp.dot(w_ref[slot], v, preferred_element_type=F32)
        oacc_ref[jnp.minimum(unit[0], nq - 1)] = acc
        return acc

    def step(slot, state):
        t, mass, acc, u1, u2, u3, u4 = state
        stage_scores(t, u1, slot)
        acc = stage_pv(u4, 1 - slot, acc)
        mass = stage_weights(u3, slot, mass)
        stage_logs(1 - slot)
        return t + 1, mass, acc, advance(u1), u1, u2, u3

    def steps(count, state):
        for k in range(count):
            state = step(k % 2, state)
        return state

    d = (lax.broadcasted_iota(jnp.int32, (tq, ks), 0) - lax.broadcasted_iota(jnp.int32, (tq, ks), 1))
    bias_ref[0] = jnp.zeros((tq, ks), F32)
    bias_ref[1] = jnp.where(d > ks, 0.0, NEG)
    bias_ref[2] = jnp.where(d > 0, 0.0, NEG)
    bias_ref[3] = jnp.full((tq, ks), NEG, F32)
    z_ref[1] = jnp.full((tq, ks), NEG, F32)
    zc_ref[0] = jnp.full((tq, ks), NEG, F32)
    hl_ref[0] = jnp.zeros((tq, ks), BF16)
    w_ref[1] = jnp.zeros((tq, ks), BF16)
    zero = jnp.int32(0)
    first = (zero, zero)
    state = (zero, jnp.zeros((tq, 1), F32), jnp.zeros((tq, B_DIM), F32), first, first, first, first)
    n_steps = n_units + 3
    state = lax.fori_loop(0, n_steps // 4, lambda _, st: steps(4, st), state)
    steps(n_steps % 4, state)
    for qb in range(nq):
        gate = g_ref[qb * tq:(qb + 1) * tq, :].astype(F32)
        o_ref[qb * tq:(qb + 1) * tq, :] = (oacc_ref[qb] * _silu(gate)).astype(o_ref.dtype)


def _attn_b_flat(proj, batch, seq, tq, ks):
    m = proj.shape[0]
    nq = seq // tq
    q0 = _Cols.QKVB // B_DIM
    k0 = q0 + B_HEADS
    v0 = k0 + B_HEADS
    g0 = _Cols.GB // B_DIM
    return pl.pallas_call(
        functools.partial(_attn_b_flat_kernel, tq=tq, ks=ks, nq=nq,
                          scale2=LOG2E / math.sqrt(B_DIM)),
        grid=(batch, B_HEADS),
        in_specs=[pl.BlockSpec((seq, B_DIM), lambda b, h: (b, q0 + h)),
                  pl.BlockSpec((seq, B_DIM), lambda b, h: (b, k0 + h)),
                  pl.BlockSpec((seq, B_DIM), lambda b, h: (b, v0 + h)),
                  pl.BlockSpec((seq, B_DIM), lambda b, h: (b, g0 + h))],
        out_specs=pl.BlockSpec((seq, B_DIM), lambda b, h: (b, h)),
        out_shape=jax.ShapeDtypeStruct((m, B_HEADS * B_DIM), BF16),
        scratch_shapes=[pltpu.VMEM((2, tq, ks), F32), pltpu.VMEM((2, tq, ks), F32),
                        pltpu.VMEM((2, tq, ks), BF16), pltpu.VMEM((2, tq, ks), BF16),
                        pltpu.VMEM((4, tq, ks), F32), pltpu.VMEM((nq, tq, B_DIM), F32)],
        compiler_params=_cparams(("parallel", "arbitrary")),
        name="attn_b",
    )(proj, proj, proj, proj)


def _mem_prep_kernel(mem_ref, g_ref, w_ref, gk_ref, o_ref):
    j = pl.program_id(0)
    x = mem_ref[...]
    xn = (x * lax.rsqrt(jnp.mean(x * x, axis=-1, keepdims=True) + EPS) * g_ref[...]).astype(BF16)
    y = jnp.dot(xn, w_ref[...].astype(BF16), preferred_element_type=F32)
    yk = y * lax.rsqrt(jnp.mean(y * y, axis=-1, keepdims=True) + EPS) * gk_ref[...]
    o_ref[...] = jnp.where(j < M_HEADS, yk, y).astype(o_ref.dtype)


def _mem_prep(mem2d, g_row, w_mkv, layer, gk_row):
    mm, d = mem2d.shape
    n = 2 * M_HEADS * M_DIM
    return pl.pallas_call(
        _mem_prep_kernel,
        grid=(2 * M_HEADS,),
        in_specs=[pl.BlockSpec((mm, d), lambda j: (0, 0)),
                  pl.BlockSpec((1, d), lambda j: (0, 0)),
                  pl.BlockSpec((None, d, M_DIM), lambda j: (layer, 0, j)),
                  pl.BlockSpec((1, M_DIM), lambda j: (0, 0))],
        out_specs=pl.BlockSpec((mm, M_DIM), lambda j: (0, j)),
        out_shape=jax.ShapeDtypeStruct((mm, n), BF16),
        compiler_params=_cparams(("arbitrary",)),
        name="mem_prep",
    )(mem2d, g_row, w_mkv, gk_row)


def _attn_m_kernel(q_ref, k_ref, v_ref, g_ref, gq_ref, o_ref, *, scale):
    q = q_ref[...].astype(F32)
    qn = q * (lax.rsqrt(jnp.mean(q * q, axis=-1, keepdims=True) + EPS) * scale) * gq_ref[...]
    s = lax.dot_general(qn.astype(BF16), k_ref[...], _NT, preferred_element_type=F32)
    p = jnp.exp(s - jnp.max(s, axis=-1, keepdims=True))
    l = jnp.sum(p, axis=-1, keepdims=True)
    o = jnp.dot(p.astype(BF16), v_ref[...], preferred_element_type=F32) / l
    o_ref[...] = (o * _silu(g_ref[...].astype(F32))).astype(o_ref.dtype)


def _attn_m(proj, kv_m, gq_row, seq, mem_len, tm):
    m = proj.shape[0]
    nb = seq // tm
    q0 = _Cols.QM // M_DIM
    g0 = _Cols.GM // M_DIM
    assert mem_len == M_DIM, "memory block spec assumes MEM_LEN == M_DIM rows per batch"
    return pl.pallas_call(
        functools.partial(_attn_m_kernel, scale=1.0 / math.sqrt(M_DIM)),
        grid=(m // tm, M_HEADS),
        in_specs=[pl.BlockSpec((tm, M_DIM), lambda i, h: (i, q0 + h)),
                  pl.BlockSpec((mem_len, M_DIM), lambda i, h: (i // nb, h)),
                  pl.BlockSpec((mem_len, M_DIM), lambda i, h: (i // nb, M_HEADS + h)),
                  pl.BlockSpec((tm, M_DIM), lambda i, h: (i, g0 + h)),
                  pl.BlockSpec((1, M_DIM), lambda i, h: (0, 0))],
        out_specs=pl.BlockSpec((tm, M_DIM), lambda i, h: (i, h)),
        out_shape=jax.ShapeDtypeStruct((m, M_HEADS * M_DIM), BF16),
        compiler_params=_cparams(("parallel", "arbitrary")),
        name="attn_m",
    )(proj, kv_m, kv_m, proj, gq_row)


def _mix_kernel(ua_ref, ub_ref, um_ref, wa_ref, wb_ref, wm_ref, ra_ref, rb_ref, rm_ref, o_ref,
                wa_bf, wb_bf, wm_bf):
    @pl.when(pl.program_id(1) == 0)
    def _cast():
        wa_bf[...] = wa_ref[...].astype(BF16)
        wb_bf[...] = wb_ref[...].astype(BF16)
        wm_bf[...] = wm_ref[...].astype(BF16)

    ya = jnp.dot(ua_ref[...], wa_bf[...], preferred_element_type=F32)
    yb = jnp.dot(ub_ref[...], wb_bf[...], preferred_element_type=F32)
    ym = jnp.dot(um_ref[...], wm_bf[...], preferred_element_type=F32)
    mixed = (_sigmoid(ra_ref[...].astype(F32)) * ya + _sigmoid(rb_ref[...].astype(F32)) * yb
             + _sigmoid(rm_ref[...].astype(F32)) * ym)
    o_ref[...] = mixed.astype(o_ref.dtype)


def _mix(u_a, u_b, u_m, w_pa, w_pb, w_pm, layer, proj, tm, tn):
    m = u_a.shape[0]
    r0 = _Cols.MERGE // tn
    rstep = D_MODEL // tn

    def rspec(branch):
        return pl.BlockSpec((tm, tn), lambda j, i: (i, r0 + branch * rstep + j))

    def wspec(w):
        return pl.BlockSpec((None, w.shape[1], tn), lambda j, i: (layer, 0, j))

    return pl.pallas_call(
        _mix_kernel,
        grid=(D_MODEL // tn, m // tm),
        in_specs=[pl.BlockSpec((tm, u_a.shape[1]), lambda j, i: (i, 0)),
                  pl.BlockSpec((tm, u_b.shape[1]), lambda j, i: (i, 0)),
                  pl.BlockSpec((tm, u_m.shape[1]), lambda j, i: (i, 0)),
                  wspec(w_pa), wspec(w_pb), wspec(w_pm),
                  rspec(0), rspec(1), rspec(2)],
        out_specs=pl.BlockSpec((tm, tn), lambda j, i: (i, j)),
        out_shape=jax.ShapeDtypeStruct((m, D_MODEL), BF16),
        scratch_shapes=[pltpu.VMEM((w_pa.shape[1], tn), BF16), pltpu.VMEM((w_pb.shape[1], tn), BF16),
                        pltpu.VMEM((w_pm.shape[1], tn), BF16)],
        compiler_params=_cparams(("arbitrary", "arbitrary")),
        name="mix",
    )(u_a, u_b, u_m, w_pa, w_pb, w_pm, proj, proj, proj)


def _out_kernel(a_ref, w_ref, x_ref, o_ref, w_bf):
    @pl.when(pl.program_id(1) == 0)
    def _cast():
        w_bf[...] = w_ref[...].astype(BF16)

    o_ref[...] = x_ref[...] + jnp.dot(a_ref[...], w_bf[...], preferred_element_type=F32)


def _out_proj(mixed, w_out, layer, x, tm, tn):
    m, k = mixed.shape
    n = w_out.shape[2]
    return pl.pallas_call(
        _out_kernel,
        grid=(n // tn, m // tm),
        in_specs=[pl.BlockSpec((tm, k), lambda j, i: (i, 0)),
                  pl.BlockSpec((None, k, tn), lambda j, i: (layer, 0, j)),
                  pl.BlockSpec((tm, tn), lambda j, i: (i, j))],
        out_specs=pl.BlockSpec((tm, tn), lambda j, i: (i, j)),
        out_shape=jax.ShapeDtypeStruct((m, n), F32),
        scratch_shapes=[pltpu.VMEM((k, tn), BF16)],
        compiler_params=_cparams(("arbitrary", "arbitrary")),
        name="out_proj",
    )(mixed, w_out, x)


def _rotate_half_cols(w):
    half = A_ROPE // 2
    return jnp.concatenate([-w[..., half:], w[..., :half]], axis=-1)


def _rope_gain(g_rope):
    half = A_ROPE // 2
    return jnp.concatenate([g_rope, g_rope[half:], g_rope[:half]])


def _prep_w_uq(w_uq):
    w = w_uq.reshape(Q_LORA, A_HEADS, A_QK)
    rope = w[..., A_NOPE:]
    w = jnp.concatenate([w[..., :A_NOPE], rope, _rotate_half_cols(rope)], axis=-1)
    return w.reshape(Q_LORA, A_HEADS * A_PAD).astype(BF16)


def _head_gain(g):
    return jnp.concatenate([g[:A_NOPE], _rope_gain(g[A_NOPE:])])[None, :]


def _tiles(m, seq):
    return dict(
        row=min(512, m),
        mm_m=min(1024, m),
        mix_n=512,
        attn=min(512, seq // 2),
        b_keys=min(256, seq // 4),
    )


def kernel(x, mem, positions, g_pre, w_in, g_q_lat, w_uq, g_kv_lat, w_ukv, g_qn_a, g_kn_a,
           w_pa, w_pb, g_mem, w_mkv, g_qn_m, g_kn_m, w_pm, w_out):
    batch, seq, d = x.shape
    mem_len = mem.shape[1]
    depth = w_in.shape[0]
    m = batch * seq
    t = _tiles(m, seq)

    half = A_ROPE // 2
    freqs = ROPE_THETA ** (-jnp.arange(half, dtype=F32) / half)
    freq_row = jnp.tile(freqs, LANE // half)[None, :]
    pos_b = jnp.broadcast_to(positions.reshape(m, 1).astype(F32), (m, LANE))
    tbl = _rope_table(pos_b, freq_row, t["row"])

    xs = x.reshape(m, d)
    mem2d = mem.reshape(batch * mem_len, d)
    w_in_t = jnp.swapaxes(w_in, 1, 2)
    for l in range(depth):
        h = _norm(xs, g_pre[l][None, :], t["row"])
        proj = _in_proj(h, w_in_t, l, t["mm_m"])

        q_a = _q_prep(proj, g_q_lat[l][None, :], _prep_w_uq(w_uq[l]), _head_gain(g_qn_a[l]),
                      tbl, t["row"])
        k_a, v_a = _kv_prep(proj, g_kv_lat[l][None, :], w_ukv[l].astype(BF16),
                            _head_gain(g_kn_a[l]), tbl, t["row"])
        u_a = _attn_a_flat(q_a, k_a, v_a, proj, batch, seq, t["attn"])
        u_b = _attn_b_flat(proj, batch, seq, t["attn"], t["b_keys"])

        kv_m = _mem_prep(mem2d, g_mem[l][None, :], w_mkv, l, g_kn_m[l][None, :])
        u_m = _attn_m(proj, kv_m, g_qn_m[l][None, :], seq, mem_len, t["row"])

        mixed = _mix(u_a, u_b, u_m, w_pa, w_pb, w_pm, l, proj, t["mm_m"], t["mix_n"])
        xs = _out_proj(mixed, w_out, l, xs, t["mm_m"], t["mix_n"])
    return xs.reshape(batch, seq, d)
```

```python
import functools
import math

import jax
import jax.numpy as jnp
from jax import lax
from jax.experimental import pallas as pl
from jax.experimental.pallas import tpu as pltpu

F32 = jnp.float32
BF16 = jnp.bfloat16

EPS = 1e-6
LOG2E = 1.4426950408889634
NEG = -1e30
STEPS_PER_TRIP = 16
ROPE_THETA = 10000.0
CHUNK = 64

A_HEADS, A_NOPE, A_ROPE, A_V = 16, 128, 64, 128
A_QK = A_NOPE + A_ROPE
A_PAD = 256
Q_LORA, KV_LORA = 1024, 512
B_HEADS, B_DIM = 8, 128
M_HEADS, M_DIM = 4, 256
D_MODEL = 4096
N_BRANCH = 3

LANE = 128
VMEM_LIMIT = 56 * 1024 * 1024


class _Cols:
    CQ = 0
    CKV = CQ + Q_LORA
    GA = CKV + KV_LORA
    QKVB = GA + A_HEADS * A_V
    GB = QKVB + 3 * B_HEADS * B_DIM
    QM = GB + B_HEADS * B_DIM
    GM = QM + M_HEADS * M_DIM
    MERGE = GM + M_HEADS * M_DIM
    KR = MERGE + N_BRANCH * D_MODEL
    USED = KR + LANE
    TOTAL = 22528


def _cparams(sem):
    return pltpu.CompilerParams(dimension_semantics=sem, vmem_limit_bytes=VMEM_LIMIT)


def _sigmoid(v):
    return 1.0 / (1.0 + jnp.exp(-v))


def _silu(v):
    return v * _sigmoid(v)


_NT = (((1,), (1,)), ((), ()))


def _rope_table_kernel(pos_ref, freq_ref, o_ref):
    ang = pos_ref[...] * freq_ref[...]
    lane = lax.broadcasted_iota(jnp.int32, ang.shape, 1)
    o_ref[...] = jnp.where(lane < A_ROPE, jnp.cos(ang), jnp.sin(ang))


def _rope_table(pos_b, freq_row, tm):
    m = pos_b.shape[0]
    return pl.pallas_call(
        _rope_table_kernel,
        grid=(m // tm,),
        in_specs=[pl.BlockSpec((tm, LANE), lambda i: (i, 0)),
                  pl.BlockSpec((1, LANE), lambda i: (0, 0))],
        out_specs=pl.BlockSpec((tm, LANE), lambda i: (i, 0)),
        out_shape=jax.ShapeDtypeStruct((m, LANE), F32),
        compiler_params=_cparams(("parallel",)),
        name="rope_table",
    )(pos_b, freq_row)


def _norm_kernel(x_ref, g_ref, o_ref):
    x = x_ref[...]
    y = x * lax.rsqrt(jnp.mean(x * x, axis=-1, keepdims=True) + EPS)
    o_ref[...] = (y * g_ref[...]).astype(o_ref.dtype)


def _norm(x, g_row, tm):
    m, d = x.shape
    return pl.pallas_call(
        _norm_kernel,
        grid=(m // tm,),
        in_specs=[pl.BlockSpec((tm, d), lambda i: (i, 0)),
                  pl.BlockSpec((1, d), lambda i: (0, 0))],
        out_specs=pl.BlockSpec((tm, d), lambda i: (i, 0)),
        out_shape=jax.ShapeDtypeStruct((m, d), BF16),
        compiler_params=_cparams(("parallel",)),
        name="norm",
    )(x, g_row)


IN_TN = 1024
KR_SRC = Q_LORA + KV_LORA


def _in_proj_kernel(a_ref, w_ref, wnext_ref, wkr_ref, o_ref, wp_ref):
    j = pl.program_id(0)
    last = pl.num_programs(0) - 1
    cut = KR_SRC % IN_TN
    tail = _Cols.KR % IN_TN
    half = A_ROPE // 2
    assert KR_SRC // IN_TN == 1 and _Cols.KR // IN_TN == _Cols.TOTAL // IN_TN - 1

    @pl.when(pl.program_id(1) == 0)
    def _assemble():
        @pl.when(j == 0)
        def _():
            wp_ref[...] = w_ref[...].astype(BF16)

        @pl.when(j == 1)
        def _():
            wp_ref[:cut, :] = w_ref[:cut, :].astype(BF16)
            wp_ref[cut:IN_TN - A_ROPE, :] = w_ref[cut + A_ROPE:, :].astype(BF16)
            wp_ref[IN_TN - A_ROPE:, :] = wnext_ref[...].astype(BF16)

        @pl.when(jnp.logical_and(j > 1, j < last))
        def _():
            wp_ref[:IN_TN - A_ROPE, :] = w_ref[A_ROPE:, :].astype(BF16)
            wp_ref[IN_TN - A_ROPE:, :] = wnext_ref[...].astype(BF16)

        @pl.when(j == last)
        def _():
            kr = wkr_ref[...]
            wp_ref[:tail, :] = w_ref[A_ROPE:A_ROPE + tail, :].astype(BF16)
            wp_ref[tail:tail + A_ROPE, :] = kr.astype(BF16)
            wp_ref[tail + A_ROPE:tail + A_ROPE + half, :] = (-kr[half:, :]).astype(BF16)
            wp_ref[tail + A_ROPE + half:tail + 2 * A_ROPE, :] = kr[:half, :].astype(BF16)
            wp_ref[tail + 2 * A_ROPE:, :] = jnp.zeros((IN_TN - tail - 2 * A_ROPE, wp_ref.shape[1]),
                                                      BF16)

    o_ref[...] = lax.dot_general(a_ref[...], wp_ref[...], _NT,
                                 preferred_element_type=F32).astype(o_ref.dtype)


def _in_proj(h, w_in_t, layer, tm):
    m, k = h.shape
    n_src = w_in_t.shape[1]
    rope_blocks = IN_TN // A_ROPE
    last_rope_block = n_src // A_ROPE - 1
    return pl.pallas_call(
        _in_proj_kernel,
        grid=(_Cols.TOTAL // IN_TN, m // tm),
        in_specs=[pl.BlockSpec((tm, k), lambda j, i: (i, 0)),
                  pl.BlockSpec((None, IN_TN, k), lambda j, i: (layer, j, 0),
                               pipeline_mode=pl.Buffered(1)),
                  pl.BlockSpec((None, A_ROPE, k),
                               lambda j, i: (layer, jnp.minimum((j + 1) * rope_blocks,
                                                                last_rope_block), 0)),
                  pl.BlockSpec((None, A_ROPE, k), lambda j, i: (layer, KR_SRC // A_ROPE, 0))],
        out_specs=pl.BlockSpec((tm, IN_TN), lambda j, i: (i, j)),
        out_shape=jax.ShapeDtypeStruct((m, _Cols.TOTAL), BF16),
        scratch_shapes=[pltpu.VMEM((IN_TN, k), BF16)],
        compiler_params=_cparams(("arbitrary", "arbitrary")),
        name="in_proj",
    )(h, w_in_t, w_in_t, w_in_t)


def _q_prep_kernel(c_ref, gl_ref, w_ref, gh_ref, tbl_ref, o_ref, *, scale):
    c = c_ref[...].astype(F32)
    cn = c * lax.rsqrt(jnp.mean(c * c, axis=-1, keepdims=True) + EPS) * gl_ref[...]
    cn = cn.astype(BF16)
    tbl = tbl_ref[...]
    g1 = gh_ref[:, :A_NOPE]
    g2t = gh_ref[:, A_NOPE:] * tbl
    lane = lax.broadcasted_iota(jnp.int32, tbl.shape, 1)
    for h in range(A_HEADS):
        acc = jnp.dot(cn, w_ref[:, h * A_PAD:(h + 1) * A_PAD], preferred_element_type=F32)
        v1 = acc[:, :A_NOPE]
        v2 = acc[:, A_NOPE:]
        ss = (jnp.sum(v1 * v1, axis=-1, keepdims=True)
              + jnp.sum(jnp.where(lane < A_ROPE, v2 * v2, 0.0), axis=-1, keepdims=True))
        rs = lax.rsqrt(ss * (1.0 / A_QK) + EPS) * scale
        o_ref[:, h * A_PAD:h * A_PAD + A_NOPE] = (v1 * g1 * rs).astype(o_ref.dtype)
        o_ref[:, h * A_PAD + A_NOPE:(h + 1) * A_PAD] = (v2 * g2t * rs).astype(o_ref.dtype)


def _q_prep(proj, gl_row, w_uq, gh_row, tbl, tm):
    m = proj.shape[0]
    n = A_HEADS * A_PAD
    return pl.pallas_call(
        functools.partial(_q_prep_kernel, scale=LOG2E / math.sqrt(A_QK)),
        grid=(m // tm,),
        in_specs=[pl.BlockSpec((tm, Q_LORA), lambda i: (i, _Cols.CQ // Q_LORA)),
                  pl.BlockSpec((1, Q_LORA), lambda i: (0, 0)),
                  pl.BlockSpec((Q_LORA, n), lambda i: (0, 0)),
                  pl.BlockSpec((1, A_PAD), lambda i: (0, 0)),
                  pl.BlockSpec((tm, LANE), lambda i: (i, 0))],
        out_specs=pl.BlockSpec((tm, n), lambda i: (i, 0)),
        out_shape=jax.ShapeDtypeStruct((m, n), BF16),
        compiler_params=_cparams(("parallel",)),
        name="q_prep",
    )(proj, gl_row, w_uq, gh_row, tbl)


def _kv_prep_kernel(c_ref, kr_ref, gl_ref, w_ref, gh_ref, tbl_ref, kt_ref, v_ref):
    c = c_ref[...].astype(F32)
    cn = c * lax.rsqrt(jnp.mean(c * c, axis=-1, keepdims=True) + EPS) * gl_ref[...]
    cn = cn.astype(BF16)
    kr = kr_ref[...].astype(F32)
    lane = lax.broadcasted_iota(jnp.int32, kr.shape, 1)
    ss_r = jnp.sum(jnp.where(lane < A_ROPE, kr * kr, 0.0), axis=-1, keepdims=True)
    ab = kr * gh_ref[:, A_NOPE:] * tbl_ref[...]
    kk = ab + pltpu.roll(ab, A_ROPE, axis=1)
    g1 = gh_ref[:, :A_NOPE]
    ones = jnp.ones((kr.shape[0], A_PAD - A_V), v_ref.dtype)
    for h in range(A_HEADS):
        acc = jnp.dot(cn, w_ref[:, h * A_PAD:(h + 1) * A_PAD], preferred_element_type=F32)
        kn = acc[:, :A_NOPE]
        ss = jnp.sum(kn * kn, axis=-1, keepdims=True) + ss_r
        rs = lax.rsqrt(ss * (1.0 / A_QK) + EPS)
        kt_ref[h * A_PAD:h * A_PAD + A_NOPE, :] = (kn * g1 * rs).T.astype(kt_ref.dtype)
        kt_ref[h * A_PAD + A_NOPE:(h + 1) * A_PAD, :] = (kk * rs).T.astype(kt_ref.dtype)
        v_ref[:, h * A_PAD:h * A_PAD + A_V] = acc[:, A_NOPE:].astype(v_ref.dtype)
        v_ref[:, h * A_PAD + A_V:(h + 1) * A_PAD] = ones


def _kv_prep(proj, gl_row, w_ukv, gh_row, tbl, tm):
    m = proj.shape[0]
    nk = A_HEADS * A_PAD
    nv = A_HEADS * A_PAD
    return pl.pallas_call(
        _kv_prep_kernel,
        grid=(m // tm,),
        in_specs=[pl.BlockSpec((tm, KV_LORA), lambda i: (i, _Cols.CKV // KV_LORA)),
                  pl.BlockSpec((tm, LANE), lambda i: (i, _Cols.KR // LANE)),
                  pl.BlockSpec((1, KV_LORA), lambda i: (0, 0)),
                  pl.BlockSpec((KV_LORA, nk), lambda i: (0, 0)),
                  pl.BlockSpec((1, A_PAD), lambda i: (0, 0)),
                  pl.BlockSpec((tm, LANE), lambda i: (i, 0))],
        out_specs=[pl.BlockSpec((nk, tm), lambda i: (0, i)),
                   pl.BlockSpec((tm, nv), lambda i: (i, 0))],
        out_shape=[jax.ShapeDtypeStruct((nk, m), BF16),
                   jax.ShapeDtypeStruct((m, nv), BF16)],
        compiler_params=_cparams(("parallel",)),
        name="kv_prep",
    )(proj, proj, gl_row, w_ukv, gh_row, tbl)


def _attn_a_flat_kernel(q_ref, kt_ref, v_ref, g_ref, o_ref, s_ref, p_ref, bias_ref, oacc_ref,
                        *, tq, nq):
    n_units = nq * (nq + 1) // 2

    def advance(unit):
        qb, u = unit
        done = u == qb
        return jnp.where(done, qb + 1, qb), jnp.where(done, 0, u + 1)

    def key_block(unit):
        qb, u = unit
        return jnp.where(u == 0, qb, u - 1)

    def scores(unit, slot):
        qb, u = unit
        q = q_ref[pl.ds(pl.multiple_of(qb * tq, tq), tq), :]
        kt = kt_ref[:, pl.ds(pl.multiple_of(key_block(unit) * tq, tq), tq)]
        own = jnp.where(u == 0, 1, 0)
        s_ref[slot] = jnp.dot(q, kt, preferred_element_type=F32) + bias_ref[own]

    def p_times_v(unit, slot):
        v = v_ref[pl.ds(pl.multiple_of(key_block(unit) * tq, tq), tq), :]
        return jnp.dot(p_ref[slot], v, preferred_element_type=F32)

    def softmax(unit, slot, m, acc):
        s = s_ref[slot]
        m = jnp.where(unit[1] == 0, NEG, m)
        m_new = jnp.maximum(m, jnp.max(s, axis=-1, keepdims=True))
        p_ref[slot] = jnp.exp2(s - m_new).astype(BF16)
        return m_new, jnp.exp2(m - m_new) * acc

    def step(slot, state):
        m, pend, u1, u2, u3 = state
        scores(u1, slot)
        acc = pend + p_times_v(u3, slot)
        oacc_ref[u3[0]] = acc
        m, pend = softmax(u2, 1 - slot, m, acc)
        return m, pend, advance(u1), u1, u2

    row_chunk = lax.broadcasted_iota(jnp.int32, (tq, tq), 0) // CHUNK
    key_chunk = lax.broadcasted_iota(jnp.int32, (tq, tq), 1) // CHUNK
    bias_ref[0] = jnp.zeros((tq, tq), F32)
    bias_ref[1] = jnp.where(key_chunk <= row_chunk, 0.0, NEG)
    p_ref[1] = jnp.zeros((tq, tq), BF16)
    zero = jnp.int32(0)
    first = (zero, zero)
    scores(first, 0)
    state = (jnp.full((tq, 1), NEG, F32), jnp.zeros((tq, A_PAD), F32), advance(first), first, first)

    def steps(count, state):
        for k in range(count):
            state = step((1 + k) % 2, state)
        return state

    n_loop = n_units - 1
    rest = n_loop % STEPS_PER_TRIP
    state = lax.fori_loop(0, n_loop // STEPS_PER_TRIP, lambda _, st: steps(STEPS_PER_TRIP, st),
                          state)
    state = lax.fori_loop(0, rest // 2, lambda _, st: steps(2, st), state)
    state = steps(rest % 2, state)
    m, pend, _, u2, u3 = state
    last_slot = (n_units - 1) % 2
    acc = pend + p_times_v(u3, 1 - last_slot)
    _, pend = softmax(u2, last_slot, m, acc)
    oacc_ref[u2[0]] = pend + p_times_v(u2, last_slot)
    for qb in range(nq):
        acc = oacc_ref[qb]
        gate = g_ref[qb * tq:(qb + 1) * tq, :].astype(F32)
        o = acc[:, :A_V] / acc[:, A_V:]
        o_ref[qb * tq:(qb + 1) * tq, :] = (o * _silu(gate)).astype(o_ref.dtype)


def _attn_a_flat(q_a, kt_a, v_a, proj, batch, seq, tq):
    m = q_a.shape[0]
    nq = seq // tq
    gate0 = _Cols.GA // A_V
    return pl.pallas_call(
        functools.partial(_attn_a_flat_kernel, tq=tq, nq=nq),
        grid=(batch, A_HEADS),
        in_specs=[pl.BlockSpec((seq, A_PAD), lambda b, h: (b, h)),
                  pl.BlockSpec((A_PAD, seq), lambda b, h: (h, b)),
                  pl.BlockSpec((seq, A_PAD), lambda b, h: (b, h)),
                  pl.BlockSpec((seq, A_V), lambda b, h: (b, gate0 + h))],
        out_specs=pl.BlockSpec((seq, A_V), lambda b, h: (b, h)),
        out_shape=jax.ShapeDtypeStruct((m, A_HEADS * A_V), BF16),
        scratch_shapes=[pltpu.VMEM((2, tq, tq), F32), pltpu.VMEM((2, tq, tq), BF16),
                        pltpu.VMEM((2, tq, tq), F32), pltpu.VMEM((nq, tq, A_PAD), F32)],
        compiler_params=_cparams(("parallel", "arbitrary")),
        name="attn_a",
    )(q_a, kt_a, v_a, proj)


def _attn_b_flat_kernel(q_ref, k_ref, v_ref, g_ref, o_ref,
                        z_ref, zc_ref, hl_ref, w_ref, bias_ref, oacc_ref, *, tq, ks, nq, scale2):
    per_block = tq // ks
    assert per_block == 2, "bias tables below cover the two own-block units"
    n_units = per_block * nq * (nq + 1) // 2
    r = lax.broadcasted_iota(jnp.int32, (ks, ks), 0)
    c = lax.broadcasted_iota(jnp.int32, (ks, ks), 1)
    tri = jnp.where(r >= c, 1.0, 0.0).astype(BF16)

    def advance(unit):
        qb, u = unit
        done = u == per_block * qb + per_block - 1
        return jnp.where(done, qb + 1, qb), jnp.where(done, 0, u + 1)

    def key_start(unit):
        qb, u = unit
        sub = jnp.clip(per_block * qb + per_block - 1 - u, 0, per_block * nq - 1)
        return pl.multiple_of(sub * ks, ks)

    def stage_scores(t, unit, slot):
        qb, u = unit
        q = q_ref[pl.ds(pl.multiple_of(jnp.minimum(qb, nq - 1) * tq, tq), tq), :]
        z = lax.dot_general(q, k_ref[pl.ds(key_start(unit), ks), :], _NT,
                            preferred_element_type=F32)
        table = jnp.where(t < n_units, jnp.where(u < per_block, u + 1, 0), 3)
        z_ref[slot] = z * scale2 + bias_ref[table]

    def stage_logs(slot):
        z = z_ref[slot]
        neg_abs = lax.bitcast_convert_type(
            lax.bitcast_convert_type(z, jnp.uint32) | jnp.uint32(0x80000000), F32)
        nlf = jnp.maximum(z, 0.0) + jnp.log(1.0 + jnp.exp2(neg_abs)) * LOG2E
        hl_ref[slot] = nlf.astype(BF16)
        zc_ref[slot] = z

    def stage_weights(unit, slot, mass):
        mass = jnp.where(unit[1] == 0, 0.0, mass)
        tail = jnp.dot(hl_ref[slot], tri, preferred_element_type=F32)
        w_ref[slot] = jnp.exp2(zc_ref[slot] - tail - mass).astype(BF16)
        return mass + tail[:, :1]

    def stage_pv(unit, slot, acc):
        acc = jnp.where(unit[1] == 0, 0.0, acc)
        v = v_ref[pl.ds(key_start(unit), ks), :]
        acc = acc + jnp.dot(w_ref[slot], v, preferred_element_type=F32)
        oacc_ref[jnp.minimum(unit[0], nq - 1)] = acc
        return acc

    def step(slot, state):
        t, mass, acc, u1, u2, u3, u4 = state
        stage_scores(t, u1, slot)
        acc = stage_pv(u4, 1 - slot, acc)
        mass = stage_weights(u3, slot, mass)
        stage_logs(1 - slot)
        return t + 1, mass, acc, advance(u1), u1, u2, u3

    def steps(count, state):
        for k in range(count):
            state = step(k % 2, state)
        return state

    d = (lax.broadcasted_iota(jnp.int32, (tq, ks), 0) - lax.broadcasted_iota(jnp.int32, (tq, ks), 1))
    bias_ref[0] = jnp.zeros((tq, ks), F32)
    bias_ref[1] = jnp.where(d > ks, 0.0, NEG)
    bias_ref[2] = jnp.where(d > 0, 0.0, NEG)
    bias_ref[3] = jnp.full((tq, ks), NEG, F32)
    z_ref[1] = jnp.full((tq, ks), NEG, F32)
    zc_ref[0] = jnp.full((tq, ks), NEG, F32)
    hl_ref[0] = jnp.zeros((tq, ks), BF16)
    w_ref[1] = jnp.zeros((tq, ks), BF16)
    zero = jnp.int32(0)
    first = (zero, zero)
    state = (zero, jnp.zeros((tq, 1), F32), jnp.zeros((tq, B_DIM), F32), first, first, first, first)
    n_steps = n_units + 3
    rest = n_steps % STEPS_PER_TRIP
    state = lax.fori_loop(0, n_steps // STEPS_PER_TRIP, lambda _, st: steps(STEPS_PER_TRIP, st),
                          state)
    state = lax.fori_loop(0, rest // 2, lambda _, st: steps(2, st), state)
    steps(rest % 2, state)
    for qb in range(nq):
        gate = g_ref[qb * tq:(qb + 1) * tq, :].astype(F32)
        o_ref[qb * tq:(qb + 1) * tq, :] = (oacc_ref[qb] * _silu(gate)).astype(o_ref.dtype)


def _attn_b_flat(proj, batch, seq, tq, ks):
    m = proj.shape[0]
    nq = seq // tq
    q0 = _Cols.QKVB // B_DIM
    k0 = q0 + B_HEADS
    v0 = k0 + B_HEADS
    g0 = _Cols.GB // B_DIM
    return pl.pallas_call(
        functools.partial(_attn_b_flat_kernel, tq=tq, ks=ks, nq=nq,
                          scale2=LOG2E / math.sqrt(B_DIM)),
        grid=(batch, B_HEADS),
        in_specs=[pl.BlockSpec((seq, B_DIM), lambda b, h: (b, q0 + h)),
                  pl.BlockSpec((seq, B_DIM), lambda b, h: (b, k0 + h)),
                  pl.BlockSpec((seq, B_DIM), lambda b, h: (b, v0 + h)),
                  pl.BlockSpec((seq, B_DIM), lambda b, h: (b, g0 + h))],
        out_specs=pl.BlockSpec((seq, B_DIM), lambda b, h: (b, h)),
        out_shape=jax.ShapeDtypeStruct((m, B_HEADS * B_DIM), BF16),
        scratch_shapes=[pltpu.VMEM((2, tq, ks), F32), pltpu.VMEM((2, tq, ks), F32),
                        pltpu.VMEM((2, tq, ks), BF16), pltpu.VMEM((2, tq, ks), BF16),
                        pltpu.VMEM((4, tq, ks), F32), pltpu.VMEM((nq, tq, B_DIM), F32)],
        compiler_params=_cparams(("parallel", "arbitrary")),
        name="attn_b",
    )(proj, proj, proj, proj)


def _mem_prep_kernel(mem_ref, g_ref, w_ref, gk_ref, o_ref):
    j = pl.program_id(0)
    x = mem_ref[...]
    xn = (x * lax.rsqrt(jnp.mean(x * x, axis=-1, keepdims=True) + EPS) * g_ref[...]).astype(BF16)
    y = jnp.dot(xn, w_ref[...].astype(BF16), preferred_element_type=F32)
    yk = y * lax.rsqrt(jnp.mean(y * y, axis=-1, keepdims=True) + EPS) * gk_ref[...]
    o_ref[...] = jnp.where(j < M_HEADS, yk, y).astype(o_ref.dtype)


def _mem_prep(mem2d, g_row, w_mkv, layer, gk_row):
    mm, d = mem2d.shape
    n = 2 * M_HEADS * M_DIM
    return pl.pallas_call(
        _mem_prep_kernel,
        grid=(2 * M_HEADS,),
        in_specs=[pl.BlockSpec((mm, d), lambda j: (0, 0)),
                  pl.BlockSpec((1, d), lambda j: (0, 0)),
                  pl.BlockSpec((None, d, M_DIM), lambda j: (layer, 0, j)),
                  pl.BlockSpec((1, M_DIM), lambda j: (0, 0))],
        out_specs=pl.BlockSpec((mm, M_DIM), lambda j: (0, j)),
        out_shape=jax.ShapeDtypeStruct((mm, n), BF16),
        compiler_params=_cparams(("arbitrary",)),
        name="mem_prep",
    )(mem2d, g_row, w_mkv, gk_row)


def _attn_m_kernel(q_ref, k_ref, v_ref, g_ref, gq_ref, o_ref, *, scale):
    q = q_ref[...].astype(F32)
    qn = q * (lax.rsqrt(jnp.mean(q * q, axis=-1, keepdims=True) + EPS) * scale) * gq_ref[...]
    s = lax.dot_general(qn.astype(BF16), k_ref[...], _NT, preferred_element_type=F32)
    p = jnp.exp(s - jnp.max(s, axis=-1, keepdims=True))
    l = jnp.sum(p, axis=-1, keepdims=True)
    o = jnp.dot(p.astype(BF16), v_ref[...], preferred_element_type=F32) / l
    o_ref[...] = (o * _silu(g_ref[...].astype(F32))).astype(o_ref.dtype)


def _attn_m(proj, kv_m, gq_row, seq, mem_len, tm):
    m = proj.shape[0]
    nb = seq // tm
    q0 = _Cols.QM // M_DIM
    g0 = _Cols.GM // M_DIM
    assert mem_len == M_DIM, "memory block spec assumes MEM_LEN == M_DIM rows per batch"
    return pl.pallas_call(
        functools.partial(_attn_m_kernel, scale=1.0 / math.sqrt(M_DIM)),
        grid=(m // tm, M_HEADS),
        in_specs=[pl.BlockSpec((tm, M_DIM), lambda i, h: (i, q0 + h)),
                  pl.BlockSpec((mem_len, M_DIM), lambda i, h: (i // nb, h)),
                  pl.BlockSpec((mem_len, M_DIM), lambda i, h: (i // nb, M_HEADS + h)),
                  pl.BlockSpec((tm, M_DIM), lambda i, h: (i, g0 + h)),
                  pl.BlockSpec((1, M_DIM), lambda i, h: (0, 0))],
        out_specs=pl.BlockSpec((tm, M_DIM), lambda i, h: (i, h)),
        out_shape=jax.ShapeDtypeStruct((m, M_HEADS * M_DIM), BF16),
        compiler_params=_cparams(("parallel", "arbitrary")),
        name="attn_m",
    )(proj, kv_m, kv_m, proj, gq_row)


def _mix_kernel(ua_ref, ub_ref, um_ref, wa_ref, wb_ref, wm_ref, ra_ref, rb_ref, rm_ref, o_ref,
                wa_bf, wb_bf, wm_bf):
    @pl.when(pl.program_id(1) == 0)
    def _cast():
        wa_bf[...] = wa_ref[...].astype(BF16)
        wb_bf[...] = wb_ref[...].astype(BF16)
        wm_bf[...] = wm_ref[...].astype(BF16)

    ya = jnp.dot(ua_ref[...], wa_bf[...], preferred_element_type=F32)
    yb = jnp.dot(ub_ref[...], wb_bf[...], preferred_element_type=F32)
    ym = jnp.dot(um_ref[...], wm_bf[...], preferred_element_type=F32)
    mixed = (_sigmoid(ra_ref[...].astype(F32)) * ya + _sigmoid(rb_ref[...].astype(F32)) * yb
             + _sigmoid(rm_ref[...].astype(F32)) * ym)
    o_ref[...] = mixed.astype(o_ref.dtype)


def _mix(u_a, u_b, u_m, w_pa, w_pb, w_pm, layer, proj, tm, tn):
    m = u_a.shape[0]
    r0 = _Cols.MERGE // tn
    rstep = D_MODEL // tn

    def rspec(branch):
        return pl.BlockSpec((tm, tn), lambda j, i: (i, r0 + branch * rstep + j))

    def wspec(w):
        return pl.BlockSpec((None, w.shape[1], tn), lambda j, i: (layer, 0, j))

    return pl.pallas_call(
        _mix_kernel,
        grid=(D_MODEL // tn, m // tm),
        in_specs=[pl.BlockSpec((tm, u_a.shape[1]), lambda j, i: (i, 0)),
                  pl.BlockSpec((tm, u_b.shape[1]), lambda j, i: (i, 0)),
                  pl.BlockSpec((tm, u_m.shape[1]), lambda j, i: (i, 0)),
                  wspec(w_pa), wspec(w_pb), wspec(w_pm),
                  rspec(0), rspec(1), rspec(2)],
        out_specs=pl.BlockSpec((tm, tn), lambda j, i: (i, j)),
        out_shape=jax.ShapeDtypeStruct((m, D_MODEL), BF16),
        scratch_shapes=[pltpu.VMEM((w_pa.shape[1], tn), BF16), pltpu.VMEM((w_pb.shape[1], tn), BF16),
                        pltpu.VMEM((w_pm.shape[1], tn), BF16)],
        compiler_params=_cparams(("arbitrary", "arbitrary")),
        name="mix",
    )(u_a, u_b, u_m, w_pa, w_pb, w_pm, proj, proj, proj)


def _out_kernel(a_ref, w_ref, x_ref, o_ref, w_bf):
    @pl.when(pl.program_id(1) == 0)
    def _cast():
        w_bf[...] = w_ref[...].astype(BF16)

    o_ref[...] = x_ref[...] + jnp.dot(a_ref[...], w_bf[...], preferred_element_type=F32)


def _out_proj(mixed, w_out, layer, x, tm, tn):
    m, k = mixed.shape
    n = w_out.shape[2]
    return pl.pallas_call(
        _out_kernel,
        grid=(n // tn, m // tm),
        in_specs=[pl.BlockSpec((tm, k), lambda j, i: (i, 0)),
                  pl.BlockSpec((None, k, tn), lambda j, i: (layer, 0, j)),
                  pl.BlockSpec((tm, tn), lambda j, i: (i, j))],
        out_specs=pl.BlockSpec((tm, tn), lambda j, i: (i, j)),
        out_shape=jax.ShapeDtypeStruct((m, n), F32),
        scratch_shapes=[pltpu.VMEM((k, tn), BF16)],
        compiler_params=_cparams(("arbitrary", "arbitrary")),
        name="out_proj",
    )(mixed, w_out, x)


def _rotate_half_cols(w):
    half = A_ROPE // 2
    return jnp.concatenate([-w[..., half:], w[..., :half]], axis=-1)


def _rope_gain(g_rope):
    half = A_ROPE // 2
    return jnp.concatenate([g_rope, g_rope[half:], g_rope[:half]])


def _prep_w_uq(w_uq):
    w = w_uq.reshape(Q_LORA, A_HEADS, A_QK)
    rope = w[..., A_NOPE:]
    w = jnp.concatenate([w[..., :A_NOPE], rope, _rotate_half_cols(rope)], axis=-1)
    return w.reshape(Q_LORA, A_HEADS * A_PAD).astype(BF16)


def _head_gain(g):
    return jnp.concatenate([g[:A_NOPE], _rope_gain(g[A_NOPE:])])[None, :]


def _tiles(m, seq):
    return dict(
        row=min(512, m),
        mm_m=min(1024, m),
        mix_n=512,
        attn=min(512, seq // 2),
        b_keys=min(256, seq // 4),
    )


def kernel(x, mem, positions, g_pre, w_in, g_q_lat, w_uq, g_kv_lat, w_ukv, g_qn_a, g_kn_a,
           w_pa, w_pb, g_mem, w_mkv, g_qn_m, g_kn_m, w_pm, w_out):
    batch, seq, d = x.shape
    mem_len = mem.shape[1]
    depth = w_in.shape[0]
    m = batch * seq
    t = _tiles(m, seq)

    half = A_ROPE // 2
    freqs = ROPE_THETA ** (-jnp.arange(half, dtype=F32) / half)
    freq_row = jnp.tile(freqs, LANE // half)[None, :]
    pos_b = jnp.broadcast_to(positions.reshape(m, 1).astype(F32), (m, LANE))
    tbl = _rope_table(pos_b, freq_row, t["row"])

    xs = x.reshape(m, d)
    mem2d = mem.reshape(batch * mem_len, d)
    w_in_t = jnp.swapaxes(w_in, 1, 2)
    for l in range(depth):
        h = _norm(xs, g_pre[l][None, :], t["row"])
        proj = _in_proj(h, w_in_t, l, t["mm_m"])

        q_a = _q_prep(proj, g_q_lat[l][None, :], _prep_w_uq(w_uq[l]), _head_gain(g_qn_a[l]),
                      tbl, t["row"])
        k_a, v_a = _kv_prep(proj, g_kv_lat[l][None, :], w_ukv[l].astype(BF16),
                            _head_gain(g_kn_a[l]), tbl, t["row"])
        u_a = _attn_a_flat(q_a, k_a, v_a, proj, batch, seq, t["attn"])
        u_b = _attn_b_flat(proj, batch, seq, t["attn"], t["b_keys"])

        kv_m = _mem_prep(mem2d, g_mem[l][None, :], w_mkv, l, g_kn_m[l][None, :])
        u_m = _attn_m(proj, kv_m, g_qn_m[l][None, :], seq, mem_len, t["row"])

        mixed = _mix(u_a, u_b, u_m, w_pa, w_pb, w_pm, l, proj, t["mm_m"], t["mix_n"])
        xs = _out_proj(mixed, w_out, l, xs, t["mm_m"], t["mix_n"])
    return xs.reshape(batch, seq, d)
```

```python
import functools
import math

import jax
import jax.numpy as jnp
from jax import lax
from jax.experimental import pallas as pl
from jax.experimental.pallas import tpu as pltpu

F32 = jnp.float32
BF16 = jnp.bfloat16

EPS = 1e-6
LOG2E = 1.4426950408889634
NEG = -1e30
STEPS_PER_TRIP = 16
ROPE_THETA = 10000.0
CHUNK = 64

A_HEADS, A_NOPE, A_ROPE, A_V = 16, 128, 64, 128
A_QK = A_NOPE + A_ROPE
A_PAD = 256
Q_LORA, KV_LORA = 1024, 512
B_HEADS, B_DIM = 8, 128
M_HEADS, M_DIM = 4, 256
D_MODEL = 4096
N_BRANCH = 3

LANE = 128
VMEM_LIMIT = 56 * 1024 * 1024


class _Cols:
    CQ = 0
    CKV = CQ + Q_LORA
    GA = CKV + KV_LORA
    QKVB = GA + A_HEADS * A_V
    GB = QKVB + 3 * B_HEADS * B_DIM
    QM = GB + B_HEADS * B_DIM
    GM = QM + M_HEADS * M_DIM
    MERGE = GM + M_HEADS * M_DIM
    KR = MERGE + N_BRANCH * D_MODEL
    USED = KR + LANE
    TOTAL = 22528


def _cparams(sem):
    return pltpu.CompilerParams(dimension_semantics=sem, vmem_limit_bytes=VMEM_LIMIT)


def _sigmoid(v):
    return 1.0 / (1.0 + jnp.exp(-v))


def _silu(v):
    return v * _sigmoid(v)


_NT = (((1,), (1,)), ((), ()))


def _rope_table_kernel(pos_ref, freq_ref, o_ref):
    ang = pos_ref[...] * freq_ref[...]
    lane = lax.broadcasted_iota(jnp.int32, ang.shape, 1)
    o_ref[...] = jnp.where(lane < A_ROPE, jnp.cos(ang), jnp.sin(ang))


def _rope_table(pos_b, freq_row, tm):
    m = pos_b.shape[0]
    return pl.pallas_call(
        _rope_table_kernel,
        grid=(m // tm,),
        in_specs=[pl.BlockSpec((tm, LANE), lambda i: (i, 0)),
                  pl.BlockSpec((1, LANE), lambda i: (0, 0))],
        out_specs=pl.BlockSpec((tm, LANE), lambda i: (i, 0)),
        out_shape=jax.ShapeDtypeStruct((m, LANE), F32),
        compiler_params=_cparams(("parallel",)),
        name="rope_table",
    )(pos_b, freq_row)


def _norm_kernel(x_ref, g_ref, o_ref):
    x = x_ref[...]
    y = x * lax.rsqrt(jnp.mean(x * x, axis=-1, keepdims=True) + EPS)
    o_ref[...] = (y * g_ref[...]).astype(o_ref.dtype)


def _norm(x, g_row, tm):
    m, d = x.shape
    return pl.pallas_call(
        _norm_kernel,
        grid=(m // tm,),
        in_specs=[pl.BlockSpec((tm, d), lambda i: (i, 0)),
                  pl.BlockSpec((1, d), lambda i: (0, 0))],
        out_specs=pl.BlockSpec((tm, d), lambda i: (i, 0)),
        out_shape=jax.ShapeDtypeStruct((m, d), BF16),
        compiler_params=_cparams(("parallel",)),
        name="norm",
    )(x, g_row)


IN_TN = 1024
KR_SRC = Q_LORA + KV_LORA


def _in_proj_kernel(a_ref, w_ref, wnext_ref, wkr_ref, o_ref, wp_ref):
    j = pl.program_id(0)
    last = pl.num_programs(0) - 1
    cut = KR_SRC % IN_TN
    tail = _Cols.KR % IN_TN
    half = A_ROPE // 2
    assert KR_SRC // IN_TN == 1 and _Cols.KR // IN_TN == _Cols.TOTAL // IN_TN - 1

    @pl.when(pl.program_id(1) == 0)
    def _assemble():
        @pl.when(j == 0)
        def _():
            wp_ref[...] = w_ref[...].astype(BF16)

        @pl.when(j == 1)
        def _():
            wp_ref[:cut, :] = w_ref[:cut, :].astype(BF16)
            wp_ref[cut:IN_TN - A_ROPE, :] = w_ref[cut + A_ROPE:, :].astype(BF16)
            wp_ref[IN_TN - A_ROPE:, :] = wnext_ref[...].astype(BF16)

        @pl.when(jnp.logical_and(j > 1, j < last))
        def _():
            wp_ref[:IN_TN - A_ROPE, :] = w_ref[A_ROPE:, :].astype(BF16)
            wp_ref[IN_TN - A_ROPE:, :] = wnext_ref[...].astype(BF16)

        @pl.when(j == last)
        def _():
            kr = wkr_ref[...]
            wp_ref[:tail, :] = w_ref[A_ROPE:A_ROPE + tail, :].astype(BF16)
            wp_ref[tail:tail + A_ROPE, :] = kr.astype(BF16)
            wp_ref[tail + A_ROPE:tail + A_ROPE + half, :] = (-kr[half:, :]).astype(BF16)
            wp_ref[tail + A_ROPE + half:tail + 2 * A_ROPE, :] = kr[:half, :].astype(BF16)
            wp_ref[tail + 2 * A_ROPE:, :] = jnp.zeros((IN_TN - tail - 2 * A_ROPE, wp_ref.shape[1]),
                                                      BF16)

    o_ref[...] = lax.dot_general(a_ref[...], wp_ref[...], _NT,
                                 preferred_element_type=F32).astype(o_ref.dtype)


def _in_proj(h, w_in_t, layer, tm):
    m, k = h.shape
    n_src = w_in_t.shape[1]
    rope_blocks = IN_TN // A_ROPE
    last_rope_block = n_src // A_ROPE - 1
    return pl.pallas_call(
        _in_proj_kernel,
        grid=(_Cols.TOTAL // IN_TN, m // tm),
        in_specs=[pl.BlockSpec((tm, k), lambda j, i: (i, 0)),
                  pl.BlockSpec((None, IN_TN, k), lambda j, i: (layer, j, 0),
                               pipeline_mode=pl.Buffered(1)),
                  pl.BlockSpec((None, A_ROPE, k),
                               lambda j, i: (layer, jnp.minimum((j + 1) * rope_blocks,
                                                                last_rope_block), 0)),
                  pl.BlockSpec((None, A_ROPE, k), lambda j, i: (layer, KR_SRC // A_ROPE, 0))],
        out_specs=pl.BlockSpec((tm, IN_TN), lambda j, i: (i, j)),
        out_shape=jax.ShapeDtypeStruct((m, _Cols.TOTAL), BF16),
        scratch_shapes=[pltpu.VMEM((IN_TN, k), BF16)],
        compiler_params=_cparams(("arbitrary", "arbitrary")),
        name="in_proj",
    )(h, w_in_t, w_in_t, w_in_t)


def _q_prep_kernel(c_ref, gl_ref, w_ref, gh_ref, tbl_ref, o_ref, *, scale):
    c = c_ref[...].astype(F32)
    cn = c * lax.rsqrt(jnp.mean(c * c, axis=-1, keepdims=True) + EPS) * gl_ref[...]
    cn = cn.astype(BF16)
    tbl = tbl_ref[...]
    g1 = gh_ref[:, :A_NOPE]
    g2t = gh_ref[:, A_NOPE:] * tbl
    lane = lax.broadcasted_iota(jnp.int32, tbl.shape, 1)
    for h in range(A_HEADS):
        acc = jnp.dot(cn, w_ref[:, h * A_PAD:(h + 1) * A_PAD], preferred_element_type=F32)
        v1 = acc[:, :A_NOPE]
        v2 = acc[:, A_NOPE:]
        ss = (jnp.sum(v1 * v1, axis=-1, keepdims=True)
              + jnp.sum(jnp.where(lane < A_ROPE, v2 * v2, 0.0), axis=-1, keepdims=True))
        rs = lax.rsqrt(ss * (1.0 / A_QK) + EPS) * scale
        o_ref[:, h * A_PAD:h * A_PAD + A_NOPE] = (v1 * g1 * rs).astype(o_ref.dtype)
        o_ref[:, h * A_PAD + A_NOPE:(h + 1) * A_PAD] = (v2 * g2t * rs).astype(o_ref.dtype)


def _q_prep(proj, gl_row, w_uq, gh_row, tbl, tm):
    m = proj.shape[0]
    n = A_HEADS * A_PAD
    return pl.pallas_call(
        functools.partial(_q_prep_kernel, scale=LOG2E / math.sqrt(A_QK)),
        grid=(m // tm,),
        in_specs=[pl.BlockSpec((tm, Q_LORA), lambda i: (i, _Cols.CQ // Q_LORA)),
                  pl.BlockSpec((1, Q_LORA), lambda i: (0, 0)),
                  pl.BlockSpec((Q_LORA, n), lambda i: (0, 0)),
                  pl.BlockSpec((1, A_PAD), lambda i: (0, 0)),
                  pl.BlockSpec((tm, LANE), lambda i: (i, 0))],
        out_specs=pl.BlockSpec((tm, n), lambda i: (i, 0)),
        out_shape=jax.ShapeDtypeStruct((m, n), BF16),
        compiler_params=_cparams(("parallel",)),
        name="q_prep",
    )(proj, gl_row, w_uq, gh_row, tbl)


def _kv_prep_kernel(c_ref, kr_ref, gl_ref, w_ref, gh_ref, tbl_ref, kt_ref, v_ref):
    c = c_ref[...].astype(F32)
    cn = c * lax.rsqrt(jnp.mean(c * c, axis=-1, keepdims=True) + EPS) * gl_ref[...]
    cn = cn.astype(BF16)
    kr = kr_ref[...].astype(F32)
    lane = lax.broadcasted_iota(jnp.int32, kr.shape, 1)
    ss_r = jnp.sum(jnp.where(lane < A_ROPE, kr * kr, 0.0), axis=-1, keepdims=True)
    ab = kr * gh_ref[:, A_NOPE:] * tbl_ref[...]
    kk = ab + pltpu.roll(ab, A_ROPE, axis=1)
    g1 = gh_ref[:, :A_NOPE]
    ones = jnp.ones((kr.shape[0], A_PAD - A_V), v_ref.dtype)
    for h in range(A_HEADS):
        acc = jnp.dot(cn, w_ref[:, h * A_PAD:(h + 1) * A_PAD], preferred_element_type=F32)
        kn = acc[:, :A_NOPE]
        ss = jnp.sum(kn * kn, axis=-1, keepdims=True) + ss_r
        rs = lax.rsqrt(ss * (1.0 / A_QK) + EPS)
        kt_ref[h * A_PAD:h * A_PAD + A_NOPE, :] = (kn * g1 * rs).T.astype(kt_ref.dtype)
        kt_ref[h * A_PAD + A_NOPE:(h + 1) * A_PAD, :] = (kk * rs).T.astype(kt_ref.dtype)
        v_ref[:, h * A_PAD:h * A_PAD + A_V] = acc[:, A_NOPE:].astype(v_ref.dtype)
        v_ref[:, h * A_PAD + A_V:(h + 1) * A_PAD] = ones


def _kv_prep(proj, gl_row, w_ukv, gh_row, tbl, tm):
    m = proj.shape[0]
    nk = A_HEADS * A_PAD
    nv = A_HEADS * A_PAD
    return pl.pallas_call(
        _kv_prep_kernel,
        grid=(m // tm,),
        in_specs=[pl.BlockSpec((tm, KV_LORA), lambda i: (i, _Cols.CKV // KV_LORA)),
                  pl.BlockSpec((tm, LANE), lambda i: (i, _Cols.KR // LANE)),
                  pl.BlockSpec((1, KV_LORA), lambda i: (0, 0)),
                  pl.BlockSpec((KV_LORA, nk), lambda i: (0, 0)),
                  pl.BlockSpec((1, A_PAD), lambda i: (0, 0)),
                  pl.BlockSpec((tm, LANE), lambda i: (i, 0))],
        out_specs=[pl.BlockSpec((nk, tm), lambda i: (0, i)),
                   pl.BlockSpec((tm, nv), lambda i: (i, 0))],
        out_shape=[jax.ShapeDtypeStruct((nk, m), BF16),
                   jax.ShapeDtypeStruct((m, nv), BF16)],
        compiler_params=_cparams(("parallel",)),
        name="kv_prep",
    )(proj, proj, gl_row, w_ukv, gh_row, tbl)


def _attn_a_flat_kernel(q_ref, kt_ref, v_ref, g_ref, o_ref, s_ref, p_ref, bias_ref, oacc_ref,
                        *, tq, nq):
    n_units = nq * (nq + 1) // 2

    def advance(unit):
        qb, u = unit
        done = u == qb
        return jnp.where(done, qb + 1, qb), jnp.where(done, 0, u + 1)

    def key_block(unit):
        qb, u = unit
        return jnp.where(u == 0, qb, u - 1)

    def scores(unit, slot):
        qb, u = unit
        q = q_ref[pl.ds(pl.multiple_of(qb * tq, tq), tq), :]
        kt = kt_ref[:, pl.ds(pl.multiple_of(key_block(unit) * tq, tq), tq)]
        own = jnp.where(u == 0, 1, 0)
        s_ref[slot] = jnp.dot(q, kt, preferred_element_type=F32) + bias_ref[own]

    def p_times_v(unit, slot):
        v = v_ref[pl.ds(pl.multiple_of(key_block(unit) * tq, tq), tq), :]
        return jnp.dot(p_ref[slot], v, preferred_element_type=F32)

    def softmax(unit, slot, m, acc):
        s = s_ref[slot]
        m = jnp.where(unit[1] == 0, NEG, m)
        m_new = jnp.maximum(m, jnp.max(s, axis=-1, keepdims=True))
        p_ref[slot] = jnp.exp2(s - m_new).astype(BF16)
        return m_new, jnp.exp2(m - m_new) * acc

    def step(slot, state):
        m, pend, u1, u2, u3 = state
        scores(u1, slot)
        acc = pend + p_times_v(u3, slot)
        oacc_ref[u3[0]] = acc
        m, pend = softmax(u2, 1 - slot, m, acc)
        return m, pend, advance(u1), u1, u2

    row_chunk = lax.broadcasted_iota(jnp.int32, (tq, tq), 0) // CHUNK
    key_chunk = lax.broadcasted_iota(jnp.int32, (tq, tq), 1) // CHUNK
    bias_ref[0] = jnp.zeros((tq, tq), F32)
    bias_ref[1] = jnp.where(key_chunk <= row_chunk, 0.0, NEG)
    p_ref[1] = jnp.zeros((tq, tq), BF16)
    zero = jnp.int32(0)
    first = (zero, zero)
    scores(first, 0)
    state = (jnp.full((tq, 1), NEG, F32), jnp.zeros((tq, A_PAD), F32), advance(first), first, first)

    def steps(count, state):
        for k in range(count):
            state = step((1 + k) % 2, state)
        return state

    n_loop = n_units - 1
    rest = n_loop % STEPS_PER_TRIP
    state = lax.fori_loop(0, n_loop // STEPS_PER_TRIP, lambda _, st: steps(STEPS_PER_TRIP, st),
                          state)
    state = lax.fori_loop(0, rest // 2, lambda _, st: steps(2, st), state)
    state = steps(rest % 2, state)
    m, pend, _, u2, u3 = state
    last_slot = (n_units - 1) % 2
    acc = pend + p_times_v(u3, 1 - last_slot)
    _, pend = softmax(u2, last_slot, m, acc)
    oacc_ref[u2[0]] = pend + p_times_v(u2, last_slot)
    for qb in range(nq):
        acc = oacc_ref[qb]
        gate = g_ref[qb * tq:(qb + 1) * tq, :].astype(F32)
        o = acc[:, :A_V] / acc[:, A_V:]
        o_ref[qb * tq:(qb + 1) * tq, :] = (o * _silu(gate)).astype(o_ref.dtype)


def _attn_a_flat(q_a, kt_a, v_a, proj, batch, seq, tq):
    m = q_a.shape[0]
    nq = seq // tq
    gate0 = _Cols.GA // A_V
    return pl.pallas_call(
        functools.partial(_attn_a_flat_kernel, tq=tq, nq=nq),
        grid=(batch, A_HEADS),
        in_specs=[pl.BlockSpec((seq, A_PAD), lambda b, h: (b, h)),
                  pl.BlockSpec((A_PAD, seq), lambda b, h: (h, b)),
                  pl.BlockSpec((seq, A_PAD), lambda b, h: (b, h)),
                  pl.BlockSpec((seq, A_V), lambda b, h: (b, gate0 + h))],
        out_specs=pl.BlockSpec((seq, A_V), lambda b, h: (b, h)),
        out_shape=jax.ShapeDtypeStruct((m, A_HEADS * A_V), BF16),
        scratch_shapes=[pltpu.VMEM((2, tq, tq), F32), pltpu.VMEM((2, tq, tq), BF16),
                        pltpu.VMEM((2, tq, tq), F32), pltpu.VMEM((nq, tq, A_PAD), F32)],
        compiler_params=_cparams(("parallel", "arbitrary")),
        name="attn_a",
    )(q_a, kt_a, v_a, proj)


def _attn_b_flat_kernel(q_ref, k_ref, v_ref, g_ref, o_ref,
                        z_ref, hl_ref, w_ref, bias_ref, oacc_ref, *, tq, ks, nq, scale2):
    per_block = tq // ks
    assert per_block == 2, "bias tables below cover the two own-block units"
    n_units = per_block * nq * (nq + 1) // 2
    r = lax.broadcasted_iota(jnp.int32, (ks, ks), 0)
    c = lax.broadcasted_iota(jnp.int32, (ks, ks), 1)
    tri = jnp.where(r >= c, 1.0, 0.0).astype(BF16)

    def advance(unit):
        qb, u = unit
        done = u == per_block * qb + per_block - 1
        return jnp.where(done, qb + 1, qb), jnp.where(done, 0, u + 1)

    def key_start(unit):
        qb, u = unit
        sub = jnp.clip(per_block * qb + per_block - 1 - u, 0, per_block * nq - 1)
        return pl.multiple_of(sub * ks, ks)

    def stage_scores(t, unit, zslot):
        qb, u = unit
        q = q_ref[pl.ds(pl.multiple_of(jnp.minimum(qb, nq - 1) * tq, tq), tq), :]
        z = lax.dot_general(q, k_ref[pl.ds(key_start(unit), ks), :], _NT,
                            preferred_element_type=F32)
        table = jnp.where(t < n_units, jnp.where(u < per_block, u + 1, 0), 3)
        z_ref[zslot] = z * scale2 + bias_ref[table]

    def stage_logs(slot, zslot):
        z = z_ref[zslot]
        neg_abs = lax.bitcast_convert_type(
            lax.bitcast_convert_type(z, jnp.uint32) | jnp.uint32(0x80000000), F32)
        nlf = jnp.maximum(z, 0.0) + jnp.log(1.0 + jnp.exp2(neg_abs)) * LOG2E
        hl_ref[slot] = nlf.astype(BF16)

    def stage_weights(unit, slot, zslot, mass):
        mass = jnp.where(unit[1] == 0, 0.0, mass)
        tail = jnp.dot(hl_ref[slot], tri, preferred_element_type=F32)
        w_ref[slot] = jnp.exp2(z_ref[zslot] - tail - mass).astype(BF16)
        return mass + tail[:, :1]

    def stage_pv(unit, slot, acc):
        acc = jnp.where(unit[1] == 0, 0.0, acc)
        v = v_ref[pl.ds(key_start(unit), ks), :]
        acc = acc + jnp.dot(w_ref[slot], v, preferred_element_type=F32)
        oacc_ref[jnp.minimum(unit[0], nq - 1)] = acc
        return acc

    def step(k, state):
        t, mass, acc, u1, u2, u3, u4 = state
        slot = k % 2
        stage_scores(t, u1, k)
        acc = stage_pv(u4, 1 - slot, acc)
        mass = stage_weights(u3, slot, (k - 2) % 4, mass)
        stage_logs(1 - slot, (k - 1) % 4)
        return t + 1, mass, acc, advance(u1), u1, u2, u3

    def steps(count, state):
        for k in range(count):
            state = step(k % 4, state)
        return state

    d = (lax.broadcasted_iota(jnp.int32, (tq, ks), 0) - lax.broadcasted_iota(jnp.int32, (tq, ks), 1))
    bias_ref[0] = jnp.zeros((tq, ks), F32)
    bias_ref[1] = jnp.where(d > ks, 0.0, NEG)
    bias_ref[2] = jnp.where(d > 0, 0.0, NEG)
    bias_ref[3] = jnp.full((tq, ks), NEG, F32)
    z_ref[2] = jnp.full((tq, ks), NEG, F32)
    z_ref[3] = jnp.full((tq, ks), NEG, F32)
    hl_ref[0] = jnp.zeros((tq, ks), BF16)
    w_ref[1] = jnp.zeros((tq, ks), BF16)
    zero = jnp.int32(0)
    first = (zero, zero)
    state = (zero, jnp.zeros((tq, 1), F32), jnp.zeros((tq, B_DIM), F32), first, first, first, first)
    n_steps = n_units + 3
    rest = n_steps % STEPS_PER_TRIP
    state = lax.fori_loop(0, n_steps // STEPS_PER_TRIP, lambda _, st: steps(STEPS_PER_TRIP, st),
                          state)
    state = lax.fori_loop(0, rest // 4, lambda _, st: steps(4, st), state)
    steps(rest % 4, state)
    for qb in range(nq):
        gate = g_ref[qb * tq:(qb + 1) * tq, :].astype(F32)
        o_ref[qb * tq:(qb + 1) * tq, :] = (oacc_ref[qb] * _silu(gate)).astype(o_ref.dtype)


def _attn_b_flat(proj, batch, seq, tq, ks):
    m = proj.shape[0]
    nq = seq // tq
    q0 = _Cols.QKVB // B_DIM
    k0 = q0 + B_HEADS
    v0 = k0 + B_HEADS
    g0 = _Cols.GB // B_DIM
    return pl.pallas_call(
        functools.partial(_attn_b_flat_kernel, tq=tq, ks=ks, nq=nq,
                          scale2=LOG2E / math.sqrt(B_DIM)),
        grid=(batch, B_HEADS),
        in_specs=[pl.BlockSpec((seq, B_DIM), lambda b, h: (b, q0 + h)),
                  pl.BlockSpec((seq, B_DIM), lambda b, h: (b, k0 + h)),
                  pl.BlockSpec((seq, B_DIM), lambda b, h: (b, v0 + h)),
                  pl.BlockSpec((seq, B_DIM), lambda b, h: (b, g0 + h))],
        out_specs=pl.BlockSpec((seq, B_DIM), lambda b, h: (b, h)),
        out_shape=jax.ShapeDtypeStruct((m, B_HEADS * B_DIM), BF16),
        scratch_shapes=[pltpu.VMEM((4, tq, ks), F32),
                        pltpu.VMEM((2, tq, ks), BF16), pltpu.VMEM((2, tq, ks), BF16),
                        pltpu.VMEM((4, tq, ks), F32), pltpu.VMEM((nq, tq, B_DIM), F32)],
        compiler_params=_cparams(("parallel", "arbitrary")),
        name="attn_b",
    )(proj, proj, proj, proj)


def _mem_prep_kernel(mem_ref, g_ref, w_ref, gk_ref, o_ref):
    j = pl.program_id(0)
    x = mem_ref[...]
    xn = (x * lax.rsqrt(jnp.mean(x * x, axis=-1, keepdims=True) + EPS) * g_ref[...]).astype(BF16)
    y = jnp.dot(xn, w_ref[...].astype(BF16), preferred_element_type=F32)
    yk = y * lax.rsqrt(jnp.mean(y * y, axis=-1, keepdims=True) + EPS) * gk_ref[...]
    o_ref[...] = jnp.where(j < M_HEADS, yk, y).astype(o_ref.dtype)


def _mem_prep(mem2d, g_row, w_mkv, layer, gk_row):
    mm, d = mem2d.shape
    n = 2 * M_HEADS * M_DIM
    return pl.pallas_call(
        _mem_prep_kernel,
        grid=(2 * M_HEADS,),
        in_specs=[pl.BlockSpec((mm, d), lambda j: (0, 0)),
                  pl.BlockSpec((1, d), lambda j: (0, 0)),
                  pl.BlockSpec((None, d, M_DIM), lambda j: (layer, 0, j)),
                  pl.BlockSpec((1, M_DIM), lambda j: (0, 0))],
        out_specs=pl.BlockSpec((mm, M_DIM), lambda j: (0, j)),
        out_shape=jax.ShapeDtypeStruct((mm, n), BF16),
        compiler_params=_cparams(("arbitrary",)),
        name="mem_prep",
    )(mem2d, g_row, w_mkv, gk_row)


def _attn_m_kernel(q_ref, k_ref, v_ref, g_ref, gq_ref, o_ref, *, scale):
    q = q_ref[...].astype(F32)
    qn = q * (lax.rsqrt(jnp.mean(q * q, axis=-1, keepdims=True) + EPS) * scale) * gq_ref[...]
    s = lax.dot_general(qn.astype(BF16), k_ref[...], _NT, preferred_element_type=F32)
    p = jnp.exp(s - jnp.max(s, axis=-1, keepdims=True))
    l = jnp.sum(p, axis=-1, keepdims=True)
    o = jnp.dot(p.astype(BF16), v_ref[...], preferred_element_type=F32) / l
    o_ref[...] = (o * _silu(g_ref[...].astype(F32))).astype(o_ref.dtype)


def _attn_m(proj, kv_m, gq_row, seq, mem_len, tm):
    m = proj.shape[0]
    nb = seq // tm
    q0 = _Cols.QM // M_DIM
    g0 = _Cols.GM // M_DIM
    assert mem_len == M_DIM, "memory block spec assumes MEM_LEN == M_DIM rows per batch"
    return pl.pallas_call(
        functools.partial(_attn_m_kernel, scale=1.0 / math.sqrt(M_DIM)),
        grid=(m // tm, M_HEADS),
        in_specs=[pl.BlockSpec((tm, M_DIM), lambda i, h: (i, q0 + h)),
                  pl.BlockSpec((mem_len, M_DIM), lambda i, h: (i // nb, h)),
                  pl.BlockSpec((mem_len, M_DIM), lambda i, h: (i // nb, M_HEADS + h)),
                  pl.BlockSpec((tm, M_DIM), lambda i, h: (i, g0 + h)),
                  pl.BlockSpec((1, M_DIM), lambda i, h: (0, 0))],
        out_specs=pl.BlockSpec((tm, M_DIM), lambda i, h: (i, h)),
        out_shape=jax.ShapeDtypeStruct((m, M_HEADS * M_DIM), BF16),
        compiler_params=_cparams(("parallel", "arbitrary")),
        name="attn_m",
    )(proj, kv_m, kv_m, proj, gq_row)


def _mix_kernel(ua_ref, ub_ref, um_ref, wa_ref, wb_ref, wm_ref, ra_ref, rb_ref, rm_ref, o_ref,
                wa_bf, wb_bf, wm_bf):
    @pl.when(pl.program_id(1) == 0)
    def _cast():
        wa_bf[...] = wa_ref[...].astype(BF16)
        wb_bf[...] = wb_ref[...].astype(BF16)
        wm_bf[...] = wm_ref[...].astype(BF16)

    ya = jnp.dot(ua_ref[...], wa_bf[...], preferred_element_type=F32)
    yb = jnp.dot(ub_ref[...], wb_bf[...], preferred_element_type=F32)
    ym = jnp.dot(um_ref[...], wm_bf[...], preferred_element_type=F32)
    mixed = (_sigmoid(ra_ref[...].astype(F32)) * ya + _sigmoid(rb_ref[...].astype(F32)) * yb
             + _sigmoid(rm_ref[...].astype(F32)) * ym)
    o_ref[...] = mixed.astype(o_ref.dtype)


def _mix(u_a, u_b, u_m, w_pa, w_pb, w_pm, layer, proj, tm, tn):
    m = u_a.shape[0]
    r0 = _Cols.MERGE // tn
    rstep = D_MODEL // tn

    def rspec(branch):
        return pl.BlockSpec((tm, tn), lambda j, i: (i, r0 + branch * rstep + j))

    def wspec(w):
        return pl.BlockSpec((None, w.shape[1], tn), lambda j, i: (layer, 0, j))

    return pl.pallas_call(
        _mix_kernel,
        grid=(D_MODEL // tn, m // tm),
        in_specs=[pl.BlockSpec((tm, u_a.shape[1]), lambda j, i: (i, 0)),
                  pl.BlockSpec((tm, u_b.shape[1]), lambda j, i: (i, 0)),
                  pl.BlockSpec((tm, u_m.shape[1]), lambda j, i: (i, 0)),
                  wspec(w_pa), wspec(w_pb), wspec(w_pm),
                  rspec(0), rspec(1), rspec(2)],
        out_specs=pl.BlockSpec((tm, tn), lambda j, i: (i, j)),
        out_shape=jax.ShapeDtypeStruct((m, D_MODEL), BF16),
        scratch_shapes=[pltpu.VMEM((w_pa.shape[1], tn), BF16), pltpu.VMEM((w_pb.shape[1], tn), BF16),
                        pltpu.VMEM((w_pm.shape[1], tn), BF16)],
        compiler_params=_cparams(("arbitrary", "arbitrary")),
        name="mix",
    )(u_a, u_b, u_m, w_pa, w_pb, w_pm, proj, proj, proj)


def _out_kernel(a_ref, w_ref, x_ref, o_ref, w_bf):
    @pl.when(pl.program_id(1) == 0)
    def _cast():
        w_bf[...] = w_ref[...].astype(BF16)

    o_ref[...] = x_ref[...] + jnp.dot(a_ref[...], w_bf[...], preferred_element_type=F32)


def _out_proj(mixed, w_out, layer, x, tm, tn):
    m, k = mixed.shape
    n = w_out.shape[2]
    return pl.pallas_call(
        _out_kernel,
        grid=(n // tn, m // tm),
        in_specs=[pl.BlockSpec((tm, k), lambda j, i: (i, 0)),
                  pl.BlockSpec((None, k, tn), lambda j, i: (layer, 0, j)),
                  pl.BlockSpec((tm, tn), lambda j, i: (i, j))],
        out_specs=pl.BlockSpec((tm, tn), lambda j, i: (i, j)),
        out_shape=jax.ShapeDtypeStruct((m, n), F32),
        scratch_shapes=[pltpu.VMEM((k, tn), BF16)],
        compiler_params=_cparams(("arbitrary", "arbitrary")),
        name="out_proj",
    )(mixed, w_out, x)


def _rotate_half_cols(w):
    half = A_ROPE // 2
    return jnp.concatenate([-w[..., half:], w[..., :half]], axis=-1)


def _rope_gain(g_rope):
    half = A_ROPE // 2
    return jnp.concatenate([g_rope, g_rope[half:], g_rope[:half]])


def _prep_w_uq(w_uq):
    w = w_uq.reshape(Q_LORA, A_HEADS, A_QK)
    rope = w[..., A_NOPE:]
    w = jnp.concatenate([w[..., :A_NOPE], rope, _rotate_half_cols(rope)], axis=-1)
    return w.reshape(Q_LORA, A_HEADS * A_PAD).astype(BF16)


def _head_gain(g):
    return jnp.concatenate([g[:A_NOPE], _rope_gain(g[A_NOPE:])])[None, :]


def _tiles(m, seq):
    return dict(
        row=min(512, m),
        mm_m=min(1024, m),
        mix_n=512,
        attn=min(512, seq // 2),
        b_keys=min(256, seq // 4),
    )


def kernel(x, mem, positions, g_pre, w_in, g_q_lat, w_uq, g_kv_lat, w_ukv, g_qn_a, g_kn_a,
           w_pa, w_pb, g_mem, w_mkv, g_qn_m, g_kn_m, w_pm, w_out):
    batch, seq, d = x.shape
    mem_len = mem.shape[1]
    depth = w_in.shape[0]
    m = batch * seq
    t = _tiles(m, seq)

    half = A_ROPE // 2
    freqs = ROPE_THETA ** (-jnp.arange(half, dtype=F32) / half)
    freq_row = jnp.tile(freqs, LANE // half)[None, :]
    pos_b = jnp.broadcast_to(positions.reshape(m, 1).astype(F32), (m, LANE))
    tbl = _rope_table(pos_b, freq_row, t["row"])

    xs = x.reshape(m, d)
    mem2d = mem.reshape(batch * mem_len, d)
    w_in_t = jnp.swapaxes(w_in, 1, 2)
    for l in range(depth):
        h = _norm(xs, g_pre[l][None, :], t["row"])
        proj = _in_proj(h, w_in_t, l, t["mm_m"])

        q_a = _q_prep(proj, g_q_lat[l][None, :], _prep_w_uq(w_uq[l]), _head_gain(g_qn_a[l]),
                      tbl, t["row"])
        k_a, v_a = _kv_prep(proj, g_kv_lat[l][None, :], w_ukv[l].astype(BF16),
                            _head_gain(g_kn_a[l]), tbl, t["row"])
        u_a = _attn_a_flat(q_a, k_a, v_a, proj, batch, seq, t["attn"])
        u_b = _attn_b_flat(proj, batch, seq, t["attn"], t["b_keys"])

        kv_m = _mem_prep(mem2d, g_mem[l][None, :], w_mkv, l, g_kn_m[l][None, :])
        u_m = _attn_m(proj, kv_m, g_qn_m[l][None, :], seq, mem_len, t["row"])

        mixed = _mix(u_a, u_b, u_m, w_pa, w_pb, w_pm, l, proj, t["mm_m"], t["mix_n"])
        xs = _out_proj(mixed, w_out, l, xs, t["mm_m"], t["mix_n"])
    return xs.reshape(batch, seq, d)
```

```python
import functools
import math

import jax
import jax.numpy as jnp
from jax import lax
from jax.experimental import pallas as pl
from jax.experimental.pallas import tpu as pltpu

F32 = jnp.float32
BF16 = jnp.bfloat16

EPS = 1e-6
LOG2E = 1.4426950408889634
NEG = -1e30
STEPS_PER_TRIP = 16
ROPE_THETA = 10000.0
CHUNK = 64

A_HEADS, A_NOPE, A_ROPE, A_V = 16, 128, 64, 128
A_QK = A_NOPE + A_ROPE
A_PAD = 256
Q_LORA, KV_LORA = 1024, 512
B_HEADS, B_DIM = 8, 128
M_HEADS, M_DIM = 4, 256
D_MODEL = 4096
N_BRANCH = 3

LANE = 128
VMEM_LIMIT = 56 * 1024 * 1024


class _Cols:
    CQ = 0
    CKV = CQ + Q_LORA
    GA = CKV + KV_LORA
    QKVB = GA + A_HEADS * A_V
    GB = QKVB + 3 * B_HEADS * B_DIM
    QM = GB + B_HEADS * B_DIM
    GM = QM + M_HEADS * M_DIM
    MERGE = GM + M_HEADS * M_DIM
    KR = MERGE + N_BRANCH * D_MODEL
    USED = KR + LANE
    TOTAL = 22528


def _cparams(sem):
    return pltpu.CompilerParams(dimension_semantics=sem, vmem_limit_bytes=VMEM_LIMIT)


def _sigmoid(v):
    return 1.0 / (1.0 + jnp.exp(-v))


def _silu(v):
    return v * _sigmoid(v)


_NT = (((1,), (1,)), ((), ()))


def _rope_table_kernel(pos_ref, freq_ref, o_ref):
    ang = pos_ref[...] * freq_ref[...]
    lane = lax.broadcasted_iota(jnp.int32, ang.shape, 1)
    o_ref[...] = jnp.where(lane < A_ROPE, jnp.cos(ang), jnp.sin(ang))


def _rope_table(pos_b, freq_row, tm):
    m = pos_b.shape[0]
    return pl.pallas_call(
        _rope_table_kernel,
        grid=(m // tm,),
        in_specs=[pl.BlockSpec((tm, LANE), lambda i: (i, 0)),
                  pl.BlockSpec((1, LANE), lambda i: (0, 0))],
        out_specs=pl.BlockSpec((tm, LANE), lambda i: (i, 0)),
        out_shape=jax.ShapeDtypeStruct((m, LANE), F32),
        compiler_params=_cparams(("parallel",)),
        name="rope_table",
    )(pos_b, freq_row)


def _norm_kernel(x_ref, g_ref, o_ref):
    x = x_ref[...]
    y = x * lax.rsqrt(jnp.mean(x * x, axis=-1, keepdims=True) + EPS)
    o_ref[...] = (y * g_ref[...]).astype(o_ref.dtype)


def _norm(x, g_row, tm):
    m, d = x.shape
    return pl.pallas_call(
        _norm_kernel,
        grid=(m // tm,),
        in_specs=[pl.BlockSpec((tm, d), lambda i: (i, 0)),
                  pl.BlockSpec((1, d), lambda i: (0, 0))],
        out_specs=pl.BlockSpec((tm, d), lambda i: (i, 0)),
        out_shape=jax.ShapeDtypeStruct((m, d), BF16),
        compiler_params=_cparams(("parallel",)),
        name="norm",
    )(x, g_row)


IN_TN = 1024
B_SCALE2 = LOG2E / math.sqrt(B_DIM)
assert 2 * IN_TN <= _Cols.QKVB and _Cols.QKVB + B_HEADS * B_DIM <= _Cols.KR - _Cols.KR % IN_TN
KR_SRC = Q_LORA + KV_LORA


def _in_proj_kernel(a_ref, w_ref, wnext_ref, wkr_ref, o_ref, wp_ref):
    j = pl.program_id(0)
    last = pl.num_programs(0) - 1
    cut = KR_SRC % IN_TN
    tail = _Cols.KR % IN_TN
    half = A_ROPE // 2
    assert KR_SRC // IN_TN == 1 and _Cols.KR // IN_TN == _Cols.TOTAL // IN_TN - 1

    @pl.when(pl.program_id(1) == 0)
    def _assemble():
        @pl.when(j == 0)
        def _():
            wp_ref[...] = w_ref[...].astype(BF16)

        @pl.when(j == 1)
        def _():
            wp_ref[:cut, :] = w_ref[:cut, :].astype(BF16)
            wp_ref[cut:IN_TN - A_ROPE, :] = w_ref[cut + A_ROPE:, :].astype(BF16)
            wp_ref[IN_TN - A_ROPE:, :] = wnext_ref[...].astype(BF16)

        @pl.when(jnp.logical_and(j > 1, j < last))
        def _():
            row = j * IN_TN + lax.broadcasted_iota(jnp.int32, (IN_TN, 1), 0)
            is_qb = jnp.logical_and(row >= _Cols.QKVB, row < _Cols.QKVB + B_HEADS * B_DIM)
            sc = jnp.where(is_qb, B_SCALE2, 1.0)
            wp_ref[:IN_TN - A_ROPE, :] = (w_ref[A_ROPE:, :] * sc[:IN_TN - A_ROPE]).astype(BF16)
            wp_ref[IN_TN - A_ROPE:, :] = (wnext_ref[...] * sc[IN_TN - A_ROPE:]).astype(BF16)

        @pl.when(j == last)
        def _():
            kr = wkr_ref[...]
            wp_ref[:tail, :] = w_ref[A_ROPE:A_ROPE + tail, :].astype(BF16)
            wp_ref[tail:tail + A_ROPE, :] = kr.astype(BF16)
            wp_ref[tail + A_ROPE:tail + A_ROPE + half, :] = (-kr[half:, :]).astype(BF16)
            wp_ref[tail + A_ROPE + half:tail + 2 * A_ROPE, :] = kr[:half, :].astype(BF16)
            wp_ref[tail + 2 * A_ROPE:, :] = jnp.zeros((IN_TN - tail - 2 * A_ROPE, wp_ref.shape[1]),
                                                      BF16)

    o_ref[...] = lax.dot_general(a_ref[...], wp_ref[...], _NT,
                                 preferred_element_type=F32).astype(o_ref.dtype)


def _in_proj(h, w_in_t, layer, tm):
    m, k = h.shape
    n_src = w_in_t.shape[1]
    rope_blocks = IN_TN // A_ROPE
    last_rope_block = n_src // A_ROPE - 1
    return pl.pallas_call(
        _in_proj_kernel,
        grid=(_Cols.TOTAL // IN_TN, m // tm),
        in_specs=[pl.BlockSpec((tm, k), lambda j, i: (i, 0)),
                  pl.BlockSpec((None, IN_TN, k), lambda j, i: (layer, j, 0),
                               pipeline_mode=pl.Buffered(1)),
                  pl.BlockSpec((None, A_ROPE, k),
                               lambda j, i: (layer, jnp.minimum((j + 1) * rope_blocks,
                                                                last_rope_block), 0)),
                  pl.BlockSpec((None, A_ROPE, k), lambda j, i: (layer, KR_SRC // A_ROPE, 0))],
        out_specs=pl.BlockSpec((tm, IN_TN), lambda j, i: (i, j)),
        out_shape=jax.ShapeDtypeStruct((m, _Cols.TOTAL), BF16),
        scratch_shapes=[pltpu.VMEM((IN_TN, k), BF16)],
        compiler_params=_cparams(("arbitrary", "arbitrary")),
        name="in_proj",
    )(h, w_in_t, w_in_t, w_in_t)


def _q_prep_kernel(c_ref, gl_ref, w_ref, gh_ref, tbl_ref, o_ref, *, scale):
    c = c_ref[...].astype(F32)
    cn = c * lax.rsqrt(jnp.mean(c * c, axis=-1, keepdims=True) + EPS) * gl_ref[...]
    cn = cn.astype(BF16)
    tbl = tbl_ref[...]
    g1 = gh_ref[:, :A_NOPE]
    g2t = gh_ref[:, A_NOPE:] * tbl
    lane = lax.broadcasted_iota(jnp.int32, tbl.shape, 1)
    for h in range(A_HEADS):
        acc = jnp.dot(cn, w_ref[:, h * A_PAD:(h + 1) * A_PAD], preferred_element_type=F32)
        v1 = acc[:, :A_NOPE]
        v2 = acc[:, A_NOPE:]
        ss = (jnp.sum(v1 * v1, axis=-1, keepdims=True)
              + jnp.sum(jnp.where(lane < A_ROPE, v2 * v2, 0.0), axis=-1, keepdims=True))
        rs = lax.rsqrt(ss * (1.0 / A_QK) + EPS) * scale
        o_ref[:, h * A_PAD:h * A_PAD + A_NOPE] = (v1 * g1 * rs).astype(o_ref.dtype)
        o_ref[:, h * A_PAD + A_NOPE:(h + 1) * A_PAD] = (v2 * g2t * rs).astype(o_ref.dtype)


def _q_prep(proj, gl_row, w_uq, gh_row, tbl, tm):
    m = proj.shape[0]
    n = A_HEADS * A_PAD
    return pl.pallas_call(
        functools.partial(_q_prep_kernel, scale=LOG2E / math.sqrt(A_QK)),
        grid=(m // tm,),
        in_specs=[pl.BlockSpec((tm, Q_LORA), lambda i: (i, _Cols.CQ // Q_LORA)),
                  pl.BlockSpec((1, Q_LORA), lambda i: (0, 0)),
                  pl.BlockSpec((Q_LORA, n), lambda i: (0, 0)),
                  pl.BlockSpec((1, A_PAD), lambda i: (0, 0)),
                  pl.BlockSpec((tm, LANE), lambda i: (i, 0))],
        out_specs=pl.BlockSpec((tm, n), lambda i: (i, 0)),
        out_shape=jax.ShapeDtypeStruct((m, n), BF16),
        compiler_params=_cparams(("parallel",)),
        name="q_prep",
    )(proj, gl_row, w_uq, gh_row, tbl)


def _kv_prep_kernel(c_ref, kr_ref, gl_ref, w_ref, gh_ref, tbl_ref, kt_ref, v_ref):
    c = c_ref[...].astype(F32)
    cn = c * lax.rsqrt(jnp.mean(c * c, axis=-1, keepdims=True) + EPS) * gl_ref[...]
    cn = cn.astype(BF16)
    kr = kr_ref[...].astype(F32)
    lane = lax.broadcasted_iota(jnp.int32, kr.shape, 1)
    ss_r = jnp.sum(jnp.where(lane < A_ROPE, kr * kr, 0.0), axis=-1, keepdims=True)
    ab = kr * gh_ref[:, A_NOPE:] * tbl_ref[...]
    kk = ab + pltpu.roll(ab, A_ROPE, axis=1)
    g1 = gh_ref[:, :A_NOPE]
    ones = jnp.ones((kr.shape[0], A_PAD - A_V), v_ref.dtype)
    for h in range(A_HEADS):
        acc = jnp.dot(cn, w_ref[:, h * A_PAD:(h + 1) * A_PAD], preferred_element_type=F32)
        kn = acc[:, :A_NOPE]
        ss = jnp.sum(kn * kn, axis=-1, keepdims=True) + ss_r
        rs = lax.rsqrt(ss * (1.0 / A_QK) + EPS)
        kt_ref[h * A_PAD:h * A_PAD + A_NOPE, :] = (kn * g1 * rs).T.astype(kt_ref.dtype)
        kt_ref[h * A_PAD + A_NOPE:(h + 1) * A_PAD, :] = (kk * rs).T.astype(kt_ref.dtype)
        v_ref[:, h * A_PAD:h * A_PAD + A_V] = acc[:, A_NOPE:].astype(v_ref.dtype)
        v_ref[:, h * A_PAD + A_V:(h + 1) * A_PAD] = ones


def _kv_prep(proj, gl_row, w_ukv, gh_row, tbl, tm):
    m = proj.shape[0]
    nk = A_HEADS * A_PAD
    nv = A_HEADS * A_PAD
    return pl.pallas_call(
        _kv_prep_kernel,
        grid=(m // tm,),
        in_specs=[pl.BlockSpec((tm, KV_LORA), lambda i: (i, _Cols.CKV // KV_LORA)),
                  pl.BlockSpec((tm, LANE), lambda i: (i, _Cols.KR // LANE)),
                  pl.BlockSpec((1, KV_LORA), lambda i: (0, 0)),
                  pl.BlockSpec((KV_LORA, nk), lambda i: (0, 0)),
                  pl.BlockSpec((1, A_PAD), lambda i: (0, 0)),
                  pl.BlockSpec((tm, LANE), lambda i: (i, 0))],
        out_specs=[pl.BlockSpec((nk, tm), lambda i: (0, i)),
                   pl.BlockSpec((tm, nv), lambda i: (i, 0))],
        out_shape=[jax.ShapeDtypeStruct((nk, m), BF16),
                   jax.ShapeDtypeStruct((m, nv), BF16)],
        compiler_params=_cparams(("parallel",)),
        name="kv_prep",
    )(proj, proj, gl_row, w_ukv, gh_row, tbl)


def _attn_a_flat_kernel(q_ref, kt_ref, v_ref, g_ref, o_ref, s_ref, p_ref, bias_ref, oacc_ref,
                        *, tq, nq):
    n_units = nq * (nq + 1) // 2

    def advance(unit):
        qb, u = unit
        done = u == qb
        return jnp.where(done, qb + 1, qb), jnp.where(done, 0, u + 1)

    def key_block(unit):
        qb, u = unit
        return jnp.where(u == 0, qb, u - 1)

    def scores(unit, slot):
        qb, u = unit
        q = q_ref[pl.ds(pl.multiple_of(qb * tq, tq), tq), :]
        kt = kt_ref[:, pl.ds(pl.multiple_of(key_block(unit) * tq, tq), tq)]
        own = jnp.where(u == 0, 1, 0)
        s_ref[slot] = jnp.dot(q, kt, preferred_element_type=F32) + bias_ref[own]

    def p_times_v(unit, slot):
        v = v_ref[pl.ds(pl.multiple_of(key_block(unit) * tq, tq), tq), :]
        return jnp.dot(p_ref[slot], v, preferred_element_type=F32)

    def softmax(unit, slot, m, acc):
        s = s_ref[slot]
        m = jnp.where(unit[1] == 0, NEG, m)
        m_new = jnp.maximum(m, jnp.max(s, axis=-1, keepdims=True))
        p_ref[slot] = jnp.exp2(s - m_new).astype(BF16)
        return m_new, jnp.exp2(m - m_new) * acc

    def step(slot, state):
        m, pend, u1, u2, u3 = state
        scores(u1, slot)
        acc = pend + p_times_v(u3, slot)
        oacc_ref[u3[0]] = acc
        m, pend = softmax(u2, 1 - slot, m, acc)
        return m, pend, advance(u1), u1, u2

    row_chunk = lax.broadcasted_iota(jnp.int32, (tq, tq), 0) // CHUNK
    key_chunk = lax.broadcasted_iota(jnp.int32, (tq, tq), 1) // CHUNK
    bias_ref[0] = jnp.zeros((tq, tq), F32)
    bias_ref[1] = jnp.where(key_chunk <= row_chunk, 0.0, NEG)
    p_ref[1] = jnp.zeros((tq, tq), BF16)
    zero = jnp.int32(0)
    first = (zero, zero)
    scores(first, 0)
    state = (jnp.full((tq, 1), NEG, F32), jnp.zeros((tq, A_PAD), F32), advance(first), first, first)

    def steps(count, state):
        for k in range(count):
            state = step((1 + k) % 2, state)
        return state

    n_loop = n_units - 1
    rest = n_loop % STEPS_PER_TRIP
    state = lax.fori_loop(0, n_loop // STEPS_PER_TRIP, lambda _, st: steps(STEPS_PER_TRIP, st),
                          state)
    state = lax.fori_loop(0, rest // 2, lambda _, st: steps(2, st), state)
    state = steps(rest % 2, state)
    m, pend, _, u2, u3 = state
    last_slot = (n_units - 1) % 2
    acc = pend + p_times_v(u3, 1 - last_slot)
    _, pend = softmax(u2, last_slot, m, acc)
    oacc_ref[u2[0]] = pend + p_times_v(u2, last_slot)
    for qb in range(nq):
        acc = oacc_ref[qb]
        gate = g_ref[qb * tq:(qb + 1) * tq, :].astype(F32)
        o = acc[:, :A_V] / acc[:, A_V:]
        o_ref[qb * tq:(qb + 1) * tq, :] = (o * _silu(gate)).astype(o_ref.dtype)


def _attn_a_flat(q_a, kt_a, v_a, proj, batch, seq, tq):
    m = q_a.shape[0]
    nq = seq // tq
    gate0 = _Cols.GA // A_V
    return pl.pallas_call(
        functools.partial(_attn_a_flat_kernel, tq=tq, nq=nq),
        grid=(batch, A_HEADS),
        in_specs=[pl.BlockSpec((seq, A_PAD), lambda b, h: (b, h)),
                  pl.BlockSpec((A_PAD, seq), lambda b, h: (h, b)),
                  pl.BlockSpec((seq, A_PAD), lambda b, h: (b, h)),
                  pl.BlockSpec((seq, A_V), lambda b, h: (b, gate0 + h))],
        out_specs=pl.BlockSpec((seq, A_V), lambda b, h: (b, h)),
        out_shape=jax.ShapeDtypeStruct((m, A_HEADS * A_V), BF16),
        scratch_shapes=[pltpu.VMEM((2, tq, tq), F32), pltpu.VMEM((2, tq, tq), BF16),
                        pltpu.VMEM((2, tq, tq), F32), pltpu.VMEM((nq, tq, A_PAD), F32)],
        compiler_params=_cparams(("parallel", "arbitrary")),
        name="attn_a",
    )(q_a, kt_a, v_a, proj)


def _attn_b_flat_kernel(q_ref, k_ref, v_ref, g_ref, o_ref,
                        z_ref, hl_ref, w_ref, bias_ref, oacc_ref, *, tq, ks, nq):
    per_block = tq // ks
    assert per_block == 2, "bias tables below cover the two own-block units"
    n_units = per_block * nq * (nq + 1) // 2
    r = lax.broadcasted_iota(jnp.int32, (ks, ks), 0)
    c = lax.broadcasted_iota(jnp.int32, (ks, ks), 1)
    tri = jnp.where(r >= c, 1.0, 0.0).astype(BF16)

    def advance(unit):
        qb, u = unit
        done = u == per_block * qb + per_block - 1
        return jnp.where(done, qb + 1, qb), jnp.where(done, 0, u + 1)

    def key_start(unit):
        qb, u = unit
        sub = jnp.clip(per_block * qb + per_block - 1 - u, 0, per_block * nq - 1)
        return pl.multiple_of(sub * ks, ks)

    def stage_scores(t, unit, zslot):
        qb, u = unit
        q = q_ref[pl.ds(pl.multiple_of(jnp.minimum(qb, nq - 1) * tq, tq), tq), :]
        z = lax.dot_general(q, k_ref[pl.ds(key_start(unit), ks), :], _NT,
                            preferred_element_type=F32)
        table = jnp.where(t < n_units, jnp.where(u < per_block, u + 1, 0), 3)
        z_ref[zslot] = z + bias_ref[table]

    def stage_logs(slot, zslot):
        z = z_ref[zslot]
        neg_abs = lax.bitcast_convert_type(
            lax.bitcast_convert_type(z, jnp.uint32) | jnp.uint32(0x80000000), F32)
        nlf = jnp.maximum(z, 0.0) + jnp.log(1.0 + jnp.exp2(neg_abs)) * LOG2E
        hl_ref[slot] = nlf.astype(BF16)

    def stage_weights(unit, slot, zslot, mass):
        mass = jnp.where(unit[1] == 0, 0.0, mass)
        tail = jnp.dot(hl_ref[slot], tri, preferred_element_type=F32)
        w_ref[slot] = jnp.exp2(z_ref[zslot] - tail - mass).astype(BF16)
        return mass + tail[:, :1]

    def stage_pv(unit, slot, acc):
        acc = jnp.where(unit[1] == 0, 0.0, acc)
        v = v_ref[pl.ds(key_start(unit), ks), :]
        acc = acc + jnp.dot(w_ref[slot], v, preferred_element_type=F32)
        oacc_ref[jnp.minimum(unit[0], nq - 1)] = acc
        return acc

    def step(k, state):
        t, mass, acc, u1, u2, u3, u4 = state
        slot = k % 2
        stage_scores(t, u1, k)
        acc = stage_pv(u4, 1 - slot, acc)
        mass = stage_weights(u3, slot, (k - 2) % 4, mass)
        stage_logs(1 - slot, (k - 1) % 4)
        return t + 1, mass, acc, advance(u1), u1, u2, u3

    def steps(count, state):
        for k in range(count):
            state = step(k % 4, state)
        return state

    d = (lax.broadcasted_iota(jnp.int32, (tq, ks), 0) - lax.broadcasted_iota(jnp.int32, (tq, ks), 1))
    bias_ref[0] = jnp.zeros((tq, ks), F32)
    bias_ref[1] = jnp.where(d > ks, 0.0, NEG)
    bias_ref[2] = jnp.where(d > 0, 0.0, NEG)
    bias_ref[3] = jnp.full((tq, ks), NEG, F32)
    z_ref[2] = jnp.full((tq, ks), NEG, F32)
    z_ref[3] = jnp.full((tq, ks), NEG, F32)
    hl_ref[0] = jnp.zeros((tq, ks), BF16)
    w_ref[1] = jnp.zeros((tq, ks), BF16)
    zero = jnp.int32(0)
    first = (zero, zero)
    state = (zero, jnp.zeros((tq, 1), F32), jnp.zeros((tq, B_DIM), F32), first, first, first, first)
    n_steps = n_units + 3
    rest = n_steps % STEPS_PER_TRIP
    state = lax.fori_loop(0, n_steps // STEPS_PER_TRIP, lambda _, st: steps(STEPS_PER_TRIP, st),
                          state)
    state = lax.fori_loop(0, rest // 4, lambda _, st: steps(4, st), state)
    steps(rest % 4, state)
    for qb in range(nq):
        gate = g_ref[qb * tq:(qb + 1) * tq, :].astype(F32)
        o_ref[qb * tq:(qb + 1) * tq, :] = (oacc_ref[qb] * _silu(gate)).astype(o_ref.dtype)


def _attn_b_flat(proj, batch, seq, tq, ks):
    m = proj.shape[0]
    nq = seq // tq
    q0 = _Cols.QKVB // B_DIM
    k0 = q0 + B_HEADS
    v0 = k0 + B_HEADS
    g0 = _Cols.GB // B_DIM
    return pl.pallas_call(
        functools.partial(_attn_b_flat_kernel, tq=tq, ks=ks, nq=nq),
        grid=(batch, B_HEADS),
        in_specs=[pl.BlockSpec((seq, B_DIM), lambda b, h: (b, q0 + h)),
                  pl.BlockSpec((seq, B_DIM), lambda b, h: (b, k0 + h)),
                  pl.BlockSpec((seq, B_DIM), lambda b, h: (b, v0 + h)),
                  pl.BlockSpec((seq, B_DIM), lambda b, h: (b, g0 + h))],
        out_specs=pl.BlockSpec((seq, B_DIM), lambda b, h: (b, h)),
        out_shape=jax.ShapeDtypeStruct((m, B_HEADS * B_DIM), BF16),
        scratch_shapes=[pltpu.VMEM((4, tq, ks), F32),
                        pltpu.VMEM((2, tq, ks), BF16), pltpu.VMEM((2, tq, ks), BF16),
                        pltpu.VMEM((4, tq, ks), F32), pltpu.VMEM((nq, tq, B_DIM), F32)],
        compiler_params=_cparams(("parallel", "arbitrary")),
        name="attn_b",
    )(proj, proj, proj, proj)


def _mem_prep_kernel(mem_ref, g_ref, w_ref, gk_ref, o_ref):
    j = pl.program_id(0)
    x = mem_ref[...]
    xn = (x * lax.rsqrt(jnp.mean(x * x, axis=-1, keepdims=True) + EPS) * g_ref[...]).astype(BF16)
    y = jnp.dot(xn, w_ref[...].astype(BF16), preferred_element_type=F32)
    yk = y * lax.rsqrt(jnp.mean(y * y, axis=-1, keepdims=True) + EPS) * gk_ref[...]
    o_ref[...] = jnp.where(j < M_HEADS, yk, y).astype(o_ref.dtype)


def _mem_prep(mem2d, g_row, w_mkv, layer, gk_row):
    mm, d = mem2d.shape
    n = 2 * M_HEADS * M_DIM
    return pl.pallas_call(
        _mem_prep_kernel,
        grid=(2 * M_HEADS,),
        in_specs=[pl.BlockSpec((mm, d), lambda j: (0, 0)),
                  pl.BlockSpec((1, d), lambda j: (0, 0)),
                  pl.BlockSpec((None, d, M_DIM), lambda j: (layer, 0, j)),
                  pl.BlockSpec((1, M_DIM), lambda j: (0, 0))],
        out_specs=pl.BlockSpec((mm, M_DIM), lambda j: (0, j)),
        out_shape=jax.ShapeDtypeStruct((mm, n), BF16),
        compiler_params=_cparams(("arbitrary",)),
        name="mem_prep",
    )(mem2d, g_row, w_mkv, gk_row)


def _attn_m_kernel(q_ref, k_ref, v_ref, g_ref, gq_ref, o_ref, *, scale):
    q = q_ref[...].astype(F32)
    qn = q * (lax.rsqrt(jnp.mean(q * q, axis=-1, keepdims=True) + EPS) * scale) * gq_ref[...]
    s = lax.dot_general(qn.astype(BF16), k_ref[...], _NT, preferred_element_type=F32)
    p = jnp.exp(s - jnp.max(s, axis=-1, keepdims=True))
    l = jnp.sum(p, axis=-1, keepdims=True)
    o = jnp.dot(p.astype(BF16), v_ref[...], preferred_element_type=F32) / l
    o_ref[...] = (o * _silu(g_ref[...].astype(F32))).astype(o_ref.dtype)


def _attn_m(proj, kv_m, gq_row, seq, mem_len, tm):
    m = proj.shape[0]
    nb = seq // tm
    q0 = _Cols.QM // M_DIM
    g0 = _Cols.GM // M_DIM
    assert mem_len == M_DIM, "memory block spec assumes MEM_LEN == M_DIM rows per batch"
    return pl.pallas_call(
        functools.partial(_attn_m_kernel, scale=1.0 / math.sqrt(M_DIM)),
        grid=(m // tm, M_HEADS),
        in_specs=[pl.BlockSpec((tm, M_DIM), lambda i, h: (i, q0 + h)),
                  pl.BlockSpec((mem_len, M_DIM), lambda i, h: (i // nb, h)),
                  pl.BlockSpec((mem_len, M_DIM), lambda i, h: (i // nb, M_HEADS + h)),
                  pl.BlockSpec((tm, M_DIM), lambda i, h: (i, g0 + h)),
                  pl.BlockSpec((1, M_DIM), lambda i, h: (0, 0))],
        out_specs=pl.BlockSpec((tm, M_DIM), lambda i, h: (i, h)),
        out_shape=jax.ShapeDtypeStruct((m, M_HEADS * M_DIM), BF16),
        compiler_params=_cparams(("parallel", "arbitrary")),
        name="attn_m",
    )(proj, kv_m, kv_m, proj, gq_row)


def _mix_kernel(ua_ref, ub_ref, um_ref, wa_ref, wb_ref, wm_ref, ra_ref, rb_ref, rm_ref, o_ref,
                wa_bf, wb_bf, wm_bf):
    @pl.when(pl.program_id(1) == 0)
    def _cast():
        wa_bf[...] = wa_ref[...].astype(BF16)
        wb_bf[...] = wb_ref[...].astype(BF16)
        wm_bf[...] = wm_ref[...].astype(BF16)

    ya = jnp.dot(ua_ref[...], wa_bf[...], preferred_element_type=F32)
    yb = jnp.dot(ub_ref[...], wb_bf[...], preferred_element_type=F32)
    ym = jnp.dot(um_ref[...], wm_bf[...], preferred_element_type=F32)
    mixed = (_sigmoid(ra_ref[...].astype(F32)) * ya + _sigmoid(rb_ref[...].astype(F32)) * yb
             + _sigmoid(rm_ref[...].astype(F32)) * ym)
    o_ref[...] = mixed.astype(o_ref.dtype)


def _mix(u_a, u_b, u_m, w_pa, w_pb, w_pm, layer, proj, tm, tn):
    m = u_a.shape[0]
    r0 = _Cols.MERGE // tn
    rstep = D_MODEL // tn

    def rspec(branch):
        return pl.BlockSpec((tm, tn), lambda j, i: (i, r0 + branch * rstep + j))

    def wspec(w):
        return pl.BlockSpec((None, w.shape[1], tn), lambda j, i: (layer, 0, j))

    return pl.pallas_call(
        _mix_kernel,
        grid=(D_MODEL // tn, m // tm),
        in_specs=[pl.BlockSpec((tm, u_a.shape[1]), lambda j, i: (i, 0)),
                  pl.BlockSpec((tm, u_b.shape[1]), lambda j, i: (i, 0)),
                  pl.BlockSpec((tm, u_m.shape[1]), lambda j, i: (i, 0)),
                  wspec(w_pa), wspec(w_pb), wspec(w_pm),
                  rspec(0), rspec(1), rspec(2)],
        out_specs=pl.BlockSpec((tm, tn), lambda j, i: (i, j)),
        out_shape=jax.ShapeDtypeStruct((m, D_MODEL), BF16),
        scratch_shapes=[pltpu.VMEM((w_pa.shape[1], tn), BF16), pltpu.VMEM((w_pb.shape[1], tn), BF16),
                        pltpu.VMEM((w_pm.shape[1], tn), BF16)],
        compiler_params=_cparams(("arbitrary", "arbitrary")),
        name="mix",
    )(u_a, u_b, u_m, w_pa, w_pb, w_pm, proj, proj, proj)


def _out_kernel(a_ref, w_ref, x_ref, o_ref, w_bf):
    @pl.when(pl.program_id(1) == 0)
    def _cast():
        w_bf[...] = w_ref[...].astype(BF16)

    o_ref[...] = x_ref[...] + jnp.dot(a_ref[...], w_bf[...], preferred_element_type=F32)


def _out_proj(mixed, w_out, layer, x, tm, tn):
    m, k = mixed.shape
    n = w_out.shape[2]
    return pl.pallas_call(
        _out_kernel,
        grid=(n // tn, m // tm),
        in_specs=[pl.BlockSpec((tm, k), lambda j, i: (i, 0)),
                  pl.BlockSpec((None, k, tn), lambda j, i: (layer, 0, j)),
                  pl.BlockSpec((tm, tn), lambda j, i: (i, j))],
        out_specs=pl.BlockSpec((tm, tn), lambda j, i: (i, j)),
        out_shape=jax.ShapeDtypeStruct((m, n), F32),
        scratch_shapes=[pltpu.VMEM((k, tn), BF16)],
        compiler_params=_cparams(("arbitrary", "arbitrary")),
        name="out_proj",
    )(mixed, w_out, x)


def _rotate_half_cols(w):
    half = A_ROPE // 2
    return jnp.concatenate([-w[..., half:], w[..., :half]], axis=-1)


def _rope_gain(g_rope):
    half = A_ROPE // 2
    return jnp.concatenate([g_rope, g_rope[half:], g_rope[:half]])


def _prep_w_uq(w_uq):
    w = w_uq.reshape(Q_LORA, A_HEADS, A_QK)
    rope = w[..., A_NOPE:]
    w = jnp.concatenate([w[..., :A_NOPE], rope, _rotate_half_cols(rope)], axis=-1)
    return w.reshape(Q_LORA, A_HEADS * A_PAD).astype(BF16)


def _head_gain(g):
    return jnp.concatenate([g[:A_NOPE], _rope_gain(g[A_NOPE:])])[None, :]


def _tiles(m, seq):
    return dict(
        row=min(512, m),
        mm_m=min(1024, m),
        mix_n=512,
        attn=min(512, seq // 2),
        b_keys=min(256, seq // 4),
    )


def kernel(x, mem, positions, g_pre, w_in, g_q_lat, w_uq, g_kv_lat, w_ukv, g_qn_a, g_kn_a,
           w_pa, w_pb, g_mem, w_mkv, g_qn_m, g_kn_m, w_pm, w_out):
    batch, seq, d = x.shape
    mem_len = mem.shape[1]
    depth = w_in.shape[0]
    m = batch * seq
    t = _tiles(m, seq)

    half = A_ROPE // 2
    freqs = ROPE_THETA ** (-jnp.arange(half, dtype=F32) / half)
    freq_row = jnp.tile(freqs, LANE // half)[None, :]
    pos_b = jnp.broadcast_to(positions.reshape(m, 1).astype(F32), (m, LANE))
    tbl = _rope_table(pos_b, freq_row, t["row"])

    xs = x.reshape(m, d)
    mem2d = mem.reshape(batch * mem_len, d)
    w_in_t = jnp.swapaxes(w_in, 1, 2)
    for l in range(depth):
        h = _norm(xs, g_pre[l][None, :], t["row"])
        proj = _in_proj(h, w_in_t, l, t["mm_m"])

        q_a = _q_prep(proj, g_q_lat[l][None, :], _prep_w_uq(w_uq[l]), _head_gain(g_qn_a[l]),
                      tbl, t["row"])
        k_a, v_a = _kv_prep(proj, g_kv_lat[l][None, :], w_ukv[l].astype(BF16),
                            _head_gain(g_kn_a[l]), tbl, t["row"])
        u_a = _attn_a_flat(q_a, k_a, v_a, proj, batch, seq, t["attn"])
        u_b = _attn_b_flat(proj, batch, seq, t["attn"], t["b_keys"])

        kv_m = _mem_prep(mem2d, g_mem[l][None, :], w_mkv, l, g_kn_m[l][None, :])
        u_m = _attn_m(proj, kv_m, g_qn_m[l][None, :], seq, mem_len, t["row"])

        mixed = _mix(u_a, u_b, u_m, w_pa, w_pb, w_pm, l, proj, t["mm_m"], t["mix_n"])
        xs = _out_proj(mixed, w_out, l, xs, t["mm_m"], t["mix_n"])
    return xs.reshape(batch, seq, d)
```

```python
import functools
import math

import jax
import jax.numpy as jnp
from jax import lax
from jax.experimental import pallas as pl
from jax.experimental.pallas import tpu as pltpu

F32 = jnp.float32
BF16 = jnp.bfloat16

EPS = 1e-6
LOG2E = 1.4426950408889634
NEG = -1e30
MIX_ROW_CHUNKS = 4
STEPS_PER_TRIP = 16
ROPE_THETA = 10000.0
CHUNK = 64

A_HEADS, A_NOPE, A_ROPE, A_V = 16, 128, 64, 128
A_QK = A_NOPE + A_ROPE
A_PAD = 256
Q_LORA, KV_LORA = 1024, 512
B_HEADS, B_DIM = 8, 128
M_HEADS, M_DIM = 4, 256
D_MODEL = 4096
N_BRANCH = 3

LANE = 128
VMEM_LIMIT = 56 * 1024 * 1024


class _Cols:
    CQ = 0
    CKV = CQ + Q_LORA
    GA = CKV + KV_LORA
    QKVB = GA + A_HEADS * A_V
    GB = QKVB + 3 * B_HEADS * B_DIM
    QM = GB + B_HEADS * B_DIM
    GM = QM + M_HEADS * M_DIM
    MERGE = GM + M_HEADS * M_DIM
    KR = MERGE + N_BRANCH * D_MODEL
    USED = KR + LANE
    TOTAL = 22528


def _cparams(sem):
    return pltpu.CompilerParams(dimension_semantics=sem, vmem_limit_bytes=VMEM_LIMIT)


def _sigmoid(v):
    return 1.0 / (1.0 + jnp.exp(-v))


def _silu(v):
    return v * _sigmoid(v)


_NT = (((1,), (1,)), ((), ()))


def _rope_table_kernel(pos_ref, freq_ref, o_ref):
    ang = pos_ref[...] * freq_ref[...]
    lane = lax.broadcasted_iota(jnp.int32, ang.shape, 1)
    o_ref[...] = jnp.where(lane < A_ROPE, jnp.cos(ang), jnp.sin(ang))


def _rope_table(pos_b, freq_row, tm):
    m = pos_b.shape[0]
    return pl.pallas_call(
        _rope_table_kernel,
        grid=(m // tm,),
        in_specs=[pl.BlockSpec((tm, LANE), lambda i: (i, 0)),
                  pl.BlockSpec((1, LANE), lambda i: (0, 0))],
        out_specs=pl.BlockSpec((tm, LANE), lambda i: (i, 0)),
        out_shape=jax.ShapeDtypeStruct((m, LANE), F32),
        compiler_params=_cparams(("parallel",)),
        name="rope_table",
    )(pos_b, freq_row)


def _norm_kernel(x_ref, g_ref, o_ref):
    x = x_ref[...]
    y = x * lax.rsqrt(jnp.mean(x * x, axis=-1, keepdims=True) + EPS)
    o_ref[...] = (y * g_ref[...]).astype(o_ref.dtype)


def _norm(x, g_row, tm):
    m, d = x.shape
    return pl.pallas_call(
        _norm_kernel,
        grid=(m // tm,),
        in_specs=[pl.BlockSpec((tm, d), lambda i: (i, 0)),
                  pl.BlockSpec((1, d), lambda i: (0, 0))],
        out_specs=pl.BlockSpec((tm, d), lambda i: (i, 0)),
        out_shape=jax.ShapeDtypeStruct((m, d), BF16),
        compiler_params=_cparams(("parallel",)),
        name="norm",
    )(x, g_row)


IN_TN = 1024
B_SCALE2 = LOG2E / math.sqrt(B_DIM)
assert 2 * IN_TN <= _Cols.QKVB and _Cols.QKVB + B_HEADS * B_DIM <= _Cols.KR - _Cols.KR % IN_TN
KR_SRC = Q_LORA + KV_LORA


def _in_proj_kernel(a_ref, w_ref, wnext_ref, wkr_ref, o_ref, wp_ref):
    j = pl.program_id(0)
    last = pl.num_programs(0) - 1
    cut = KR_SRC % IN_TN
    tail = _Cols.KR % IN_TN
    half = A_ROPE // 2
    assert KR_SRC // IN_TN == 1 and _Cols.KR // IN_TN == _Cols.TOTAL // IN_TN - 1

    @pl.when(pl.program_id(1) == 0)
    def _assemble():
        @pl.when(j == 0)
        def _():
            wp_ref[...] = w_ref[...].astype(BF16)

        @pl.when(j == 1)
        def _():
            wp_ref[:cut, :] = w_ref[:cut, :].astype(BF16)
            wp_ref[cut:IN_TN - A_ROPE, :] = w_ref[cut + A_ROPE:, :].astype(BF16)
            wp_ref[IN_TN - A_ROPE:, :] = wnext_ref[...].astype(BF16)

        @pl.when(jnp.logical_and(j > 1, j < last))
        def _():
            row = j * IN_TN + lax.broadcasted_iota(jnp.int32, (IN_TN, 1), 0)
            is_qb = jnp.logical_and(row >= _Cols.QKVB, row < _Cols.QKVB + B_HEADS * B_DIM)
            sc = jnp.where(is_qb, B_SCALE2, 1.0)
            wp_ref[:IN_TN - A_ROPE, :] = (w_ref[A_ROPE:, :] * sc[:IN_TN - A_ROPE]).astype(BF16)
            wp_ref[IN_TN - A_ROPE:, :] = (wnext_ref[...] * sc[IN_TN - A_ROPE:]).astype(BF16)

        @pl.when(j == last)
        def _():
            kr = wkr_ref[...]
            wp_ref[:tail, :] = w_ref[A_ROPE:A_ROPE + tail, :].astype(BF16)
            wp_ref[tail:tail + A_ROPE, :] = kr.astype(BF16)
            wp_ref[tail + A_ROPE:tail + A_ROPE + half, :] = (-kr[half:, :]).astype(BF16)
            wp_ref[tail + A_ROPE + half:tail + 2 * A_ROPE, :] = kr[:half, :].astype(BF16)
            wp_ref[tail + 2 * A_ROPE:, :] = jnp.zeros((IN_TN - tail - 2 * A_ROPE, wp_ref.shape[1]),
                                                      BF16)

    o_ref[...] = lax.dot_general(a_ref[...], wp_ref[...], _NT,
                                 preferred_element_type=F32).astype(o_ref.dtype)


def _in_proj(h, w_in_t, layer, tm):
    m, k = h.shape
    n_src = w_in_t.shape[1]
    rope_blocks = IN_TN // A_ROPE
    last_rope_block = n_src // A_ROPE - 1
    return pl.pallas_call(
        _in_proj_kernel,
        grid=(_Cols.TOTAL // IN_TN, m // tm),
        in_specs=[pl.BlockSpec((tm, k), lambda j, i: (i, 0)),
                  pl.BlockSpec((None, IN_TN, k), lambda j, i: (layer, j, 0),
                               pipeline_mode=pl.Buffered(1)),
                  pl.BlockSpec((None, A_ROPE, k),
                               lambda j, i: (layer, jnp.minimum((j + 1) * rope_blocks,
                                                                last_rope_block), 0)),
                  pl.BlockSpec((None, A_ROPE, k), lambda j, i: (layer, KR_SRC // A_ROPE, 0))],
        out_specs=pl.BlockSpec((tm, IN_TN), lambda j, i: (i, j)),
        out_shape=jax.ShapeDtypeStruct((m, _Cols.TOTAL), BF16),
        scratch_shapes=[pltpu.VMEM((IN_TN, k), BF16)],
        compiler_params=_cparams(("arbitrary", "arbitrary")),
        name="in_proj",
    )(h, w_in_t, w_in_t, w_in_t)


def _q_prep_kernel(c_ref, gl_ref, w_ref, gh_ref, tbl_ref, o_ref, *, scale):
    c = c_ref[...].astype(F32)
    cn = c * lax.rsqrt(jnp.mean(c * c, axis=-1, keepdims=True) + EPS) * gl_ref[...]
    cn = cn.astype(BF16)
    tbl = tbl_ref[...]
    g1 = gh_ref[:, :A_NOPE]
    g2t = gh_ref[:, A_NOPE:] * tbl
    lane = lax.broadcasted_iota(jnp.int32, tbl.shape, 1)
    for h in range(A_HEADS):
        acc = jnp.dot(cn, w_ref[:, h * A_PAD:(h + 1) * A_PAD], preferred_element_type=F32)
        v1 = acc[:, :A_NOPE]
        v2 = acc[:, A_NOPE:]
        ss = (jnp.sum(v1 * v1, axis=-1, keepdims=True)
              + jnp.sum(jnp.where(lane < A_ROPE, v2 * v2, 0.0), axis=-1, keepdims=True))
        rs = lax.rsqrt(ss * (1.0 / A_QK) + EPS) * scale
        o_ref[:, h * A_PAD:h * A_PAD + A_NOPE] = (v1 * g1 * rs).astype(o_ref.dtype)
        o_ref[:, h * A_PAD + A_NOPE:(h + 1) * A_PAD] = (v2 * g2t * rs).astype(o_ref.dtype)


def _q_prep(proj, gl_row, w_uq, gh_row, tbl, tm):
    m = proj.shape[0]
    n = A_HEADS * A_PAD
    return pl.pallas_call(
        functools.partial(_q_prep_kernel, scale=LOG2E / math.sqrt(A_QK)),
        grid=(m // tm,),
        in_specs=[pl.BlockSpec((tm, Q_LORA), lambda i: (i, _Cols.CQ // Q_LORA)),
                  pl.BlockSpec((1, Q_LORA), lambda i: (0, 0)),
                  pl.BlockSpec((Q_LORA, n), lambda i: (0, 0)),
                  pl.BlockSpec((1, A_PAD), lambda i: (0, 0)),
                  pl.BlockSpec((tm, LANE), lambda i: (i, 0))],
        out_specs=pl.BlockSpec((tm, n), lambda i: (i, 0)),
        out_shape=jax.ShapeDtypeStruct((m, n), BF16),
        compiler_params=_cparams(("parallel",)),
        name="q_prep",
    )(proj, gl_row, w_uq, gh_row, tbl)


def _kv_prep_kernel(c_ref, kr_ref, gl_ref, w_ref, gh_ref, tbl_ref, kt_ref, v_ref):
    c = c_ref[...].astype(F32)
    cn = c * lax.rsqrt(jnp.mean(c * c, axis=-1, keepdims=True) + EPS) * gl_ref[...]
    cn = cn.astype(BF16)
    kr = kr_ref[...].astype(F32)
    lane = lax.broadcasted_iota(jnp.int32, kr.shape, 1)
    ss_r = jnp.sum(jnp.where(lane < A_ROPE, kr * kr, 0.0), axis=-1, keepdims=True)
    ab = kr * gh_ref[:, A_NOPE:] * tbl_ref[...]
    kk = ab + pltpu.roll(ab, A_ROPE, axis=1)
    g1 = gh_ref[:, :A_NOPE]
    ones = jnp.ones((kr.shape[0], A_PAD - A_V), v_ref.dtype)
    for h in range(A_HEADS):
        acc = jnp.dot(cn, w_ref[:, h * A_PAD:(h + 1) * A_PAD], preferred_element_type=F32)
        kn = acc[:, :A_NOPE]
        ss = jnp.sum(kn * kn, axis=-1, keepdims=True) + ss_r
        rs = lax.rsqrt(ss * (1.0 / A_QK) + EPS)
        kt_ref[h * A_PAD:h * A_PAD + A_NOPE, :] = (kn * g1 * rs).T.astype(kt_ref.dtype)
        kt_ref[h * A_PAD + A_NOPE:(h + 1) * A_PAD, :] = (kk * rs).T.astype(kt_ref.dtype)
        v_ref[:, h * A_PAD:h * A_PAD + A_V] = acc[:, A_NOPE:].astype(v_ref.dtype)
        v_ref[:, h * A_PAD + A_V:(h + 1) * A_PAD] = ones


def _kv_prep(proj, gl_row, w_ukv, gh_row, tbl, tm):
    m = proj.shape[0]
    nk = A_HEADS * A_PAD
    nv = A_HEADS * A_PAD
    return pl.pallas_call(
        _kv_prep_kernel,
        grid=(m // tm,),
        in_specs=[pl.BlockSpec((tm, KV_LORA), lambda i: (i, _Cols.CKV // KV_LORA)),
                  pl.BlockSpec((tm, LANE), lambda i: (i, _Cols.KR // LANE)),
                  pl.BlockSpec((1, KV_LORA), lambda i: (0, 0)),
                  pl.BlockSpec((KV_LORA, nk), lambda i: (0, 0)),
                  pl.BlockSpec((1, A_PAD), lambda i: (0, 0)),
                  pl.BlockSpec((tm, LANE), lambda i: (i, 0))],
        out_specs=[pl.BlockSpec((nk, tm), lambda i: (0, i)),
                   pl.BlockSpec((tm, nv), lambda i: (i, 0))],
        out_shape=[jax.ShapeDtypeStruct((nk, m), BF16),
                   jax.ShapeDtypeStruct((m, nv), BF16)],
        compiler_params=_cparams(("parallel",)),
        name="kv_prep",
    )(proj, proj, gl_row, w_ukv, gh_row, tbl)


def _attn_a_flat_kernel(q_ref, kt_ref, v_ref, g_ref, o_ref, s_ref, p_ref, bias_ref, oacc_ref,
                        *, tq, nq):
    n_units = nq * (nq + 1) // 2

    def advance(unit):
        qb, u = unit
        done = u == qb
        return jnp.where(done, qb + 1, qb), jnp.where(done, 0, u + 1)

    def key_block(unit):
        qb, u = unit
        return jnp.where(u == 0, qb, u - 1)

    def scores(unit, slot):
        qb, u = unit
        q = q_ref[pl.ds(pl.multiple_of(qb * tq, tq), tq), :]
        kt = kt_ref[:, pl.ds(pl.multiple_of(key_block(unit) * tq, tq), tq)]
        own = jnp.where(u == 0, 1, 0)
        s_ref[slot] = jnp.dot(q, kt, preferred_element_type=F32) + bias_ref[own]

    def p_times_v(unit, slot):
        v = v_ref[pl.ds(pl.multiple_of(key_block(unit) * tq, tq), tq), :]
        return jnp.dot(p_ref[slot], v, preferred_element_type=F32)

    def softmax(unit, slot, m, acc):
        s = s_ref[slot]
        m = jnp.where(unit[1] == 0, NEG, m)
        m_new = jnp.maximum(m, jnp.max(s, axis=-1, keepdims=True))
        p_ref[slot] = jnp.exp2(s - m_new).astype(BF16)
        return m_new, jnp.exp2(m - m_new) * acc

    def step(slot, state):
        m, pend, u1, u2, u3 = state
        scores(u1, slot)
        acc = pend + p_times_v(u3, slot)
        oacc_ref[u3[0]] = acc
        m, pend = softmax(u2, 1 - slot, m, acc)
        return m, pend, advance(u1), u1, u2

    row_chunk = lax.broadcasted_iota(jnp.int32, (tq, tq), 0) // CHUNK
    key_chunk = lax.broadcasted_iota(jnp.int32, (tq, tq), 1) // CHUNK
    bias_ref[0] = jnp.zeros((tq, tq), F32)
    bias_ref[1] = jnp.where(key_chunk <= row_chunk, 0.0, NEG)
    p_ref[1] = jnp.zeros((tq, tq), BF16)
    zero = jnp.int32(0)
    first = (zero, zero)
    scores(first, 0)
    state = (jnp.full((tq, 1), NEG, F32), jnp.zeros((tq, A_PAD), F32), advance(first), first, first)

    def steps(count, state):
        for k in range(count):
            state = step((1 + k) % 2, state)
        return state

    n_loop = n_units - 1
    rest = n_loop % STEPS_PER_TRIP
    state = lax.fori_loop(0, n_loop // STEPS_PER_TRIP, lambda _, st: steps(STEPS_PER_TRIP, st),
                          state)
    state = lax.fori_loop(0, rest // 2, lambda _, st: steps(2, st), state)
    state = steps(rest % 2, state)
    m, pend, _, u2, u3 = state
    last_slot = (n_units - 1) % 2
    acc = pend + p_times_v(u3, 1 - last_slot)
    _, pend = softmax(u2, last_slot, m, acc)
    oacc_ref[u2[0]] = pend + p_times_v(u2, last_slot)
    for qb in range(nq):
        acc = oacc_ref[qb]
        gate = g_ref[qb * tq:(qb + 1) * tq, :].astype(F32)
        o = acc[:, :A_V] / acc[:, A_V:]
        o_ref[qb * tq:(qb + 1) * tq, :] = (o * _silu(gate)).astype(o_ref.dtype)


def _attn_a_flat(q_a, kt_a, v_a, proj, batch, seq, tq):
    m = q_a.shape[0]
    nq = seq // tq
    gate0 = _Cols.GA // A_V
    return pl.pallas_call(
        functools.partial(_attn_a_flat_kernel, tq=tq, nq=nq),
        grid=(batch, A_HEADS),
        in_specs=[pl.BlockSpec((seq, A_PAD), lambda b, h: (b, h)),
                  pl.BlockSpec((A_PAD, seq), lambda b, h: (h, b)),
                  pl.BlockSpec((seq, A_PAD), lambda b, h: (b, h)),
                  pl.BlockSpec((seq, A_V), lambda b, h: (b, gate0 + h))],
        out_specs=pl.BlockSpec((seq, A_V), lambda b, h: (b, h)),
        out_shape=jax.ShapeDtypeStruct((m, A_HEADS * A_V), BF16),
        scratch_shapes=[pltpu.VMEM((2, tq, tq), F32), pltpu.VMEM((2, tq, tq), BF16),
                        pltpu.VMEM((2, tq, tq), F32), pltpu.VMEM((nq, tq, A_PAD), F32)],
        compiler_params=_cparams(("parallel", "arbitrary")),
        name="attn_a",
    )(q_a, kt_a, v_a, proj)


def _attn_b_flat_kernel(q_ref, k_ref, v_ref, g_ref, o_ref,
                        z_ref, hl_ref, w_ref, bias_ref, oacc_ref, *, tq, ks, nq):
    per_block = tq // ks
    assert per_block == 2, "bias tables below cover the two own-block units"
    n_units = per_block * nq * (nq + 1) // 2
    r = lax.broadcasted_iota(jnp.int32, (ks, ks), 0)
    c = lax.broadcasted_iota(jnp.int32, (ks, ks), 1)
    tri = jnp.where(r >= c, 1.0, 0.0).astype(BF16)

    def advance(unit):
        qb, u = unit
        done = u == per_block * qb + per_block - 1
        return jnp.where(done, qb + 1, qb), jnp.where(done, 0, u + 1)

    def key_start(unit):
        qb, u = unit
        sub = jnp.clip(per_block * qb + per_block - 1 - u, 0, per_block * nq - 1)
        return pl.multiple_of(sub * ks, ks)

    def stage_scores(t, unit, zslot):
        qb, u = unit
        q = q_ref[pl.ds(pl.multiple_of(jnp.minimum(qb, nq - 1) * tq, tq), tq), :]
        z = lax.dot_general(q, k_ref[pl.ds(key_start(unit), ks), :], _NT,
                            preferred_element_type=F32)
        table = jnp.where(t < n_units, jnp.where(u < per_block, u + 1, 0), 3)
        z_ref[zslot] = z + bias_ref[table]

    def stage_logs(slot, zslot):
        z = z_ref[zslot]
        neg_abs = lax.bitcast_convert_type(
            lax.bitcast_convert_type(z, jnp.uint32) | jnp.uint32(0x80000000), F32)
        nlf = jnp.maximum(z, 0.0) + jnp.log(1.0 + jnp.exp2(neg_abs)) * LOG2E
        hl_ref[slot] = nlf.astype(BF16)

    def stage_weights(unit, slot, zslot, mass):
        mass = jnp.where(unit[1] == 0, 0.0, mass)
        tail = jnp.dot(hl_ref[slot], tri, preferred_element_type=F32)
        w_ref[slot] = jnp.exp2(z_ref[zslot] - tail - mass).astype(BF16)
        return mass + tail[:, :1]

    def stage_pv(unit, slot, acc):
        acc = jnp.where(unit[1] == 0, 0.0, acc)
        v = v_ref[pl.ds(key_start(unit), ks), :]
        acc = acc + jnp.dot(w_ref[slot], v, preferred_element_type=F32)
        oacc_ref[jnp.minimum(unit[0], nq - 1)] = acc
        return acc

    def step(k, state):
        t, mass, acc, u1, u2, u3, u4 = state
        slot = k % 2
        stage_scores(t, u1, k)
        acc = stage_pv(u4, 1 - slot, acc)
        mass = stage_weights(u3, slot, (k - 2) % 4, mass)
        stage_logs(1 - slot, (k - 1) % 4)
        return t + 1, mass, acc, advance(u1), u1, u2, u3

    def steps(count, state):
        for k in range(count):
            state = step(k % 4, state)
        return state

    d = (lax.broadcasted_iota(jnp.int32, (tq, ks), 0) - lax.broadcasted_iota(jnp.int32, (tq, ks), 1))
    bias_ref[0] = jnp.zeros((tq, ks), F32)
    bias_ref[1] = jnp.where(d > ks, 0.0, NEG)
    bias_ref[2] = jnp.where(d > 0, 0.0, NEG)
    bias_ref[3] = jnp.full((tq, ks), NEG, F32)
    z_ref[2] = jnp.full((tq, ks), NEG, F32)
    z_ref[3] = jnp.full((tq, ks), NEG, F32)
    hl_ref[0] = jnp.zeros((tq, ks), BF16)
    w_ref[1] = jnp.zeros((tq, ks), BF16)
    zero = jnp.int32(0)
    first = (zero, zero)
    state = (zero, jnp.zeros((tq, 1), F32), jnp.zeros((tq, B_DIM), F32), first, first, first, first)
    n_steps = n_units + 3
    rest = n_steps % STEPS_PER_TRIP
    state = lax.fori_loop(0, n_steps // STEPS_PER_TRIP, lambda _, st: steps(STEPS_PER_TRIP, st),
                          state)
    state = lax.fori_loop(0, rest // 4, lambda _, st: steps(4, st), state)
    steps(rest % 4, state)
    for qb in range(nq):
        gate = g_ref[qb * tq:(qb + 1) * tq, :].astype(F32)
        o_ref[qb * tq:(qb + 1) * tq, :] = (oacc_ref[qb] * _silu(gate)).astype(o_ref.dtype)


def _attn_b_flat(proj, batch, seq, tq, ks):
    m = proj.shape[0]
    nq = seq // tq
    q0 = _Cols.QKVB // B_DIM
    k0 = q0 + B_HEADS
    v0 = k0 + B_HEADS
    g0 = _Cols.GB // B_DIM
    return pl.pallas_call(
        functools.partial(_attn_b_flat_kernel, tq=tq, ks=ks, nq=nq),
        grid=(batch, B_HEADS),
        in_specs=[pl.BlockSpec((seq, B_DIM), lambda b, h: (b, q0 + h)),
                  pl.BlockSpec((seq, B_DIM), lambda b, h: (b, k0 + h)),
                  pl.BlockSpec((seq, B_DIM), lambda b, h: (b, v0 + h)),
                  pl.BlockSpec((seq, B_DIM), lambda b, h: (b, g0 + h))],
        out_specs=pl.BlockSpec((seq, B_DIM), lambda b, h: (b, h)),
        out_shape=jax.ShapeDtypeStruct((m, B_HEADS * B_DIM), BF16),
        scratch_shapes=[pltpu.VMEM((4, tq, ks), F32),
                        pltpu.VMEM((2, tq, ks), BF16), pltpu.VMEM((2, tq, ks), BF16),
                        pltpu.VMEM((4, tq, ks), F32), pltpu.VMEM((nq, tq, B_DIM), F32)],
        compiler_params=_cparams(("parallel", "arbitrary")),
        name="attn_b",
    )(proj, proj, proj, proj)


def _mem_prep_kernel(mem_ref, g_ref, w_ref, gk_ref, o_ref):
    j = pl.program_id(0)
    x = mem_ref[...]
    xn = (x * lax.rsqrt(jnp.mean(x * x, axis=-1, keepdims=True) + EPS) * g_ref[...]).astype(BF16)
    y = jnp.dot(xn, w_ref[...].astype(BF16), preferred_element_type=F32)
    yk = y * lax.rsqrt(jnp.mean(y * y, axis=-1, keepdims=True) + EPS) * gk_ref[...]
    o_ref[...] = jnp.where(j < M_HEADS, yk, y).astype(o_ref.dtype)


def _mem_prep(mem2d, g_row, w_mkv, layer, gk_row):
    mm, d = mem2d.shape
    n = 2 * M_HEADS * M_DIM
    return pl.pallas_call(
        _mem_prep_kernel,
        grid=(2 * M_HEADS,),
        in_specs=[pl.BlockSpec((mm, d), lambda j: (0, 0)),
                  pl.BlockSpec((1, d), lambda j: (0, 0)),
                  pl.BlockSpec((None, d, M_DIM), lambda j: (layer, 0, j)),
                  pl.BlockSpec((1, M_DIM), lambda j: (0, 0))],
        out_specs=pl.BlockSpec((mm, M_DIM), lambda j: (0, j)),
        out_shape=jax.ShapeDtypeStruct((mm, n), BF16),
        compiler_params=_cparams(("arbitrary",)),
        name="mem_prep",
    )(mem2d, g_row, w_mkv, gk_row)


def _attn_m_kernel(q_ref, k_ref, v_ref, g_ref, gq_ref, o_ref, *, scale):
    q = q_ref[...].astype(F32)
    qn = q * (lax.rsqrt(jnp.mean(q * q, axis=-1, keepdims=True) + EPS) * scale) * gq_ref[...]
    s = lax.dot_general(qn.astype(BF16), k_ref[...], _NT, preferred_element_type=F32)
    p = jnp.exp(s - jnp.max(s, axis=-1, keepdims=True))
    l = jnp.sum(p, axis=-1, keepdims=True)
    o = jnp.dot(p.astype(BF16), v_ref[...], preferred_element_type=F32) / l
    o_ref[...] = (o * _silu(g_ref[...].astype(F32))).astype(o_ref.dtype)


def _attn_m(proj, kv_m, gq_row, seq, mem_len, tm):
    m = proj.shape[0]
    nb = seq // tm
    q0 = _Cols.QM // M_DIM
    g0 = _Cols.GM // M_DIM
    assert mem_len == M_DIM, "memory block spec assumes MEM_LEN == M_DIM rows per batch"
    return pl.pallas_call(
        functools.partial(_attn_m_kernel, scale=1.0 / math.sqrt(M_DIM)),
        grid=(m // tm, M_HEADS),
        in_specs=[pl.BlockSpec((tm, M_DIM), lambda i, h: (i, q0 + h)),
                  pl.BlockSpec((mem_len, M_DIM), lambda i, h: (i // nb, h)),
                  pl.BlockSpec((mem_len, M_DIM), lambda i, h: (i // nb, M_HEADS + h)),
                  pl.BlockSpec((tm, M_DIM), lambda i, h: (i, g0 + h)),
                  pl.BlockSpec((1, M_DIM), lambda i, h: (0, 0))],
        out_specs=pl.BlockSpec((tm, M_DIM), lambda i, h: (i, h)),
        out_shape=jax.ShapeDtypeStruct((m, M_HEADS * M_DIM), BF16),
        compiler_params=_cparams(("parallel", "arbitrary")),
        name="attn_m",
    )(proj, kv_m, kv_m, proj, gq_row)


def _mix_kernel(ua_ref, ub_ref, um_ref, wa_ref, wb_ref, wm_ref, ra_ref, rb_ref, rm_ref, o_ref,
                wa_bf, wb_bf, wm_bf):
    @pl.when(pl.program_id(1) == 0)
    def _cast():
        wa_bf[...] = wa_ref[...].astype(BF16)
        wb_bf[...] = wb_ref[...].astype(BF16)
        wm_bf[...] = wm_ref[...].astype(BF16)

    rows = o_ref.shape[0] // MIX_ROW_CHUNKS
    for c in range(MIX_ROW_CHUNKS):
        rs = slice(c * rows, (c + 1) * rows)
        ya = jnp.dot(ua_ref[rs, :], wa_bf[...], preferred_element_type=F32)
        yb = jnp.dot(ub_ref[rs, :], wb_bf[...], preferred_element_type=F32)
        ym = jnp.dot(um_ref[rs, :], wm_bf[...], preferred_element_type=F32)
        mixed = (_sigmoid(ra_ref[rs, :].astype(F32)) * ya + _sigmoid(rb_ref[rs, :].astype(F32)) * yb
                 + _sigmoid(rm_ref[rs, :].astype(F32)) * ym)
        o_ref[rs, :] = mixed.astype(o_ref.dtype)


def _mix(u_a, u_b, u_m, w_pa, w_pb, w_pm, layer, proj, tm, tn):
    m = u_a.shape[0]
    r0 = _Cols.MERGE // tn
    rstep = D_MODEL // tn

    def rspec(branch):
        return pl.BlockSpec((tm, tn), lambda j, i: (i, r0 + branch * rstep + j))

    def wspec(w):
        return pl.BlockSpec((None, w.shape[1], tn), lambda j, i: (layer, 0, j))

    return pl.pallas_call(
        _mix_kernel,
        grid=(D_MODEL // tn, m // tm),
        in_specs=[pl.BlockSpec((tm, u_a.shape[1]), lambda j, i: (i, 0)),
                  pl.BlockSpec((tm, u_b.shape[1]), lambda j, i: (i, 0)),
                  pl.BlockSpec((tm, u_m.shape[1]), lambda j, i: (i, 0)),
                  wspec(w_pa), wspec(w_pb), wspec(w_pm),
                  rspec(0), rspec(1), rspec(2)],
        out_specs=pl.BlockSpec((tm, tn), lambda j, i: (i, j)),
        out_shape=jax.ShapeDtypeStruct((m, D_MODEL), BF16),
        scratch_shapes=[pltpu.VMEM((w_pa.shape[1], tn), BF16), pltpu.VMEM((w_pb.shape[1], tn), BF16),
                        pltpu.VMEM((w_pm.shape[1], tn), BF16)],
        compiler_params=_cparams(("arbitrary", "arbitrary")),
        name="mix",
    )(u_a, u_b, u_m, w_pa, w_pb, w_pm, proj, proj, proj)


def _out_kernel(a_ref, w_ref, x_ref, o_ref, w_bf):
    @pl.when(pl.program_id(1) == 0)
    def _cast():
        w_bf[...] = w_ref[...].astype(BF16)

    o_ref[...] = x_ref[...] + jnp.dot(a_ref[...], w_bf[...], preferred_element_type=F32)


def _out_proj(mixed, w_out, layer, x, tm, tn):
    m, k = mixed.shape
    n = w_out.shape[2]
    return pl.pallas_call(
        _out_kernel,
        grid=(n // tn, m // tm),
        in_specs=[pl.BlockSpec((tm, k), lambda j, i: (i, 0)),
                  pl.BlockSpec((None, k, tn), lambda j, i: (layer, 0, j)),
                  pl.BlockSpec((tm, tn), lambda j, i: (i, j))],
        out_specs=pl.BlockSpec((tm, tn), lambda j, i: (i, j)),
        out_shape=jax.ShapeDtypeStruct((m, n), F32),
        scratch_shapes=[pltpu.VMEM((k, tn), BF16)],
        compiler_params=_cparams(("arbitrary", "arbitrary")),
        name="out_proj",
    )(mixed, w_out, x)


def _rotate_half_cols(w):
    half = A_ROPE // 2
    return jnp.concatenate([-w[..., half:], w[..., :half]], axis=-1)


def _rope_gain(g_rope):
    half = A_ROPE // 2
    return jnp.concatenate([g_rope, g_rope[half:], g_rope[:half]])


def _prep_w_uq(w_uq):
    w = w_uq.reshape(Q_LORA, A_HEADS, A_QK)
    rope = w[..., A_NOPE:]
    w = jnp.concatenate([w[..., :A_NOPE], rope, _rotate_half_cols(rope)], axis=-1)
    return w.reshape(Q_LORA, A_HEADS * A_PAD).astype(BF16)


def _head_gain(g):
    return jnp.concatenate([g[:A_NOPE], _rope_gain(g[A_NOPE:])])[None, :]


def _tiles(m, seq):
    return dict(
        row=min(512, m),
        mm_m=min(1024, m),
        mix_n=512,
        attn=min(512, seq // 2),
        b_keys=min(256, seq // 4),
        m_rows=min(1024, seq),
    )


def kernel(x, mem, positions, g_pre, w_in, g_q_lat, w_uq, g_kv_lat, w_ukv, g_qn_a, g_kn_a,
           w_pa, w_pb, g_mem, w_mkv, g_qn_m, g_kn_m, w_pm, w_out):
    batch, seq, d = x.shape
    mem_len = mem.shape[1]
    depth = w_in.shape[0]
    m = batch * seq
    t = _tiles(m, seq)

    half = A_ROPE // 2
    freqs = ROPE_THETA ** (-jnp.arange(half, dtype=F32) / half)
    freq_row = jnp.tile(freqs, LANE // half)[None, :]
    pos_b = jnp.broadcast_to(positions.reshape(m, 1).astype(F32), (m, LANE))
    tbl = _rope_table(pos_b, freq_row, t["row"])

    xs = x.reshape(m, d)
    mem2d = mem.reshape(batch * mem_len, d)
    w_in_t = jnp.swapaxes(w_in, 1, 2)
    for l in range(depth):
        h = _norm(xs, g_pre[l][None, :], t["row"])
        proj = _in_proj(h, w_in_t, l, t["mm_m"])

        q_a = _q_prep(proj, g_q_lat[l][None, :], _prep_w_uq(w_uq[l]), _head_gain(g_qn_a[l]),
                      tbl, t["row"])
        k_a, v_a = _kv_prep(proj, g_kv_lat[l][None, :], w_ukv[l].astype(BF16),
                            _head_gain(g_kn_a[l]), tbl, t["row"])
        u_a = _attn_a_flat(q_a, k_a, v_a, proj, batch, seq, t["attn"])
        u_b = _attn_b_flat(proj, batch, seq, t["attn"], t["b_keys"])

        kv_m = _mem_prep(mem2d, g_mem[l][None, :], w_mkv, l, g_kn_m[l][None, :])
        u_m = _attn_m(proj, kv_m, g_qn_m[l][None, :], seq, mem_len, t["m_rows"])

        mixed = _mix(u_a, u_b, u_m, w_pa, w_pb, w_pm, l, proj, t["mm_m"], t["mix_n"])
        xs = _out_proj(mixed, w_out, l, xs, t["mm_m"], t["mix_n"])
    return xs.reshape(batch, seq, d)
```

```python
import functools
import math

import jax
import jax.numpy as jnp
from jax import lax
from jax.experimental import pallas as pl
from jax.experimental.pallas import tpu as pltpu

F32 = jnp.float32
BF16 = jnp.bfloat16

EPS = 1e-6
LOG2E = 1.4426950408889634
NEG = -1e30
MIX_ROW_CHUNKS = 4
STEPS_PER_TRIP = 16
A_STEPS_PER_TRIP = 32
ROPE_THETA = 10000.0
CHUNK = 64

A_HEADS, A_NOPE, A_ROPE, A_V = 16, 128, 64, 128
A_QK = A_NOPE + A_ROPE
A_PAD = 256
Q_LORA, KV_LORA = 1024, 512
B_HEADS, B_DIM = 8, 128
M_HEADS, M_DIM = 4, 256
D_MODEL = 4096
N_BRANCH = 3

LANE = 128
VMEM_LIMIT = 56 * 1024 * 1024


class _Cols:
    CQ = 0
    CKV = CQ + Q_LORA
    GA = CKV + KV_LORA
    QKVB = GA + A_HEADS * A_V
    GB = QKVB + 3 * B_HEADS * B_DIM
    QM = GB + B_HEADS * B_DIM
    GM = QM + M_HEADS * M_DIM
    MERGE = GM + M_HEADS * M_DIM
    KR = MERGE + N_BRANCH * D_MODEL
    USED = KR + LANE
    TOTAL = 22528


def _cparams(sem):
    return pltpu.CompilerParams(dimension_semantics=sem, vmem_limit_bytes=VMEM_LIMIT)


def _sigmoid(v):
    return 1.0 / (1.0 + jnp.exp(-v))


def _silu(v):
    return v * _sigmoid(v)


_NT = (((1,), (1,)), ((), ()))


def _rope_table_kernel(pos_ref, freq_ref, o_ref):
    ang = pos_ref[...] * freq_ref[...]
    lane = lax.broadcasted_iota(jnp.int32, ang.shape, 1)
    o_ref[...] = jnp.where(lane < A_ROPE, jnp.cos(ang), jnp.sin(ang))


def _rope_table(pos_b, freq_row, tm):
    m = pos_b.shape[0]
    return pl.pallas_call(
        _rope_table_kernel,
        grid=(m // tm,),
        in_specs=[pl.BlockSpec((tm, LANE), lambda i: (i, 0)),
                  pl.BlockSpec((1, LANE), lambda i: (0, 0))],
        out_specs=pl.BlockSpec((tm, LANE), lambda i: (i, 0)),
        out_shape=jax.ShapeDtypeStruct((m, LANE), F32),
        compiler_params=_cparams(("parallel",)),
        name="rope_table",
    )(pos_b, freq_row)


def _norm_kernel(x_ref, g_ref, o_ref):
    x = x_ref[...]
    y = x * lax.rsqrt(jnp.mean(x * x, axis=-1, keepdims=True) + EPS)
    o_ref[...] = (y * g_ref[...]).astype(o_ref.dtype)


def _norm(x, g_row, tm):
    m, d = x.shape
    return pl.pallas_call(
        _norm_kernel,
        grid=(m // tm,),
        in_specs=[pl.BlockSpec((tm, d), lambda i: (i, 0)),
                  pl.BlockSpec((1, d), lambda i: (0, 0))],
        out_specs=pl.BlockSpec((tm, d), lambda i: (i, 0)),
        out_shape=jax.ShapeDtypeStruct((m, d), BF16),
        compiler_params=_cparams(("parallel",)),
        name="norm",
    )(x, g_row)


IN_TN = 1024
B_SCALE2 = LOG2E / math.sqrt(B_DIM)
assert 2 * IN_TN <= _Cols.QKVB and _Cols.QKVB + B_HEADS * B_DIM <= _Cols.KR - _Cols.KR % IN_TN
KR_SRC = Q_LORA + KV_LORA


def _in_proj_kernel(a_ref, w_ref, wnext_ref, wkr_ref, o_ref, wp_ref):
    j = pl.program_id(0)
    last = pl.num_programs(0) - 1
    cut = KR_SRC % IN_TN
    tail = _Cols.KR % IN_TN
    half = A_ROPE // 2
    assert KR_SRC // IN_TN == 1 and _Cols.KR // IN_TN == _Cols.TOTAL // IN_TN - 1

    @pl.when(pl.program_id(1) == 0)
    def _assemble():
        @pl.when(j == 0)
        def _():
            wp_ref[...] = w_ref[...].astype(BF16)

        @pl.when(j == 1)
        def _():
            wp_ref[:cut, :] = w_ref[:cut, :].astype(BF16)
            wp_ref[cut:IN_TN - A_ROPE, :] = w_ref[cut + A_ROPE:, :].astype(BF16)
            wp_ref[IN_TN - A_ROPE:, :] = wnext_ref[...].astype(BF16)

        @pl.when(jnp.logical_and(j > 1, j < last))
        def _():
            row = j * IN_TN + lax.broadcasted_iota(jnp.int32, (IN_TN, 1), 0)
            is_qb = jnp.logical_and(row >= _Cols.QKVB, row < _Cols.QKVB + B_HEADS * B_DIM)
            sc = jnp.where(is_qb, B_SCALE2, 1.0)
            wp_ref[:IN_TN - A_ROPE, :] = (w_ref[A_ROPE:, :] * sc[:IN_TN - A_ROPE]).astype(BF16)
            wp_ref[IN_TN - A_ROPE:, :] = (wnext_ref[...] * sc[IN_TN - A_ROPE:]).astype(BF16)

        @pl.when(j == last)
        def _():
            kr = wkr_ref[...]
            wp_ref[:tail, :] = w_ref[A_ROPE:A_ROPE + tail, :].astype(BF16)
            wp_ref[tail:tail + A_ROPE, :] = kr.astype(BF16)
            wp_ref[tail + A_ROPE:tail + A_ROPE + half, :] = (-kr[half:, :]).astype(BF16)
            wp_ref[tail + A_ROPE + half:tail + 2 * A_ROPE, :] = kr[:half, :].astype(BF16)
            wp_ref[tail + 2 * A_ROPE:, :] = jnp.zeros((IN_TN - tail - 2 * A_ROPE, wp_ref.shape[1]),
                                                      BF16)

    o_ref[...] = lax.dot_general(a_ref[...], wp_ref[...], _NT,
                                 preferred_element_type=F32).astype(o_ref.dtype)


def _in_proj(h, w_in_t, layer, tm):
    m, k = h.shape
    n_src = w_in_t.shape[1]
    rope_blocks = IN_TN // A_ROPE
    last_rope_block = n_src // A_ROPE - 1
    return pl.pallas_call(
        _in_proj_kernel,
        grid=(_Cols.TOTAL // IN_TN, m // tm),
        in_specs=[pl.BlockSpec((tm, k), lambda j, i: (i, 0)),
                  pl.BlockSpec((None, IN_TN, k), lambda j, i: (layer, j, 0),
                               pipeline_mode=pl.Buffered(1)),
                  pl.BlockSpec((None, A_ROPE, k),
                               lambda j, i: (layer, jnp.minimum((j + 1) * rope_blocks,
                                                                last_rope_block), 0)),
                  pl.BlockSpec((None, A_ROPE, k), lambda j, i: (layer, KR_SRC // A_ROPE, 0))],
        out_specs=pl.BlockSpec((tm, IN_TN), lambda j, i: (i, j)),
        out_shape=jax.ShapeDtypeStruct((m, _Cols.TOTAL), BF16),
        scratch_shapes=[pltpu.VMEM((IN_TN, k), BF16)],
        compiler_params=_cparams(("arbitrary", "arbitrary")),
        name="in_proj",
    )(h, w_in_t, w_in_t, w_in_t)


def _q_prep_kernel(c_ref, gl_ref, w_ref, gh_ref, tbl_ref, o_ref, *, scale):
    c = c_ref[...].astype(F32)
    cn = c * lax.rsqrt(jnp.mean(c * c, axis=-1, keepdims=True) + EPS) * gl_ref[...]
    cn = cn.astype(BF16)
    tbl = tbl_ref[...]
    g1 = gh_ref[:, :A_NOPE]
    g2t = gh_ref[:, A_NOPE:] * tbl
    lane = lax.broadcasted_iota(jnp.int32, tbl.shape, 1)
    for h in range(A_HEADS):
        acc = jnp.dot(cn, w_ref[:, h * A_PAD:(h + 1) * A_PAD], preferred_element_type=F32)
        v1 = acc[:, :A_NOPE]
        v2 = acc[:, A_NOPE:]
        ss = (jnp.sum(v1 * v1, axis=-1, keepdims=True)
              + jnp.sum(jnp.where(lane < A_ROPE, v2 * v2, 0.0), axis=-1, keepdims=True))
        rs = lax.rsqrt(ss * (1.0 / A_QK) + EPS) * scale
        o_ref[:, h * A_PAD:h * A_PAD + A_NOPE] = (v1 * g1 * rs).astype(o_ref.dtype)
        o_ref[:, h * A_PAD + A_NOPE:(h + 1) * A_PAD] = (v2 * g2t * rs).astype(o_ref.dtype)


def _q_prep(proj, gl_row, w_uq, gh_row, tbl, tm):
    m = proj.shape[0]
    n = A_HEADS * A_PAD
    return pl.pallas_call(
        functools.partial(_q_prep_kernel, scale=LOG2E / math.sqrt(A_QK)),
        grid=(m // tm,),
        in_specs=[pl.BlockSpec((tm, Q_LORA), lambda i: (i, _Cols.CQ // Q_LORA)),
                  pl.BlockSpec((1, Q_LORA), lambda i: (0, 0)),
                  pl.BlockSpec((Q_LORA, n), lambda i: (0, 0)),
                  pl.BlockSpec((1, A_PAD), lambda i: (0, 0)),
                  pl.BlockSpec((tm, LANE), lambda i: (i, 0))],
        out_specs=pl.BlockSpec((tm, n), lambda i: (i, 0)),
        out_shape=jax.ShapeDtypeStruct((m, n), BF16),
        compiler_params=_cparams(("parallel",)),
        name="q_prep",
    )(proj, gl_row, w_uq, gh_row, tbl)


def _kv_prep_kernel(c_ref, kr_ref, gl_ref, w_ref, gh_ref, tbl_ref, kt_ref, v_ref):
    c = c_ref[...].astype(F32)
    cn = c * lax.rsqrt(jnp.mean(c * c, axis=-1, keepdims=True) + EPS) * gl_ref[...]
    cn = cn.astype(BF16)
    kr = kr_ref[...].astype(F32)
    lane = lax.broadcasted_iota(jnp.int32, kr.shape, 1)
    ss_r = jnp.sum(jnp.where(lane < A_ROPE, kr * kr, 0.0), axis=-1, keepdims=True)
    ab = kr * gh_ref[:, A_NOPE:] * tbl_ref[...]
    kk = ab + pltpu.roll(ab, A_ROPE, axis=1)
    g1 = gh_ref[:, :A_NOPE]
    ones = jnp.ones((kr.shape[0], A_PAD - A_V), v_ref.dtype)
    for h in range(A_HEADS):
        acc = jnp.dot(cn, w_ref[:, h * A_PAD:(h + 1) * A_PAD], preferred_element_type=F32)
        kn = acc[:, :A_NOPE]
        ss = jnp.sum(kn * kn, axis=-1, keepdims=True) + ss_r
        rs = lax.rsqrt(ss * (1.0 / A_QK) + EPS)
        kt_ref[h * A_PAD:h * A_PAD + A_NOPE, :] = (kn * g1 * rs).T.astype(kt_ref.dtype)
        kt_ref[h * A_PAD + A_NOPE:(h + 1) * A_PAD, :] = (kk * rs).T.astype(kt_ref.dtype)
        v_ref[:, h * A_PAD:h * A_PAD + A_V] = acc[:, A_NOPE:].astype(v_ref.dtype)
        v_ref[:, h * A_PAD + A_V:(h + 1) * A_PAD] = ones


def _kv_prep(proj, gl_row, w_ukv, gh_row, tbl, tm):
    m = proj.shape[0]
    nk = A_HEADS * A_PAD
    nv = A_HEADS * A_PAD
    return pl.pallas_call(
        _kv_prep_kernel,
        grid=(m // tm,),
        in_specs=[pl.BlockSpec((tm, KV_LORA), lambda i: (i, _Cols.CKV // KV_LORA)),
                  pl.BlockSpec((tm, LANE), lambda i: (i, _Cols.KR // LANE)),
                  pl.BlockSpec((1, KV_LORA), lambda i: (0, 0)),
                  pl.BlockSpec((KV_LORA, nk), lambda i: (0, 0)),
                  pl.BlockSpec((1, A_PAD), lambda i: (0, 0)),
                  pl.BlockSpec((tm, LANE), lambda i: (i, 0))],
        out_specs=[pl.BlockSpec((nk, tm), lambda i: (0, i)),
                   pl.BlockSpec((tm, nv), lambda i: (i, 0))],
        out_shape=[jax.ShapeDtypeStruct((nk, m), BF16),
                   jax.ShapeDtypeStruct((m, nv), BF16)],
        compiler_params=_cparams(("parallel",)),
        name="kv_prep",
    )(proj, proj, gl_row, w_ukv, gh_row, tbl)


def _attn_a_flat_kernel(q_ref, kt_ref, v_ref, g_ref, o_ref, s_ref, p_ref, bias_ref, oacc_ref,
                        *, tq, nq):
    n_units = nq * (nq + 1) // 2

    def advance(unit):
        qb, u = unit
        done = u == qb
        return jnp.where(done, qb + 1, qb), jnp.where(done, 0, u + 1)

    def key_block(unit):
        qb, u = unit
        return jnp.where(u == 0, qb, u - 1)

    def scores(unit, slot):
        qb, u = unit
        q = q_ref[pl.ds(pl.multiple_of(qb * tq, tq), tq), :]
        kt = kt_ref[:, pl.ds(pl.multiple_of(key_block(unit) * tq, tq), tq)]
        own = jnp.where(u == 0, 1, 0)
        s_ref[slot] = jnp.dot(q, kt, preferred_element_type=F32) + bias_ref[own]

    def p_times_v(unit, slot):
        v = v_ref[pl.ds(pl.multiple_of(key_block(unit) * tq, tq), tq), :]
        return jnp.dot(p_ref[slot], v, preferred_element_type=F32)

    def softmax(unit, slot, m, acc):
        s = s_ref[slot]
        m = jnp.where(unit[1] == 0, NEG, m)
        m_new = jnp.maximum(m, jnp.max(s, axis=-1, keepdims=True))
        p_ref[slot] = jnp.exp2(s - m_new).astype(BF16)
        return m_new, jnp.exp2(m - m_new) * acc

    def step(slot, state):
        m, pend, u1, u2, u3 = state
        scores(u1, slot)
        acc = pend + p_times_v(u3, slot)
        oacc_ref[u3[0]] = acc
        m, pend = softmax(u2, 1 - slot, m, acc)
        return m, pend, advance(u1), u1, u2

    row_chunk = lax.broadcasted_iota(jnp.int32, (tq, tq), 0) // CHUNK
    key_chunk = lax.broadcasted_iota(jnp.int32, (tq, tq), 1) // CHUNK
    bias_ref[0] = jnp.zeros((tq, tq), F32)
    bias_ref[1] = jnp.where(key_chunk <= row_chunk, 0.0, NEG)
    p_ref[1] = jnp.zeros((tq, tq), BF16)
    zero = jnp.int32(0)
    first = (zero, zero)
    scores(first, 0)
    state = (jnp.full((tq, 1), NEG, F32), jnp.zeros((tq, A_PAD), F32), advance(first), first, first)

    def steps(count, state):
        for k in range(count):
            state = step((1 + k) % 2, state)
        return state

    n_loop = n_units - 1
    rest = n_loop % A_STEPS_PER_TRIP
    state = lax.fori_loop(0, n_loop // A_STEPS_PER_TRIP, lambda _, st: steps(A_STEPS_PER_TRIP, st),
                          state)
    state = lax.fori_loop(0, rest // 2, lambda _, st: steps(2, st), state)
    state = steps(rest % 2, state)
    m, pend, _, u2, u3 = state
    last_slot = (n_units - 1) % 2
    acc = pend + p_times_v(u3, 1 - last_slot)
    _, pend = softmax(u2, last_slot, m, acc)
    oacc_ref[u2[0]] = pend + p_times_v(u2, last_slot)
    for qb in range(nq):
        acc = oacc_ref[qb]
        gate = g_ref[qb * tq:(qb + 1) * tq, :].astype(F32)
        o = acc[:, :A_V] / acc[:, A_V:]
        o_ref[qb * tq:(qb + 1) * tq, :] = (o * _silu(gate)).astype(o_ref.dtype)


def _attn_a_flat(q_a, kt_a, v_a, proj, batch, seq, tq):
    m = q_a.shape[0]
    nq = seq // tq
    gate0 = _Cols.GA // A_V
    return pl.pallas_call(
        functools.partial(_attn_a_flat_kernel, tq=tq, nq=nq),
        grid=(batch, A_HEADS),
        in_specs=[pl.BlockSpec((seq, A_PAD), lambda b, h: (b, h)),
                  pl.BlockSpec((A_PAD, seq), lambda b, h: (h, b)),
                  pl.BlockSpec((seq, A_PAD), lambda b, h: (b, h)),
                  pl.BlockSpec((seq, A_V), lambda b, h: (b, gate0 + h))],
        out_specs=pl.BlockSpec((seq, A_V), lambda b, h: (b, h)),
        out_shape=jax.ShapeDtypeStruct((m, A_HEADS * A_V), BF16),
        scratch_shapes=[pltpu.VMEM((2, tq, tq), F32), pltpu.VMEM((2, tq, tq), BF16),
                        pltpu.VMEM((2, tq, tq), F32), pltpu.VMEM((nq, tq, A_PAD), F32)],
        compiler_params=_cparams(("parallel", "arbitrary")),
        name="attn_a",
    )(q_a, kt_a, v_a, proj)


def _attn_b_flat_kernel(q_ref, k_ref, v_ref, g_ref, o_ref,
                        z_ref, hl_ref, w_ref, bias_ref, oacc_ref, *, tq, ks, nq):
    per_block = tq // ks
    assert per_block == 2, "bias tables below cover the two own-block units"
    n_units = per_block * nq * (nq + 1) // 2
    r = lax.broadcasted_iota(jnp.int32, (ks, ks), 0)
    c = lax.broadcasted_iota(jnp.int32, (ks, ks), 1)
    tri = jnp.where(r >= c, 1.0, 0.0).astype(BF16)

    def advance(unit):
        qb, u = unit
        done = u == per_block * qb + per_block - 1
        return jnp.where(done, qb + 1, qb), jnp.where(done, 0, u + 1)

    def key_start(unit):
        qb, u = unit
        sub = jnp.clip(per_block * qb + per_block - 1 - u, 0, per_block * nq - 1)
        return pl.multiple_of(sub * ks, ks)

    def stage_scores(t, unit, zslot):
        qb, u = unit
        q = q_ref[pl.ds(pl.multiple_of(jnp.minimum(qb, nq - 1) * tq, tq), tq), :]
        z = lax.dot_general(q, k_ref[pl.ds(key_start(unit), ks), :], _NT,
                            preferred_element_type=F32)
        table = jnp.where(t < n_units, jnp.where(u < per_block, u + 1, 0), 3)
        z_ref[zslot] = z + bias_ref[table]

    def stage_logs(slot, zslot):
        z = z_ref[zslot]
        neg_abs = lax.bitcast_convert_type(
            lax.bitcast_convert_type(z, jnp.uint32) | jnp.uint32(0x80000000), F32)
        nlf = jnp.maximum(z, 0.0) + jnp.log(1.0 + jnp.exp2(neg_abs)) * LOG2E
        hl_ref[slot] = nlf.astype(BF16)

    def stage_weights(unit, slot, zslot, mass):
        mass = jnp.where(unit[1] == 0, 0.0, mass)
        tail = jnp.dot(hl_ref[slot], tri, preferred_element_type=F32)
        w_ref[slot] = jnp.exp2(z_ref[zslot] - tail - mass).astype(BF16)
        return mass + tail[:, :1]

    def stage_pv(unit, slot, acc):
        acc = jnp.where(unit[1] == 0, 0.0, acc)
        v = v_ref[pl.ds(key_start(unit), ks), :]
        acc = acc + jnp.dot(w_ref[slot], v, preferred_element_type=F32)
        oacc_ref[jnp.minimum(unit[0], nq - 1)] = acc
        return acc

    def step(k, state):
        t, mass, acc, u1, u2, u3, u4 = state
        slot = k % 2
        stage_scores(t, u1, k)
        acc = stage_pv(u4, 1 - slot, acc)
        mass = stage_weights(u3, slot, (k - 2) % 4, mass)
        stage_logs(1 - slot, (k - 1) % 4)
        return t + 1, mass, acc, advance(u1), u1, u2, u3

    def steps(count, state):
        for k in range(count):
            state = step(k % 4, state)
        return state

    d = (lax.broadcasted_iota(jnp.int32, (tq, ks), 0) - lax.broadcasted_iota(jnp.int32, (tq, ks), 1))
    bias_ref[0] = jnp.zeros((tq, ks), F32)
    bias_ref[1] = jnp.where(d > ks, 0.0, NEG)
    bias_ref[2] = jnp.where(d > 0, 0.0, NEG)
    bias_ref[3] = jnp.full((tq, ks), NEG, F32)
    z_ref[2] = jnp.full((tq, ks), NEG, F32)
    z_ref[3] = jnp.full((tq, ks), NEG, F32)
    hl_ref[0] = jnp.zeros((tq, ks), BF16)
    w_ref[1] = jnp.zeros((tq, ks), BF16)
    zero = jnp.int32(0)
    first = (zero, zero)
    state = (zero, jnp.zeros((tq, 1), F32), jnp.zeros((tq, B_DIM), F32), first, first, first, first)
    n_steps = n_units + 3
    rest = n_steps % STEPS_PER_TRIP
    state = lax.fori_loop(0, n_steps // STEPS_PER_TRIP, lambda _, st: steps(STEPS_PER_TRIP, st),
                          state)
    state = lax.fori_loop(0, rest // 4, lambda _, st: steps(4, st), state)
    steps(rest % 4, state)
    for qb in range(nq):
        gate = g_ref[qb * tq:(qb + 1) * tq, :].astype(F32)
        o_ref[qb * tq:(qb + 1) * tq, :] = (oacc_ref[qb] * _silu(gate)).astype(o_ref.dtype)


def _attn_b_flat(proj, batch, seq, tq, ks):
    m = proj.shape[0]
    nq = seq // tq
    q0 = _Cols.QKVB // B_DIM
    k0 = q0 + B_HEADS
    v0 = k0 + B_HEADS
    g0 = _Cols.GB // B_DIM
    return pl.pallas_call(
        functools.partial(_attn_b_flat_kernel, tq=tq, ks=ks, nq=nq),
        grid=(batch, B_HEADS),
        in_specs=[pl.BlockSpec((seq, B_DIM), lambda b, h: (b, q0 + h)),
                  pl.BlockSpec((seq, B_DIM), lambda b, h: (b, k0 + h)),
                  pl.BlockSpec((seq, B_DIM), lambda b, h: (b, v0 + h)),
                  pl.BlockSpec((seq, B_DIM), lambda b, h: (b, g0 + h))],
        out_specs=pl.BlockSpec((seq, B_DIM), lambda b, h: (b, h)),
        out_shape=jax.ShapeDtypeStruct((m, B_HEADS * B_DIM), BF16),
        scratch_shapes=[pltpu.VMEM((4, tq, ks), F32),
                        pltpu.VMEM((2, tq, ks), BF16), pltpu.VMEM((2, tq, ks), BF16),
                        pltpu.VMEM((4, tq, ks), F32), pltpu.VMEM((nq, tq, B_DIM), F32)],
        compiler_params=_cparams(("parallel", "arbitrary")),
        name="attn_b",
    )(proj, proj, proj, proj)


def _mem_prep_kernel(mem_ref, g_ref, w_ref, gk_ref, o_ref):
    j = pl.program_id(0)
    x = mem_ref[...]
    xn = (x * lax.rsqrt(jnp.mean(x * x, axis=-1, keepdims=True) + EPS) * g_ref[...]).astype(BF16)
    y = jnp.dot(xn, w_ref[...].astype(BF16), preferred_element_type=F32)
    yk = y * lax.rsqrt(jnp.mean(y * y, axis=-1, keepdims=True) + EPS) * gk_ref[...]
    o_ref[...] = jnp.where(j < M_HEADS, yk, y).astype(o_ref.dtype)


def _mem_prep(mem2d, g_row, w_mkv, layer, gk_row):
    mm, d = mem2d.shape
    n = 2 * M_HEADS * M_DIM
    return pl.pallas_call(
        _mem_prep_kernel,
        grid=(2 * M_HEADS,),
        in_specs=[pl.BlockSpec((mm, d), lambda j: (0, 0)),
                  pl.BlockSpec((1, d), lambda j: (0, 0)),
                  pl.BlockSpec((None, d, M_DIM), lambda j: (layer, 0, j)),
                  pl.BlockSpec((1, M_DIM), lambda j: (0, 0))],
        out_specs=pl.BlockSpec((mm, M_DIM), lambda j: (0, j)),
        out_shape=jax.ShapeDtypeStruct((mm, n), BF16),
        compiler_params=_cparams(("arbitrary",)),
        name="mem_prep",
    )(mem2d, g_row, w_mkv, gk_row)


def _attn_m_kernel(q_ref, k_ref, v_ref, g_ref, gq_ref, o_ref, *, scale):
    q = q_ref[...].astype(F32)
    qn = q * (lax.rsqrt(jnp.mean(q * q, axis=-1, keepdims=True) + EPS) * scale) * gq_ref[...]
    s = lax.dot_general(qn.astype(BF16), k_ref[...], _NT, preferred_element_type=F32)
    p = jnp.exp(s - jnp.max(s, axis=-1, keepdims=True))
    l = jnp.sum(p, axis=-1, keepdims=True)
    o = jnp.dot(p.astype(BF16), v_ref[...], preferred_element_type=F32) / l
    o_ref[...] = (o * _silu(g_ref[...].astype(F32))).astype(o_ref.dtype)


def _attn_m(proj, kv_m, gq_row, seq, mem_len, tm):
    m = proj.shape[0]
    nb = seq // tm
    q0 = _Cols.QM // M_DIM
    g0 = _Cols.GM // M_DIM
    assert mem_len == M_DIM, "memory block spec assumes MEM_LEN == M_DIM rows per batch"
    return pl.pallas_call(
        functools.partial(_attn_m_kernel, scale=1.0 / math.sqrt(M_DIM)),
        grid=(m // tm, M_HEADS),
        in_specs=[pl.BlockSpec((tm, M_DIM), lambda i, h: (i, q0 + h)),
                  pl.BlockSpec((mem_len, M_DIM), lambda i, h: (i // nb, h)),
                  pl.BlockSpec((mem_len, M_DIM), lambda i, h: (i // nb, M_HEADS + h)),
                  pl.BlockSpec((tm, M_DIM), lambda i, h: (i, g0 + h)),
                  pl.BlockSpec((1, M_DIM), lambda i, h: (0, 0))],
        out_specs=pl.BlockSpec((tm, M_DIM), lambda i, h: (i, h)),
        out_shape=jax.ShapeDtypeStruct((m, M_HEADS * M_DIM), BF16),
        compiler_params=_cparams(("parallel", "arbitrary")),
        name="attn_m",
    )(proj, kv_m, kv_m, proj, gq_row)


def _mix_kernel(ua_ref, ub_ref, um_ref, wa_ref, wb_ref, wm_ref, ra_ref, rb_ref, rm_ref, o_ref,
                wa_bf, wb_bf, wm_bf):
    @pl.when(pl.program_id(1) == 0)
    def _cast():
        wa_bf[...] = wa_ref[...].astype(BF16)
        wb_bf[...] = wb_ref[...].astype(BF16)
        wm_bf[...] = wm_ref[...].astype(BF16)

    rows = o_ref.shape[0] // MIX_ROW_CHUNKS
    for c in range(MIX_ROW_CHUNKS):
        rs = slice(c * rows, (c + 1) * rows)
        ya = jnp.dot(ua_ref[rs, :], wa_bf[...], preferred_element_type=F32)
        yb = jnp.dot(ub_ref[rs, :], wb_bf[...], preferred_element_type=F32)
        ym = jnp.dot(um_ref[rs, :], wm_bf[...], preferred_element_type=F32)
        mixed = (_sigmoid(ra_ref[rs, :].astype(F32)) * ya + _sigmoid(rb_ref[rs, :].astype(F32)) * yb
                 + _sigmoid(rm_ref[rs, :].astype(F32)) * ym)
        o_ref[rs, :] = mixed.astype(o_ref.dtype)


def _mix(u_a, u_b, u_m, w_pa, w_pb, w_pm, layer, proj, tm, tn):
    m = u_a.shape[0]
    r0 = _Cols.MERGE // tn
    rstep = D_MODEL // tn

    def rspec(branch):
        return pl.BlockSpec((tm, tn), lambda j, i: (i, r0 + branch * rstep + j))

    def wspec(w):
        return pl.BlockSpec((None, w.shape[1], tn), lambda j, i: (layer, 0, j))

    return pl.pallas_call(
        _mix_kernel,
        grid=(D_MODEL // tn, m // tm),
        in_specs=[pl.BlockSpec((tm, u_a.shape[1]), lambda j, i: (i, 0)),
                  pl.BlockSpec((tm, u_b.shape[1]), lambda j, i: (i, 0)),
                  pl.BlockSpec((tm, u_m.shape[1]), lambda j, i: (i, 0)),
                  wspec(w_pa), wspec(w_pb), wspec(w_pm),
                  rspec(0), rspec(1), rspec(2)],
        out_specs=pl.BlockSpec((tm, tn), lambda j, i: (i, j)),
        out_shape=jax.ShapeDtypeStruct((m, D_MODEL), BF16),
        scratch_shapes=[pltpu.VMEM((w_pa.shape[1], tn), BF16), pltpu.VMEM((w_pb.shape[1], tn), BF16),
                        pltpu.VMEM((w_pm.shape[1], tn), BF16)],
        compiler_params=_cparams(("arbitrary", "arbitrary")),
        name="mix",
    )(u_a, u_b, u_m, w_pa, w_pb, w_pm, proj, proj, proj)


def _out_kernel(a_ref, w_ref, x_ref, o_ref, w_bf):
    @pl.when(pl.program_id(1) == 0)
    def _cast():
        w_bf[...] = w_ref[...].astype(BF16)

    o_ref[...] = x_ref[...] + jnp.dot(a_ref[...], w_bf[...], preferred_element_type=F32)


def _out_proj(mixed, w_out, layer, x, tm, tn):
    m, k = mixed.shape
    n = w_out.shape[2]
    return pl.pallas_call(
        _out_kernel,
        grid=(n // tn, m // tm),
        in_specs=[pl.BlockSpec((tm, k), lambda j, i: (i, 0)),
                  pl.BlockSpec((None, k, tn), lambda j, i: (layer, 0, j)),
                  pl.BlockSpec((tm, tn), lambda j, i: (i, j))],
        out_specs=pl.BlockSpec((tm, tn), lambda j, i: (i, j)),
        out_shape=jax.ShapeDtypeStruct((m, n), F32),
        scratch_shapes=[pltpu.VMEM((k, tn), BF16)],
        compiler_params=_cparams(("arbitrary", "arbitrary")),
        name="out_proj",
    )(mixed, w_out, x)


def _rotate_half_cols(w):
    half = A_ROPE // 2
    return jnp.concatenate([-w[..., half:], w[..., :half]], axis=-1)


def _rope_gain(g_rope):
    half = A_ROPE // 2
    return jnp.concatenate([g_rope, g_rope[half:], g_rope[:half]])


def _prep_w_uq(w_uq):
    w = w_uq.reshape(Q_LORA, A_HEADS, A_QK)
    rope = w[..., A_NOPE:]
    w = jnp.concatenate([w[..., :A_NOPE], rope, _rotate_half_cols(rope)], axis=-1)
    return w.reshape(Q_LORA, A_HEADS * A_PAD).astype(BF16)


def _head_gain(g):
    return jnp.concatenate([g[:A_NOPE], _rope_gain(g[A_NOPE:])])[None, :]


def _tiles(m, seq):
    return dict(
        row=min(512, m),
        mm_m=min(1024, m),
        mix_n=512,
        attn=min(512, seq // 2),
        b_keys=min(256, seq // 4),
        m_rows=min(1024, seq),
    )


def kernel(x, mem, positions, g_pre, w_in, g_q_lat, w_uq, g_kv_lat, w_ukv, g_qn_a, g_kn_a,
           w_pa, w_pb, g_mem, w_mkv, g_qn_m, g_kn_m, w_pm, w_out):
    batch, seq, d = x.shape
    mem_len = mem.shape[1]
    depth = w_in.shape[0]
    m = batch * seq
    t = _tiles(m, seq)

    half = A_ROPE // 2
    freqs = ROPE_THETA ** (-jnp.arange(half, dtype=F32) / half)
    freq_row = jnp.tile(freqs, LANE // half)[None, :]
    pos_b = jnp.broadcast_to(positions.reshape(m, 1).astype(F32), (m, LANE))
    tbl = _rope_table(pos_b, freq_row, t["row"])

    xs = x.reshape(m, d)
    mem2d = mem.reshape(batch * mem_len, d)
    w_in_t = jnp.swapaxes(w_in, 1, 2)
    for l in range(depth):
        h = _norm(xs, g_pre[l][None, :], t["row"])
        proj = _in_proj(h, w_in_t, l, t["mm_m"])

        q_a = _q_prep(proj, g_q_lat[l][None, :], _prep_w_uq(w_uq[l]), _head_gain(g_qn_a[l]),
                      tbl, t["row"])
        k_a, v_a = _kv_prep(proj, g_kv_lat[l][None, :], w_ukv[l].astype(BF16),
                            _head_gain(g_kn_a[l]), tbl, t["row"])
        u_a = _attn_a_flat(q_a, k_a, v_a, proj, batch, seq, t["attn"])
        u_b = _attn_b_flat(proj, batch, seq, t["attn"], t["b_keys"])

        kv_m = _mem_prep(mem2d, g_mem[l][None, :], w_mkv, l, g_kn_m[l][None, :])
        u_m = _attn_m(proj, kv_m, g_qn_m[l][None, :], seq, mem_len, t["m_rows"])

        mixed = _mix(u_a, u_b, u_m, w_pa, w_pb, w_pm, l, proj, t["mm_m"], t["mix_n"])
        xs = _out_proj(mixed, w_out, l, xs, t["mm_m"], t["mix_n"])
    return xs.reshape(batch, seq, d)
```

```python
import functools
import math

import jax
import jax.numpy as jnp
from jax import lax
from jax.experimental import pallas as pl
from jax.experimental.pallas import tpu as pltpu

F32 = jnp.float32
BF16 = jnp.bfloat16

EPS = 1e-6
LOG2E = 1.4426950408889634
NEG = -1e30
MIX_ROW_CHUNKS = 4
STEPS_PER_TRIP = 16
A_STEPS_PER_TRIP = 32
ROPE_THETA = 10000.0
CHUNK = 64

A_HEADS, A_NOPE, A_ROPE, A_V = 16, 128, 64, 128
A_QK = A_NOPE + A_ROPE
A_PAD = 256
Q_LORA, KV_LORA = 1024, 512
B_HEADS, B_DIM = 8, 128
M_HEADS, M_DIM = 4, 256
D_MODEL = 4096
N_BRANCH = 3

LANE = 128
VMEM_LIMIT = 56 * 1024 * 1024


class _Cols:
    CQ = 0
    CKV = CQ + Q_LORA
    GA = CKV + KV_LORA
    QKVB = GA + A_HEADS * A_V
    GB = QKVB + 3 * B_HEADS * B_DIM
    QM = GB + B_HEADS * B_DIM
    GM = QM + M_HEADS * M_DIM
    MERGE = GM + M_HEADS * M_DIM
    KR = MERGE + N_BRANCH * D_MODEL
    USED = KR + LANE
    TOTAL = 22528


def _cparams(sem):
    return pltpu.CompilerParams(dimension_semantics=sem, vmem_limit_bytes=VMEM_LIMIT)


def _sigmoid(v):
    return 1.0 / (1.0 + jnp.exp(-v))


def _silu(v):
    return v * _sigmoid(v)


_NT = (((1,), (1,)), ((), ()))


def _rope_table_kernel(pos_ref, freq_ref, o_ref):
    ang = pos_ref[...] * freq_ref[...]
    lane = lax.broadcasted_iota(jnp.int32, ang.shape, 1)
    o_ref[...] = jnp.where(lane < A_ROPE, jnp.cos(ang), jnp.sin(ang))


def _rope_table(pos_b, freq_row, tm):
    m = pos_b.shape[0]
    return pl.pallas_call(
        _rope_table_kernel,
        grid=(m // tm,),
        in_specs=[pl.BlockSpec((tm, LANE), lambda i: (i, 0)),
                  pl.BlockSpec((1, LANE), lambda i: (0, 0))],
        out_specs=pl.BlockSpec((tm, LANE), lambda i: (i, 0)),
        out_shape=jax.ShapeDtypeStruct((m, LANE), F32),
        compiler_params=_cparams(("parallel",)),
        name="rope_table",
    )(pos_b, freq_row)


def _norm_kernel(x_ref, g_ref, o_ref):
    x = x_ref[...]
    y = x * lax.rsqrt(jnp.mean(x * x, axis=-1, keepdims=True) + EPS)
    o_ref[...] = (y * g_ref[...]).astype(o_ref.dtype)


def _norm(x, g_row, tm):
    m, d = x.shape
    return pl.pallas_call(
        _norm_kernel,
        grid=(m // tm,),
        in_specs=[pl.BlockSpec((tm, d), lambda i: (i, 0)),
                  pl.BlockSpec((1, d), lambda i: (0, 0))],
        out_specs=pl.BlockSpec((tm, d), lambda i: (i, 0)),
        out_shape=jax.ShapeDtypeStruct((m, d), BF16),
        compiler_params=_cparams(("parallel",)),
        name="norm",
    )(x, g_row)


IN_TN = 1024
B_SCALE2 = LOG2E / math.sqrt(B_DIM)
assert 2 * IN_TN <= _Cols.QKVB and _Cols.QKVB + B_HEADS * B_DIM <= _Cols.KR - _Cols.KR % IN_TN
KR_SRC = Q_LORA + KV_LORA


def _in_proj_kernel(a_ref, w_ref, wnext_ref, wkr_ref, o_ref, wp_ref):
    j = pl.program_id(0)
    last = pl.num_programs(0) - 1
    cut = KR_SRC % IN_TN
    tail = _Cols.KR % IN_TN
    half = A_ROPE // 2
    assert KR_SRC // IN_TN == 1 and _Cols.KR // IN_TN == _Cols.TOTAL // IN_TN - 1

    @pl.when(pl.program_id(1) == 0)
    def _assemble():
        @pl.when(j == 0)
        def _():
            wp_ref[...] = w_ref[...].astype(BF16)

        @pl.when(j == 1)
        def _():
            wp_ref[:cut, :] = w_ref[:cut, :].astype(BF16)
            wp_ref[cut:IN_TN - A_ROPE, :] = w_ref[cut + A_ROPE:, :].astype(BF16)
            wp_ref[IN_TN - A_ROPE:, :] = wnext_ref[...].astype(BF16)

        @pl.when(jnp.logical_and(j > 1, j < last))
        def _():
            row = j * IN_TN + lax.broadcasted_iota(jnp.int32, (IN_TN, 1), 0)
            is_qb = jnp.logical_and(row >= _Cols.QKVB, row < _Cols.QKVB + B_HEADS * B_DIM)
            sc = jnp.where(is_qb, B_SCALE2, 1.0)
            wp_ref[:IN_TN - A_ROPE, :] = (w_ref[A_ROPE:, :] * sc[:IN_TN - A_ROPE]).astype(BF16)
            wp_ref[IN_TN - A_ROPE:, :] = (wnext_ref[...] * sc[IN_TN - A_ROPE:]).astype(BF16)

        @pl.when(j == last)
        def _():
            kr = wkr_ref[...]
            wp_ref[:tail, :] = w_ref[A_ROPE:A_ROPE + tail, :].astype(BF16)
            wp_ref[tail:tail + A_ROPE, :] = kr.astype(BF16)
            wp_ref[tail + A_ROPE:tail + A_ROPE + half, :] = (-kr[half:, :]).astype(BF16)
            wp_ref[tail + A_ROPE + half:tail + 2 * A_ROPE, :] = kr[:half, :].astype(BF16)
            wp_ref[tail + 2 * A_ROPE:, :] = jnp.zeros((IN_TN - tail - 2 * A_ROPE, wp_ref.shape[1]),
                                                      BF16)

    o_ref[...] = lax.dot_general(a_ref[...], wp_ref[...], _NT,
                                 preferred_element_type=F32).astype(o_ref.dtype)


def _in_proj(h, w_in_t, layer, tm):
    m, k = h.shape
    n_src = w_in_t.shape[1]
    rope_blocks = IN_TN // A_ROPE
    last_rope_block = n_src // A_ROPE - 1
    return pl.pallas_call(
        _in_proj_kernel,
        grid=(_Cols.TOTAL // IN_TN, m // tm),
        in_specs=[pl.BlockSpec((tm, k), lambda j, i: (i, 0)),
                  pl.BlockSpec((None, IN_TN, k), lambda j, i: (layer, j, 0),
                               pipeline_mode=pl.Buffered(1)),
                  pl.BlockSpec((None, A_ROPE, k),
                               lambda j, i: (layer, jnp.minimum((j + 1) * rope_blocks,
                                                                last_rope_block), 0)),
                  pl.BlockSpec((None, A_ROPE, k), lambda j, i: (layer, KR_SRC // A_ROPE, 0))],
        out_specs=pl.BlockSpec((tm, IN_TN), lambda j, i: (i, j)),
        out_shape=jax.ShapeDtypeStruct((m, _Cols.TOTAL), BF16),
        scratch_shapes=[pltpu.VMEM((IN_TN, k), BF16)],
        compiler_params=_cparams(("arbitrary", "arbitrary")),
        name="in_proj",
    )(h, w_in_t, w_in_t, w_in_t)


def _q_prep_kernel(c_ref, gl_ref, w_ref, gh_ref, tbl_ref, o_ref, *, scale):
    c = c_ref[...].astype(F32)
    cn = c * lax.rsqrt(jnp.mean(c * c, axis=-1, keepdims=True) + EPS) * gl_ref[...]
    cn = cn.astype(BF16)
    tbl = tbl_ref[...]
    g1 = gh_ref[:, :A_NOPE]
    g2t = gh_ref[:, A_NOPE:] * tbl
    lane = lax.broadcasted_iota(jnp.int32, tbl.shape, 1)
    for h in range(A_HEADS):
        acc = jnp.dot(cn, w_ref[:, h * A_PAD:(h + 1) * A_PAD], preferred_element_type=F32)
        v1 = acc[:, :A_NOPE]
        v2 = acc[:, A_NOPE:]
        ss = (jnp.sum(v1 * v1, axis=-1, keepdims=True)
              + jnp.sum(jnp.where(lane < A_ROPE, v2 * v2, 0.0), axis=-1, keepdims=True))
        rs = lax.rsqrt(ss * (1.0 / A_QK) + EPS) * scale
        o_ref[:, h * A_PAD:h * A_PAD + A_NOPE] = (v1 * g1 * rs).astype(o_ref.dtype)
        o_ref[:, h * A_PAD + A_NOPE:(h + 1) * A_PAD] = (v2 * g2t * rs).astype(o_ref.dtype)


def _q_prep(proj, gl_row, w_uq, gh_row, tbl, tm):
    m = proj.shape[0]
    n = A_HEADS * A_PAD
    return pl.pallas_call(
        functools.partial(_q_prep_kernel, scale=LOG2E / math.sqrt(A_QK)),
        grid=(m // tm,),
        in_specs=[pl.BlockSpec((tm, Q_LORA), lambda i: (i, _Cols.CQ // Q_LORA)),
                  pl.BlockSpec((1, Q_LORA), lambda i: (0, 0)),
                  pl.BlockSpec((Q_LORA, n), lambda i: (0, 0)),
                  pl.BlockSpec((1, A_PAD), lambda i: (0, 0)),
                  pl.BlockSpec((tm, LANE), lambda i: (i, 0))],
        out_specs=pl.BlockSpec((tm, n), lambda i: (i, 0)),
        out_shape=jax.ShapeDtypeStruct((m, n), BF16),
        compiler_params=_cparams(("parallel",)),
        name="q_prep",
    )(proj, gl_row, w_uq, gh_row, tbl)


def _kv_prep_kernel(c_ref, kr_ref, gl_ref, w_ref, gh_ref, tbl_ref, kt_ref, v_ref):
    c = c_ref[...].astype(F32)
    cn = c * lax.rsqrt(jnp.mean(c * c, axis=-1, keepdims=True) + EPS) * gl_ref[...]
    cn = cn.astype(BF16)
    kr = kr_ref[...].astype(F32)
    lane = lax.broadcasted_iota(jnp.int32, kr.shape, 1)
    ss_r = jnp.sum(jnp.where(lane < A_ROPE, kr * kr, 0.0), axis=-1, keepdims=True)
    ab = kr * gh_ref[:, A_NOPE:] * tbl_ref[...]
    kk = ab + pltpu.roll(ab, A_ROPE, axis=1)
    g1 = gh_ref[:, :A_NOPE]
    ones = jnp.ones((kr.shape[0], A_PAD - A_V), v_ref.dtype)
    for h in range(A_HEADS):
        acc = jnp.dot(cn, w_ref[:, h * A_PAD:(h + 1) * A_PAD], preferred_element_type=F32)
        kn = acc[:, :A_NOPE]
        ss = jnp.sum(kn * kn, axis=-1, keepdims=True) + ss_r
        rs = lax.rsqrt(ss * (1.0 / A_QK) + EPS)
        kt_ref[h * A_PAD:h * A_PAD + A_NOPE, :] = (kn * g1 * rs).T.astype(kt_ref.dtype)
        kt_ref[h * A_PAD + A_NOPE:(h + 1) * A_PAD, :] = (kk * rs).T.astype(kt_ref.dtype)
        v_ref[:, h * A_PAD:h * A_PAD + A_V] = acc[:, A_NOPE:].astype(v_ref.dtype)
        v_ref[:, h * A_PAD + A_V:(h + 1) * A_PAD] = ones


def _kv_prep(proj, gl_row, w_ukv, gh_row, tbl, tm):
    m = proj.shape[0]
    nk = A_HEADS * A_PAD
    nv = A_HEADS * A_PAD
    return pl.pallas_call(
        _kv_prep_kernel,
        grid=(m // tm,),
        in_specs=[pl.BlockSpec((tm, KV_LORA), lambda i: (i, _Cols.CKV // KV_LORA)),
                  pl.BlockSpec((tm, LANE), lambda i: (i, _Cols.KR // LANE)),
                  pl.BlockSpec((1, KV_LORA), lambda i: (0, 0)),
                  pl.BlockSpec((KV_LORA, nk), lambda i: (0, 0)),
                  pl.BlockSpec((1, A_PAD), lambda i: (0, 0)),
                  pl.BlockSpec((tm, LANE), lambda i: (i, 0))],
        out_specs=[pl.BlockSpec((nk, tm), lambda i: (0, i)),
                   pl.BlockSpec((tm, nv), lambda i: (i, 0))],
        out_shape=[jax.ShapeDtypeStruct((nk, m), BF16),
                   jax.ShapeDtypeStruct((m, nv), BF16)],
        compiler_params=_cparams(("parallel",)),
        name="kv_prep",
    )(proj, proj, gl_row, w_ukv, gh_row, tbl)


def _attn_a_flat_kernel(q_ref, kt_ref, v_ref, g_ref, o_ref, s_ref, p_ref, bias_ref, oacc_ref,
                        *, tq, nq):
    n_units = nq * (nq + 1) // 2

    def advance(unit):
        qb, u = unit
        done = u == qb
        return jnp.where(done, qb + 1, qb), jnp.where(done, 0, u + 1)

    def key_block(unit):
        qb, u = unit
        return jnp.where(u == 0, qb, u - 1)

    def scores(unit, slot):
        qb, u = unit
        q = q_ref[pl.ds(pl.multiple_of(qb * tq, tq), tq), :]
        kt = kt_ref[:, pl.ds(pl.multiple_of(key_block(unit) * tq, tq), tq)]
        own = jnp.where(u == 0, 1, 0)
        s_ref[slot] = jnp.dot(q, kt, preferred_element_type=F32) + bias_ref[own]

    def p_times_v(unit, slot):
        v = v_ref[pl.ds(pl.multiple_of(key_block(unit) * tq, tq), tq), :]
        return jnp.dot(p_ref[slot], v, preferred_element_type=F32)

    def softmax(unit, slot, m, acc):
        s = s_ref[slot]
        m = jnp.where(unit[1] == 0, NEG, m)
        m_new = jnp.maximum(m, jnp.max(s, axis=-1, keepdims=True))
        p_ref[slot] = jnp.exp2(s - m_new).astype(BF16)
        return m_new, jnp.exp2(m - m_new) * acc

    def step(slot, state):
        m, pend, u1, u2, u3 = state
        scores(u1, slot)
        acc = pend + p_times_v(u3, slot)
        oacc_ref[u3[0]] = acc
        m, pend = softmax(u2, 1 - slot, m, acc)
        return m, pend, advance(u1), u1, u2

    row_chunk = lax.broadcasted_iota(jnp.int32, (tq, tq), 0) // CHUNK
    key_chunk = lax.broadcasted_iota(jnp.int32, (tq, tq), 1) // CHUNK
    bias_ref[0] = jnp.zeros((tq, tq), F32)
    bias_ref[1] = jnp.where(key_chunk <= row_chunk, 0.0, NEG)
    p_ref[1] = jnp.zeros((tq, tq), BF16)
    zero = jnp.int32(0)
    first = (zero, zero)
    scores(first, 0)
    state = (jnp.full((tq, 1), NEG, F32), jnp.zeros((tq, A_PAD), F32), advance(first), first, first)

    def steps(count, state):
        for k in range(count):
            state = step((1 + k) % 2, state)
        return state

    n_loop = n_units - 1
    rest = n_loop % A_STEPS_PER_TRIP
    state = lax.fori_loop(0, n_loop // A_STEPS_PER_TRIP, lambda _, st: steps(A_STEPS_PER_TRIP, st),
                          state)
    state = lax.fori_loop(0, rest // 2, lambda _, st: steps(2, st), state)
    state = steps(rest % 2, state)
    m, pend, _, u2, u3 = state
    last_slot = (n_units - 1) % 2
    acc = pend + p_times_v(u3, 1 - last_slot)
    _, pend = softmax(u2, last_slot, m, acc)
    oacc_ref[u2[0]] = pend + p_times_v(u2, last_slot)
    for qb in range(nq):
        acc = oacc_ref[qb]
        gate = g_ref[qb * tq:(qb + 1) * tq, :].astype(F32)
        o = acc[:, :A_V] / acc[:, A_V:]
        o_ref[qb * tq:(qb + 1) * tq, :] = (o * _silu(gate)).astype(o_ref.dtype)


def _attn_a_flat(q_a, kt_a, v_a, proj, batch, seq, tq):
    m = q_a.shape[0]
    nq = seq // tq
    gate0 = _Cols.GA // A_V
    return pl.pallas_call(
        functools.partial(_attn_a_flat_kernel, tq=tq, nq=nq),
        grid=(batch, A_HEADS),
        in_specs=[pl.BlockSpec((seq, A_PAD), lambda b, h: (b, h)),
                  pl.BlockSpec((A_PAD, seq), lambda b, h: (h, b)),
                  pl.BlockSpec((seq, A_PAD), lambda b, h: (b, h)),
                  pl.BlockSpec((seq, A_V), lambda b, h: (b, gate0 + h))],
        out_specs=pl.BlockSpec((seq, A_V), lambda b, h: (b, h)),
        out_shape=jax.ShapeDtypeStruct((m, A_HEADS * A_V), BF16),
        scratch_shapes=[pltpu.VMEM((2, tq, tq), F32), pltpu.VMEM((2, tq, tq), BF16),
                        pltpu.VMEM((2, tq, tq), F32), pltpu.VMEM((nq, tq, A_PAD), F32)],
        compiler_params=_cparams(("parallel", "arbitrary")),
        name="attn_a",
    )(q_a, kt_a, v_a, proj)


def _attn_b_flat_kernel(q_ref, k_ref, v_ref, g_ref, o_ref,
                        z_ref, hl_ref, w_ref, bias_ref, oacc_ref, *, tq, ks, nq):
    per_block = tq // ks
    assert per_block == 2, "bias tables below cover the two own-block units"
    n_units = per_block * nq * (nq + 1) // 2
    r = lax.broadcasted_iota(jnp.int32, (ks, ks), 0)
    c = lax.broadcasted_iota(jnp.int32, (ks, ks), 1)
    tri = jnp.where(r >= c, 1.0, 0.0).astype(BF16)

    def advance(unit):
        qb, u = unit
        done = u == per_block * qb + per_block - 1
        return jnp.where(done, qb + 1, qb), jnp.where(done, 0, u + 1)

    def key_start(unit):
        qb, u = unit
        sub = jnp.clip(per_block * qb + per_block - 1 - u, 0, per_block * nq - 1)
        return pl.multiple_of(sub * ks, ks)

    def stage_scores(t, unit, zslot):
        qb, u = unit
        q = q_ref[pl.ds(pl.multiple_of(jnp.minimum(qb, nq - 1) * tq, tq), tq), :]
        z = lax.dot_general(q, k_ref[pl.ds(key_start(unit), ks), :], _NT,
                            preferred_element_type=F32)
        table = jnp.where(t < n_units, jnp.where(u < per_block, u + 1, 0), 3)
        z_ref[zslot] = z + bias_ref[table]

    def stage_logs(slot, zslot):
        z = z_ref[zslot]
        neg_abs = lax.bitcast_convert_type(
            lax.bitcast_convert_type(z, jnp.uint32) | jnp.uint32(0x80000000), F32)
        nlf = jnp.maximum(z, 0.0) + jnp.log(1.0 + jnp.exp2(neg_abs)) * LOG2E
        hl_ref[slot] = nlf.astype(BF16)

    def stage_weights(unit, slot, zslot, mass):
        mass = jnp.where(unit[1] == 0, 0.0, mass)
        tail = jnp.dot(hl_ref[slot], tri, preferred_element_type=F32)
        w_ref[slot] = jnp.exp2(z_ref[zslot] - tail - mass).astype(BF16)
        return mass + tail[:, :1]

    def stage_pv(unit, slot, acc):
        acc = jnp.where(unit[1] == 0, 0.0, acc)
        v = v_ref[pl.ds(key_start(unit), ks), :]
        acc = acc + jnp.dot(w_ref[slot], v, preferred_element_type=F32)
        oacc_ref[jnp.minimum(unit[0], nq - 1)] = acc
        return acc

    def step(k, state):
        t, mass, acc, u1, u2, u3, u4 = state
        slot = k % 2
        stage_scores(t, u1, k)
        acc = stage_pv(u4, 1 - slot, acc)
        mass = stage_weights(u3, slot, (k - 2) % 4, mass)
        stage_logs(1 - slot, (k - 1) % 4)
        return t + 1, mass, acc, advance(u1), u1, u2, u3

    def steps(count, state):
        for k in range(count):
            state = step(k % 4, state)
        return state

    d = (lax.broadcasted_iota(jnp.int32, (tq, ks), 0) - lax.broadcasted_iota(jnp.int32, (tq, ks), 1))
    bias_ref[0] = jnp.zeros((tq, ks), F32)
    bias_ref[1] = jnp.where(d > ks, 0.0, NEG)
    bias_ref[2] = jnp.where(d > 0, 0.0, NEG)
    bias_ref[3] = jnp.full((tq, ks), NEG, F32)
    z_ref[2] = jnp.full((tq, ks), NEG, F32)
    z_ref[3] = jnp.full((tq, ks), NEG, F32)
    hl_ref[0] = jnp.zeros((tq, ks), BF16)
    w_ref[1] = jnp.zeros((tq, ks), BF16)
    zero = jnp.int32(0)
    first = (zero, zero)
    state = (zero, jnp.zeros((tq, 1), F32), jnp.zeros((tq, B_DIM), F32), first, first, first, first)
    n_steps = n_units + 3
    rest = n_steps % STEPS_PER_TRIP
    state = lax.fori_loop(0, n_steps // STEPS_PER_TRIP, lambda _, st: steps(STEPS_PER_TRIP, st),
                          state)
    state = lax.fori_loop(0, rest // 4, lambda _, st: steps(4, st), state)
    steps(rest % 4, state)
    for qb in range(nq):
        gate = g_ref[qb * tq:(qb + 1) * tq, :].astype(F32)
        o_ref[qb * tq:(qb + 1) * tq, :] = (oacc_ref[qb] * _silu(gate)).astype(o_ref.dtype)


def _attn_b_flat(proj, batch, seq, tq, ks):
    m = proj.shape[0]
    nq = seq // tq
    q0 = _Cols.QKVB // B_DIM
    k0 = q0 + B_HEADS
    v0 = k0 + B_HEADS
    g0 = _Cols.GB // B_DIM
    return pl.pallas_call(
        functools.partial(_attn_b_flat_kernel, tq=tq, ks=ks, nq=nq),
        grid=(batch, B_HEADS),
        in_specs=[pl.BlockSpec((seq, B_DIM), lambda b, h: (b, q0 + h)),
                  pl.BlockSpec((seq, B_DIM), lambda b, h: (b, k0 + h)),
                  pl.BlockSpec((seq, B_DIM), lambda b, h: (b, v0 + h)),
                  pl.BlockSpec((seq, B_DIM), lambda b, h: (b, g0 + h))],
        out_specs=pl.BlockSpec((seq, B_DIM), lambda b, h: (b, h)),
        out_shape=jax.ShapeDtypeStruct((m, B_HEADS * B_DIM), BF16),
        scratch_shapes=[pltpu.VMEM((4, tq, ks), F32),
                        pltpu.VMEM((2, tq, ks), BF16), pltpu.VMEM((2, tq, ks), BF16),
                        pltpu.VMEM((4, tq, ks), F32), pltpu.VMEM((nq, tq, B_DIM), F32)],
        compiler_params=_cparams(("parallel", "arbitrary")),
        name="attn_b",
    )(proj, proj, proj, proj)


def _mem_prep_kernel(mem_ref, g_ref, w_ref, gk_ref, o_ref):
    j = pl.program_id(1)
    x = mem_ref[...]
    xn = (x * lax.rsqrt(jnp.mean(x * x, axis=-1, keepdims=True) + EPS) * g_ref[...]).astype(BF16)
    y = jnp.dot(xn, w_ref[...].astype(BF16), preferred_element_type=F32)
    yk = y * lax.rsqrt(jnp.mean(y * y, axis=-1, keepdims=True) + EPS) * gk_ref[...]
    o_ref[...] = jnp.where(j < M_HEADS, yk, y).astype(o_ref.dtype)


def _mem_prep(mem2d, g_mem, w_mkv, g_kn_m):
    mm, d = mem2d.shape
    depth = w_mkv.shape[0]
    n = 2 * M_HEADS * M_DIM
    return pl.pallas_call(
        _mem_prep_kernel,
        grid=(depth, 2 * M_HEADS),
        in_specs=[pl.BlockSpec((mm, d), lambda l, j: (0, 0)),
                  pl.BlockSpec((None, 1, d), lambda l, j: (l, 0, 0)),
                  pl.BlockSpec((None, d, M_DIM), lambda l, j: (l, 0, j)),
                  pl.BlockSpec((None, 1, M_DIM), lambda l, j: (l, 0, 0))],
        out_specs=pl.BlockSpec((None, mm, M_DIM), lambda l, j: (l, 0, j)),
        out_shape=jax.ShapeDtypeStruct((depth, mm, n), BF16),
        compiler_params=_cparams(("arbitrary", "arbitrary")),
        name="mem_prep",
    )(mem2d, g_mem, w_mkv, g_kn_m)


def _attn_m_kernel(q_ref, k_ref, v_ref, g_ref, gq_ref, o_ref, *, scale):
    q = q_ref[...].astype(F32)
    qn = q * (lax.rsqrt(jnp.mean(q * q, axis=-1, keepdims=True) + EPS) * scale) * gq_ref[...]
    s = lax.dot_general(qn.astype(BF16), k_ref[...], _NT, preferred_element_type=F32)
    p = jnp.exp(s - jnp.max(s, axis=-1, keepdims=True))
    l = jnp.sum(p, axis=-1, keepdims=True)
    o = jnp.dot(p.astype(BF16), v_ref[...], preferred_element_type=F32) / l
    o_ref[...] = (o * _silu(g_ref[...].astype(F32))).astype(o_ref.dtype)


def _attn_m(proj, kv_m, gq_row, seq, mem_len, tm):
    m = proj.shape[0]
    nb = seq // tm
    q0 = _Cols.QM // M_DIM
    g0 = _Cols.GM // M_DIM
    assert mem_len == M_DIM, "memory block spec assumes MEM_LEN == M_DIM rows per batch"
    return pl.pallas_call(
        functools.partial(_attn_m_kernel, scale=1.0 / math.sqrt(M_DIM)),
        grid=(m // tm, M_HEADS),
        in_specs=[pl.BlockSpec((tm, M_DIM), lambda i, h: (i, q0 + h)),
                  pl.BlockSpec((mem_len, M_DIM), lambda i, h: (i // nb, h)),
                  pl.BlockSpec((mem_len, M_DIM), lambda i, h: (i // nb, M_HEADS + h)),
                  pl.BlockSpec((tm, M_DIM), lambda i, h: (i, g0 + h)),
                  pl.BlockSpec((1, M_DIM), lambda i, h: (0, 0))],
        out_specs=pl.BlockSpec((tm, M_DIM), lambda i, h: (i, h)),
        out_shape=jax.ShapeDtypeStruct((m, M_HEADS * M_DIM), BF16),
        compiler_params=_cparams(("parallel", "arbitrary")),
        name="attn_m",
    )(proj, kv_m, kv_m, proj, gq_row)


def _mix_kernel(ua_ref, ub_ref, um_ref, wa_ref, wb_ref, wm_ref, ra_ref, rb_ref, rm_ref, o_ref,
                wa_bf, wb_bf, wm_bf):
    @pl.when(pl.program_id(1) == 0)
    def _cast():
        wa_bf[...] = wa_ref[...].astype(BF16)
        wb_bf[...] = wb_ref[...].astype(BF16)
        wm_bf[...] = wm_ref[...].astype(BF16)

    rows = o_ref.shape[0] // MIX_ROW_CHUNKS
    for c in range(MIX_ROW_CHUNKS):
        rs = slice(c * rows, (c + 1) * rows)
        ya = jnp.dot(ua_ref[rs, :], wa_bf[...], preferred_element_type=F32)
        yb = jnp.dot(ub_ref[rs, :], wb_bf[...], preferred_element_type=F32)
        ym = jnp.dot(um_ref[rs, :], wm_bf[...], preferred_element_type=F32)
        mixed = (_sigmoid(ra_ref[rs, :].astype(F32)) * ya + _sigmoid(rb_ref[rs, :].astype(F32)) * yb
                 + _sigmoid(rm_ref[rs, :].astype(F32)) * ym)
        o_ref[rs, :] = mixed.astype(o_ref.dtype)


def _mix(u_a, u_b, u_m, w_pa, w_pb, w_pm, layer, proj, tm, tn):
    m = u_a.shape[0]
    r0 = _Cols.MERGE // tn
    rstep = D_MODEL // tn

    def rspec(branch):
        return pl.BlockSpec((tm, tn), lambda j, i: (i, r0 + branch * rstep + j))

    def wspec(w):
        return pl.BlockSpec((None, w.shape[1], tn), lambda j, i: (layer, 0, j))

    return pl.pallas_call(
        _mix_kernel,
        grid=(D_MODEL // tn, m // tm),
        in_specs=[pl.BlockSpec((tm, u_a.shape[1]), lambda j, i: (i, 0)),
                  pl.BlockSpec((tm, u_b.shape[1]), lambda j, i: (i, 0)),
                  pl.BlockSpec((tm, u_m.shape[1]), lambda j, i: (i, 0)),
                  wspec(w_pa), wspec(w_pb), wspec(w_pm),
                  rspec(0), rspec(1), rspec(2)],
        out_specs=pl.BlockSpec((tm, tn), lambda j, i: (i, j)),
        out_shape=jax.ShapeDtypeStruct((m, D_MODEL), BF16),
        scratch_shapes=[pltpu.VMEM((w_pa.shape[1], tn), BF16), pltpu.VMEM((w_pb.shape[1], tn), BF16),
                        pltpu.VMEM((w_pm.shape[1], tn), BF16)],
        compiler_params=_cparams(("arbitrary", "arbitrary")),
        name="mix",
    )(u_a, u_b, u_m, w_pa, w_pb, w_pm, proj, proj, proj)


def _out_kernel(a_ref, w_ref, x_ref, o_ref, w_bf):
    @pl.when(pl.program_id(1) == 0)
    def _cast():
        w_bf[...] = w_ref[...].astype(BF16)

    o_ref[...] = x_ref[...] + jnp.dot(a_ref[...], w_bf[...], preferred_element_type=F32)


def _out_proj(mixed, w_out, layer, x, tm, tn):
    m, k = mixed.shape
    n = w_out.shape[2]
    return pl.pallas_call(
        _out_kernel,
        grid=(n // tn, m // tm),
        in_specs=[pl.BlockSpec((tm, k), lambda j, i: (i, 0)),
                  pl.BlockSpec((None, k, tn), lambda j, i: (layer, 0, j)),
                  pl.BlockSpec((tm, tn), lambda j, i: (i, j))],
        out_specs=pl.BlockSpec((tm, tn), lambda j, i: (i, j)),
        out_shape=jax.ShapeDtypeStruct((m, n), F32),
        scratch_shapes=[pltpu.VMEM((k, tn), BF16)],
        compiler_params=_cparams(("arbitrary", "arbitrary")),
        name="out_proj",
    )(mixed, w_out, x)


def _rotate_half_cols(w):
    half = A_ROPE // 2
    return jnp.concatenate([-w[..., half:], w[..., :half]], axis=-1)


def _rope_gain(g_rope):
    half = A_ROPE // 2
    return jnp.concatenate([g_rope, g_rope[half:], g_rope[:half]])


def _prep_w_uq(w_uq):
    w = w_uq.reshape(Q_LORA, A_HEADS, A_QK)
    rope = w[..., A_NOPE:]
    w = jnp.concatenate([w[..., :A_NOPE], rope, _rotate_half_cols(rope)], axis=-1)
    return w.reshape(Q_LORA, A_HEADS * A_PAD).astype(BF16)


def _head_gain(g):
    return jnp.concatenate([g[:A_NOPE], _rope_gain(g[A_NOPE:])])[None, :]


def _tiles(m, seq):
    return dict(
        row=min(512, m),
        mm_m=min(1024, m),
        mix_n=512,
        attn=min(512, seq // 2),
        b_keys=min(256, seq // 4),
        m_rows=min(1024, seq),
    )


def kernel(x, mem, positions, g_pre, w_in, g_q_lat, w_uq, g_kv_lat, w_ukv, g_qn_a, g_kn_a,
           w_pa, w_pb, g_mem, w_mkv, g_qn_m, g_kn_m, w_pm, w_out):
    batch, seq, d = x.shape
    mem_len = mem.shape[1]
    depth = w_in.shape[0]
    m = batch * seq
    t = _tiles(m, seq)

    half = A_ROPE // 2
    freqs = ROPE_THETA ** (-jnp.arange(half, dtype=F32) / half)
    freq_row = jnp.tile(freqs, LANE // half)[None, :]
    pos_b = jnp.broadcast_to(positions.reshape(m, 1).astype(F32), (m, LANE))
    tbl = _rope_table(pos_b, freq_row, t["row"])

    xs = x.reshape(m, d)
    mem2d = mem.reshape(batch * mem_len, d)
    w_in_t = jnp.swapaxes(w_in, 1, 2)
    kv_m_all = _mem_prep(mem2d, g_mem[:, None, :], w_mkv, g_kn_m[:, None, :])
    for l in range(depth):
        h = _norm(xs, g_pre[l][None, :], t["row"])
        proj = _in_proj(h, w_in_t, l, t["mm_m"])

        q_a = _q_prep(proj, g_q_lat[l][None, :], _prep_w_uq(w_uq[l]), _head_gain(g_qn_a[l]),
                      tbl, t["row"])
        k_a, v_a = _kv_prep(proj, g_kv_lat[l][None, :], w_ukv[l].astype(BF16),
                            _head_gain(g_kn_a[l]), tbl, t["row"])
        u_a = _attn_a_flat(q_a, k_a, v_a, proj, batch, seq, t["attn"])
        u_b = _attn_b_flat(proj, batch, seq, t["attn"], t["b_keys"])

        u_m = _attn_m(proj, kv_m_all[l], g_qn_m[l][None, :], seq, mem_len, t["m_rows"])

        mixed = _mix(u_a, u_b, u_m, w_pa, w_pb, w_pm, l, proj, t["mm_m"], t["mix_n"])
        xs = _out_proj(mixed, w_out, l, xs, t["mm_m"], t["mix_n"])
    return xs.reshape(batch, seq, d)
```
